```python
import jax, jax.numpy as jnp
from jax import lax
import numpy as np

D_MODEL = 1024
BATCH = 8
SEQ = 4096
DEPTH = 1
DEC_BATCH = 32
DEC_SEQ = 64
PAST_LEN = 4096

CHUNK = 64
A_HEADS = 8
A_HEAD_DIM = 64
A_WIDTH = A_HEADS * A_HEAD_DIM
DECAY_LORA = 64
ICLR_LORA = 64
RWKV_COLS = 4 * A_WIDTH + DECAY_LORA + ICLR_LORA
B_HEADS = 8
B_KV_HEADS = 2
B_HEAD_DIM = 64
B_GROUP = B_HEADS // B_KV_HEADS
B_WIDTH = B_HEADS * B_HEAD_DIM
B_KV_WIDTH = B_KV_HEADS * B_HEAD_DIM
IDX_HEADS = 8
IDX_DIM = 32
MAX_TOPK = 256
DSA_COLS = B_WIDTH + 2 * B_KV_WIDTH + IDX_HEADS * IDX_DIM + IDX_DIM + IDX_HEADS + B_WIDTH
GATE_COLS = 2 * D_MODEL
N_IN = RWKV_COLS + DSA_COLS + GATE_COLS
NORM_EPS = 1e-6
GN_EPS = 64e-5

kernel_name = "rwkv7_dsa_gated_hybrid_stream_step"


def rmsnorm(x, w):
    x32 = x.astype(jnp.float32)
    y = x32 * lax.rsqrt(jnp.mean(x32 * x32, axis=-1, keepdims=True) + NORM_EPS)
    return (y * w.astype(jnp.float32)).astype(x.dtype)


def rwkv7_branch(p, shift_prev, wkv_prev, shift_mu, decay_w0, decay_up, iclr_a0, iclr_up,
                 k_k, k_a, r_k, gn_w, gn_b):
    f32 = jnp.float32
    bsz, t_len, _ = p.shape
    prev = jnp.concatenate([shift_prev.astype(p.dtype), p[:, :-1]], axis=1)
    ps = p + (prev - p) * shift_mu
    r, k, v, g, wd, ad = jnp.split(
        ps, [A_WIDTH, 2 * A_WIDTH, 3 * A_WIDTH, 4 * A_WIDTH, 4 * A_WIDTH + DECAY_LORA], axis=-1)
    w_log = -jax.nn.softplus(-(decay_w0 + jnp.tanh(wd) @ decay_up).astype(f32)) - 0.5
    decay = jnp.exp(-jnp.exp(w_log))
    a = jax.nn.sigmoid((iclr_a0 + ad @ iclr_up).astype(f32))
    r32, k32, v32 = r.astype(f32), k.astype(f32), v.astype(f32)
    heads = lambda t: t.reshape(bsz, t_len, A_HEADS, A_HEAD_DIM)
    kk = heads(k32 * k_k.astype(f32))
    kk = kk / jnp.maximum(jnp.sqrt(jnp.sum(kk * kk, axis=-1, keepdims=True)), 1e-12)
    k32 = k32 * (1.0 + (a - 1.0) * k_a.astype(f32))
    rh, kh, vh, wh, ah = heads(r32), heads(k32), heads(v32), heads(decay), heads(a)

    def step(S, inp):
        r_t, w_t, k_t, v_t, a_t, b_t = inp
        sa = jnp.einsum('bhvk,bhk->bhv', S, a_t)
        S = S * w_t[:, :, None, :] + sa[..., None] * b_t[:, :, None, :] + v_t[..., None] * k_t[:, :, None, :]
        return S, jnp.einsum('bhvk,bhk->bhv', S, r_t)

    xs = tuple(jnp.moveaxis(t, 1, 0) for t in (rh, wh, kh, vh, -kk, kk * ah))
    s_fin, ys = lax.scan(step, wkv_prev.astype(f32), xs)
    y = jnp.moveaxis(ys, 0, 1)
    mu = jnp.mean(y, axis=-1, keepdims=True)
    var = jnp.mean(jnp.square(y - mu), axis=-1, keepdims=True)
    yn = (y - mu) * lax.rsqrt(var + GN_EPS) * gn_w.astype(f32).reshape(A_HEADS, A_HEAD_DIM) \
        + gn_b.astype(f32).reshape(A_HEADS, A_HEAD_DIM)
    yn = yn + jnp.sum(rh * kh * r_k.astype(f32), axis=-1, keepdims=True) * vh
    out = yn.reshape(bsz, t_len, A_WIDTH).astype(p.dtype) * jax.nn.silu(g)
    return out, s_fin, p[:, -1:]


def dsa_branch(p, past_k, past_v, past_ki):
    f32 = jnp.float32
    bsz, t_len, _ = p.shape
    o1 = B_WIDTH
    o2 = o1 + B_KV_WIDTH
    o3 = o2 + B_KV_WIDTH
    o4 = o3 + IDX_HEADS * IDX_DIM
    o5 = o4 + IDX_DIM
    o6 = o5 + IDX_HEADS
    q, k, v, qi, ki, wi, g = jnp.split(p, [o1, o2, o3, o4, o5, o6], axis=-1)
    q = q.reshape(bsz, t_len, B_KV_HEADS, B_GROUP, B_HEAD_DIM)
    k = k.reshape(bsz, t_len, B_KV_HEADS, B_HEAD_DIM)
    v = v.reshape(bsz, t_len, B_KV_HEADS, B_HEAD_DIM)
    qi = qi.reshape(bsz, t_len, IDX_HEADS, IDX_DIM)
    K = jnp.concatenate([past_k.astype(k.dtype), k], axis=1)
    V = jnp.concatenate([past_v.astype(v.dtype), v], axis=1)
    KI = jnp.concatenate([past_ki.astype(ki.dtype), ki], axis=1)
    s_tot = K.shape[1]
    pos0 = s_tot - t_len
    topk = min(MAX_TOPK, s_tot // 4)
    k_pos = jnp.arange(s_tot, dtype=jnp.int32)
    slopes = jnp.exp2(-(8.0 / B_HEADS) * jnp.arange(1, B_HEADS + 1, dtype=f32)).reshape(B_KV_HEADS, B_GROUP)
    idx_scale = (IDX_HEADS ** -0.5) * (IDX_DIM ** -0.5)
    att_scale = B_HEAD_DIM ** -0.5
    qb_len = min(CHUNK, t_len)
    nb = t_len // qb_len

    def block(args):
        qb, qib, wib, qpb = args
        sc = jax.nn.relu(jnp.einsum('bqhd,bsd->bqhs', qib, KI).astype(f32))
        isc = jnp.einsum('bqhs,bqh->bqs', sc, wib.astype(f32)) * idx_scale
        adm = (k_pos[None, :] // CHUNK) <= (qpb[:, None] // CHUNK)
        vals, sel = lax.top_k(jnp.where(adm[None], isc, -jnp.inf), topk)
        valid = jnp.isfinite(vals)
        Ks = jax.vmap(lambda kb, ib: kb[ib])(K, sel)
        Vs = jax.vmap(lambda vb, ib: vb[ib])(V, sel)
        dist = jnp.abs(qpb[None, :, None] - sel).astype(f32)
        s = jnp.einsum('bqngd,bqknd->bqngk', qb, Ks).astype(f32) * att_scale
        s = s - slopes[None, None, :, :, None] * dist[:, :, None, None, :]
        s = jnp.where(valid[:, :, None, None, :], s, -jnp.inf)
        prob = jax.nn.softmax(s, axis=-1)
        return jnp.einsum('bqngk,bqknd->bqngd', prob.astype(Vs.dtype), Vs)

    q_blocks = q.reshape(bsz, nb, qb_len, B_KV_HEADS, B_GROUP, B_HEAD_DIM).swapaxes(0, 1)
    qi_blocks = qi.reshape(bsz, nb, qb_len, IDX_HEADS, IDX_DIM).swapaxes(0, 1)
    wi_blocks = wi.reshape(bsz, nb, qb_len, IDX_HEADS).swapaxes(0, 1)
    qpos_blocks = (pos0 + jnp.arange(t_len, dtype=jnp.int32)).reshape(nb, qb_len)
    o = lax.map(block, (q_blocks, qi_blocks, wi_blocks, qpos_blocks))
    o = o.swapaxes(0, 1).reshape(bsz, t_len, B_WIDTH).astype(p.dtype)
    return o * jax.nn.silu(g), k, v, ki


def mixer_layer(x, shift_prev, wkv_prev, past_k, past_v, past_ki, norm_w, w_in, shift_mu, decay_w0,
                decay_up, iclr_a0, iclr_up, k_k, k_a, r_k, gn_w, gn_b, w_pa, w_pb, w_o):
    h = rmsnorm(x, norm_w)
    proj = h @ w_in
    pa, pd, pg = jnp.split(proj, [RWKV_COLS, RWKV_COLS + DSA_COLS], axis=-1)
    ya, wkv_new, shift_new = rwkv7_branch(pa, shift_prev, wkv_prev, shift_mu, decay_w0, decay_up,
                                          iclr_a0, iclr_up, k_k, k_a, r_k, gn_w, gn_b)
    yb, k_new, v_new, ki_new = dsa_branch(pd, past_k, past_v, past_ki)
    ga, gb = jnp.split(pg, 2, axis=-1)
    merged = jax.nn.sigmoid(ga) * (ya @ w_pa) + jax.nn.sigmoid(gb) * (yb @ w_pb)
    return x + merged @ w_o, shift_new, wkv_new, k_new, v_new, ki_new


def setup_inputs(seed: int = 0) -> dict:
    key = jax.random.key(seed)
    ks = jax.random.split(key, 24)
    nrm = lambda k, shape, s: jax.random.normal(k, shape, jnp.float32) * s
    L = DEPTH
    return {
        "x_prompt": nrm(ks[0], (BATCH, SEQ, D_MODEL), 1.0),
        "x_sample": nrm(ks[1], (DEC_BATCH, DEC_SEQ, D_MODEL), 1.0),
        "cache_k": nrm(ks[2], (L, DEC_BATCH, PAST_LEN, B_KV_HEADS, B_HEAD_DIM), 1.0),
        "cache_v": nrm(ks[3], (L, DEC_BATCH, PAST_LEN, B_KV_HEADS, B_HEAD_DIM), 1.0),
        "cache_kidx": nrm(ks[4], (L, DEC_BATCH, PAST_LEN, IDX_DIM), 1.0),
        "state_wkv": nrm(ks[5], (L, DEC_BATCH, A_HEADS, A_HEAD_DIM, A_HEAD_DIM), 0.5),
        "state_shift": nrm(ks[6], (L, DEC_BATCH, 1, RWKV_COLS), 1.0),
        "norm_w": 1.0 + nrm(ks[7], (L, D_MODEL), 0.02),
        "w_in": nrm(ks[8], (L, D_MODEL, N_IN), D_MODEL ** -0.5),
        "shift_mu": jax.random.uniform(ks[9], (L, RWKV_COLS), jnp.float32, 0.0, 1.0),
        "decay_w0": jax.random.uniform(ks[10], (L, A_WIDTH), jnp.float32, -6.0, -1.0),
        "decay_up": nrm(ks[11], (L, DECAY_LORA, A_WIDTH), 0.1),
        "iclr_a0": nrm(ks[12], (L, A_WIDTH), 0.1),
        "iclr_up": nrm(ks[13], (L, ICLR_LORA, A_WIDTH), 0.1),
        "k_k": 0.85 + nrm(ks[14], (L, A_WIDTH), 0.02),
        "k_a": 1.0 + nrm(ks[15], (L, A_WIDTH), 0.02),
        "r_k": nrm(ks[16], (L, A_HEADS, A_HEAD_DIM), 0.1),
        "gn_w": 1.0 + nrm(ks[17], (L, A_WIDTH), 0.02),
        "gn_b": nrm(ks[18], (L, A_WIDTH), 0.02),
        "w_pa": nrm(ks[19], (L, A_WIDTH, D_MODEL), A_WIDTH ** -0.5),
        "w_pb": nrm(ks[20], (L, B_WIDTH, D_MODEL), B_WIDTH ** -0.5),
        "w_o": nrm(ks[21], (L, D_MODEL, D_MODEL), D_MODEL ** -0.5),
        "final_norm_w": 1.0 + nrm(ks[22], (D_MODEL,), 0.02),
    }


def reference(x_prompt, x_sample, cache_k, cache_v, cache_kidx, state_wkv, state_shift, norm_w, w_in,
              shift_mu, decay_w0, decay_up, iclr_a0, iclr_up, k_k, k_a, r_k, gn_w, gn_b, w_pa, w_pb,
              w_o, final_norm_w):
    bp = x_prompt.shape[0]
    dt = x_prompt.dtype
    xp, xs = x_prompt, x_sample
    pk, pv, pki, pwkv, psh = [], [], [], [], []
    sk, sv, ski, swkv, ssh = [], [], [], [], []
    for l in range(DEPTH):
        wts = (norm_w[l], w_in[l], shift_mu[l], decay_w0[l], decay_up[l], iclr_a0[l], iclr_up[l],
               k_k[l], k_a[l], r_k[l], gn_w[l], gn_b[l], w_pa[l], w_pb[l], w_o[l])
        xp, sh, wkv, k_new, v_new, ki_new = mixer_layer(
            xp,
            jnp.zeros((bp, 1, RWKV_COLS), dt),
            jnp.zeros((bp, A_HEADS, A_HEAD_DIM, A_HEAD_DIM), jnp.float32),
            jnp.zeros((bp, 0, B_KV_HEADS, B_HEAD_DIM), dt),
            jnp.zeros((bp, 0, B_KV_HEADS, B_HEAD_DIM), dt),
            jnp.zeros((bp, 0, IDX_DIM), dt),
            *wts)
        pk.append(k_new); pv.append(v_new); pki.append(ki_new); pwkv.append(wkv); psh.append(sh)
        xs, sh, wkv, k_new, v_new, ki_new = mixer_layer(
            xs, state_shift[l], state_wkv[l], cache_k[l], cache_v[l], cache_kidx[l], *wts)
        sk.append(k_new); sv.append(v_new); ski.append(ki_new); swkv.append(wkv); ssh.append(sh)
    y_prompt = rmsnorm(xp, final_norm_w)
    y_sample = rmsnorm(xs, final_norm_w)
    return (y_prompt, y_sample,
            jnp.stack(pk), jnp.stack(pv), jnp.stack(pki), jnp.stack(pwkv), jnp.stack(psh),
            jnp.stack(sk), jnp.stack(sv), jnp.stack(ski), jnp.stack(swkv), jnp.stack(ssh))
```

```python
import functools

import jax
import jax.numpy as jnp
from jax import lax
from jax.experimental import pallas as pl
from jax.experimental.pallas import tpu as pltpu

F32 = jnp.float32
BF16 = jnp.bfloat16
I32 = jnp.int32
HIGHEST = lax.Precision.HIGHEST

D_MODEL = 1024
CHUNK = 64
A_HEADS = 8
A_HEAD_DIM = 64
A_WIDTH = 512
LORA = 64
RWKV_COLS = 4 * A_WIDTH + 2 * LORA
B_HEADS = 8
B_KV_HEADS = 2
B_GROUP = 4
B_HEAD_DIM = 64
B_WIDTH = 512
B_KV_WIDTH = 128
IDX_HEADS = 8
IDX_DIM = 32
MAX_TOPK = 256
DSA_COLS = 1576
N_IN = 5800
NORM_EPS = 1e-6
GN_EPS = 64e-5

C_GATES = 0
C_RKVG = 2048
C_Q = 4096
C_GD = 4608
C_QI = 5120
C_KIREP = 5376
C_KD = 5632
C_VD = 5760
C_KIWI = 5888
C_WDAD = 6016
N_PAD = 6144

INT_MIN = -(2 ** 31)
NEG_BIG = -1e30
VMEM_LIMIT = 56 * 1024 * 1024


def _sigmoid(x):
    return 1.0 / (1.0 + jnp.exp(-x))


def _dot_nt(a, b, **kw):
    return lax.dot_general(a, b, (((1,), (1,)), ((), ())), **kw)


def _dot_tn(a, b, **kw):
    return lax.dot_general(a, b, (((0,), (0,)), ((), ())), **kw)


def _permute_w_in(w):
    d0 = RWKV_COLS
    g0 = RWKV_COLS + DSA_COLS
    ki = w[:, d0 + 1024:d0 + 1056]
    cols = [
        w[:, g0:g0 + 2048],
        w[:, 0:2048],
        w[:, d0:d0 + 512],
        w[:, d0 + 1064:d0 + 1576],
        w[:, d0 + 768:d0 + 1024],
        jnp.tile(ki, (1, IDX_HEADS)),
        w[:, d0 + 512:d0 + 640],
        w[:, d0 + 640:d0 + 768],
        w[:, d0 + 1024:d0 + 1064],
        jnp.zeros((w.shape[0], 128 - IDX_DIM - IDX_HEADS), w.dtype),
        w[:, 2048:2176],
    ]
    return jnp.concatenate(cols, axis=1).astype(BF16)


def _proj_kernel(x_ref, nw_ref, w_ref, o_ref):
    x = x_ref[...]
    ms = jnp.mean(x * x, axis=-1, keepdims=True)
    h = (x * lax.rsqrt(ms + NORM_EPS)) * nw_ref[...]
    o_ref[...] = jnp.dot(h.astype(BF16), w_ref[...], preferred_element_type=F32)


def _proj(x2d, norm_w, w_perm):
    m = x2d.shape[0]
    tm = min(512, m)
    tn = N_PAD // 2
    return pl.pallas_call(
        _proj_kernel,
        grid=(N_PAD // tn, m // tm),
        in_specs=[
            pl.BlockSpec((tm, D_MODEL), lambda j, i: (i, 0)),
            pl.BlockSpec((1, D_MODEL), lambda j, i: (0, 0)),
            pl.BlockSpec((D_MODEL, tn), lambda j, i: (0, j)),
        ],
        out_specs=pl.BlockSpec((tm, tn), lambda j, i: (i, j)),
        out_shape=jax.ShapeDtypeStruct((m, N_PAD), F32),
        compiler_params=pltpu.CompilerParams(
            dimension_semantics=("arbitrary", "arbitrary"), vmem_limit_bytes=VMEM_LIMIT),
        name="proj",
    )(x2d, norm_w.reshape(1, D_MODEL), w_perm)


def _rwkv_kernel(p4_ref, pw_ref, sp4_ref, spw_ref, s0_ref, mu4_ref, muw_ref, w0_ref, dup_ref,
                 a0_ref, aup_ref, kk_ref, ka_ref, rk_ref, gnw_ref, gnb_ref, bd_ref,
                 ya_ref, s_ref, c4_ref, cw_ref):
    c = CHUNK
    n = A_HEAD_DIM
    t_idx = pl.program_id(1)

    @pl.when(t_idx == 0)
    def _():
        c4_ref[...] = sp4_ref[0]
        cw_ref[...] = spw_ref[0]
        s_ref[...] = s0_ref[...]

    p4 = p4_ref[...]
    pw = pw_ref[...]
    row = lax.broadcasted_iota(I32, (c, 1), 0)
    prev4 = jnp.where(row == 0, c4_ref[...], pltpu.roll(p4, 1, 0))
    prevw = jnp.where(row == 0, cw_ref[...], pltpu.roll(pw, 1, 0))
    c4_ref[...] = p4[c - 1:c, :]
    cw_ref[...] = pw[c - 1:c, :]
    ps4 = p4 + (prev4 - p4) * mu4_ref[...]
    psw = pw + (prevw - pw) * muw_ref[...]
    r = ps4[:, 0:512]
    k = ps4[:, 512:1024]
    v = ps4[:, 1024:1536]
    g = ps4[:, 1536:2048]
    wd = psw[:, 0:LORA]
    ad = psw[:, LORA:2 * LORA]

    dot_hi = functools.partial(jnp.dot, precision=HIGHEST, preferred_element_type=F32)
    bd = bd_ref[...]

    xw = w0_ref[...] + dot_hi(jnp.tanh(wd), dup_ref[...])
    z = -xw
    softplus = jnp.maximum(z, 0.0) + jnp.log(1.0 + jnp.exp(-jnp.abs(z)))
    lw = -jnp.exp(-softplus - 0.5)
    a = _sigmoid(a0_ref[...] + dot_hi(ad, aup_ref[...]))
    kkr = k * kk_ref[...]
    kkn = kkr / jnp.maximum(jnp.sqrt(dot_hi(kkr * kkr, bd)), 1e-12)
    kmod = k * (1.0 + (a - 1.0) * ka_ref[...])

    col = lax.broadcasted_iota(I32, (c, c), 1)
    rowc = lax.broadcasted_iota(I32, (c, c), 0)
    tri_incl = (col <= rowc).astype(F32)
    tri_strict = (col < rowc).astype(F32)
    eye = (col == rowc).astype(F32)
    cum = dot_hi(tri_incl, lw)
    pdec = jnp.exp(cum)
    pinv = jnp.exp(-cum)
    rt = r * pdec
    at = -kkn * jnp.exp(cum - lw)
    bt = kkn * a * pinv
    kt = kmod * pinv
    pc = pdec[c - 1:c, :]

    ys = []
    for h in range(A_HEADS):
        sl = slice(h * n, (h + 1) * n)
        at_h, rt_h, bt_h, kt_h, v_h = at[:, sl], rt[:, sl], bt[:, sl], kt[:, sl], v[:, sl]
        lhs = jnp.concatenate([at_h, rt_h], axis=0)
        rhs = jnp.concatenate([bt_h, kt_h], axis=0)
        amat = _dot_nt(lhs, rhs, precision=HIGHEST, preferred_element_type=F32)
        a_ab = amat[0:c, 0:c] * tri_strict
        a_ak = amat[0:c, c:2 * c] * tri_strict
        a_rb = amat[c:2 * c, 0:c] * tri_incl
        a_rk = amat[c:2 * c, c:2 * c] * tri_incl
        x = a_ab
        tinv = eye + x
        for _ in range(5):
            x = dot_hi(x, x)
            tinv = tinv + dot_hi(tinv, x)
        s_h = s_ref[0, h]
        u = dot_hi(tinv, _dot_nt(at_h, s_h, precision=HIGHEST, preferred_element_type=F32)
                   + dot_hi(a_ak, v_h))
        y_h = (_dot_nt(rt_h, s_h, precision=HIGHEST, preferred_element_type=F32)
               + dot_hi(a_rb, u) + dot_hi(a_rk, v_h))
        uv = jnp.concatenate([u, v_h], axis=0)
        s_new = (s_h + _dot_tn(uv, rhs, precision=HIGHEST, preferred_element_type=F32)) * pc[:, sl]
        s_ref[0, h] = s_new
        ys.append(y_h)
    y = jnp.concatenate(ys, axis=1)

    inv_n = 1.0 / n
    mean = dot_hi(y, bd) * inv_n
    dlt = y - mean
    var = dot_hi(dlt * dlt, bd) * inv_n
    yn = dlt * lax.rsqrt(var + GN_EPS) * gnw_ref[...] + gnb_ref[...]
    yn = yn + dot_hi(r * kmod * rk_ref[...], bd) * v
    ya_ref[...] = yn * (g * _sigmoid(g))


def _rwkv(proj, bsz, t_len, shift4, shiftw, wkv_prev, prm):
    nt = t_len // CHUNK
    row1 = lambda width: pl.BlockSpec((1, width), lambda b, t: (0, 0))
    idx = lax.broadcasted_iota(I32, (A_WIDTH, A_WIDTH), 0) // A_HEAD_DIM
    bd = (idx == idx.T).astype(F32)
    return pl.pallas_call(
        _rwkv_kernel,
        grid=(bsz, nt),
        in_specs=[
            pl.BlockSpec((CHUNK, 2048), lambda b, t: (b * nt + t, C_RKVG // 2048)),
            pl.BlockSpec((CHUNK, 128), lambda b, t: (b * nt + t, C_WDAD // 128)),
            pl.BlockSpec((1, 1, 2048), lambda b, t: (b, 0, 0)),
            pl.BlockSpec((1, 1, 128), lambda b, t: (b, 0, 0)),
            pl.BlockSpec((1, A_HEADS, A_HEAD_DIM, A_HEAD_DIM), lambda b, t: (b, 0, 0, 0)),
            row1(2048), row1(128), row1(A_WIDTH),
            pl.BlockSpec((LORA, A_WIDTH), lambda b, t: (0, 0)),
            row1(A_WIDTH),
            pl.BlockSpec((LORA, A_WIDTH), lambda b, t: (0, 0)),
            row1(A_WIDTH), row1(A_WIDTH), row1(A_WIDTH), row1(A_WIDTH), row1(A_WIDTH),
            pl.BlockSpec((A_WIDTH, A_WIDTH), lambda b, t: (0, 0)),
        ],
        out_specs=[
            pl.BlockSpec((CHUNK, A_WIDTH), lambda b, t: (b * nt + t, 0)),
            pl.BlockSpec((1, A_HEADS, A_HEAD_DIM, A_HEAD_DIM), lambda b, t: (b, 0, 0, 0)),
        ],
        out_shape=[
            jax.ShapeDtypeStruct((bsz * t_len, A_WIDTH), F32),
            jax.ShapeDtypeStruct((bsz, A_HEADS, A_HEAD_DIM, A_HEAD_DIM), F32),
        ],
        scratch_shapes=[pltpu.VMEM((1, 2048), F32), pltpu.VMEM((1, 128), F32)],
        compiler_params=pltpu.CompilerParams(
            dimension_semantics=("arbitrary", "arbitrary"), vmem_limit_bytes=VMEM_LIMIT),
        name="rwkv",
    )(proj, proj, shift4, shiftw, wkv_prev, prm["mu4"], prm["muw"], prm["w0"], prm["dup"],
      prm["a0"], prm["aup"], prm["kk"], prm["ka"], prm["rk"], prm["gnw"], prm["gnb"], bd)


def _dsa_kernel(q_ref, qi_ref, kiwi_ref, k_ref, v_ref, kirep_ref, o_ref, keys_ref, *, pos0, kb_w, topk):
    tq = CHUNK
    qt = pl.program_id(1)
    n_adm = pos0 + (qt + 1) * tq
    nkb = (n_adm + kb_w - 1) // kb_w
    idx_scale = (IDX_HEADS ** -0.5) * (IDX_DIM ** -0.5)
    att_scale = B_HEAD_DIM ** -0.5
    topk = float(topk)

    qi = qi_ref[...]
    wi = kiwi_ref[:, IDX_DIM:IDX_DIM + IDX_HEADS]
    lane_q = jnp.right_shift(lax.broadcasted_iota(I32, (1, IDX_HEADS * IDX_DIM), 1), 5)
    qis = jnp.concatenate([jnp.where(lane_q == h, qi, 0.0) for h in range(IDX_HEADS)],
                          axis=0).astype(BF16)
    wis = jnp.concatenate([wi[:, h:h + 1] for h in range(IDX_HEADS)], axis=0)
    lane_k = lax.broadcasted_iota(I32, (1, kb_w), 1)

    def score_block(kb, carry):
        off = pl.multiple_of(kb * kb_w, kb_w)
        kir = kirep_ref[0, pl.ds(off, kb_w), :].astype(BF16)
        s = _dot_nt(qis, kir, preferred_element_type=F32)
        s = jnp.maximum(s, 0.0) * wis
        isc = s[0:tq]
        for h in range(1, IDX_HEADS):
            isc = isc + s[h * tq:(h + 1) * tq]
        isc = isc * idx_scale
        bits = pltpu.bitcast(isc, I32)
        key = jnp.where(bits < 0, bits ^ 0x7FFFFFFF, bits)
        key = jnp.where(key == -1, 0, key)
        key = jnp.where(off + lane_k < n_adm, key, INT_MIN)
        keys_ref[:, pl.ds(off, kb_w)] = key
        return carry

    lax.fori_loop(0, nkb, score_block, 0)

    def count_ge(cand):
        def body(kb, acc):
            off = pl.multiple_of(kb * kb_w, kb_w)
            hit = jnp.where(keys_ref[:, pl.ds(off, kb_w)] >= cand, 1.0, 0.0)
            part = hit[:, 0:128]
            for j in range(1, kb_w // 128):
                part = part + hit[:, j * 128:(j + 1) * 128]
            return acc + part
        acc = lax.fori_loop(0, nkb, body, jnp.zeros((tq, 128), F32))
        return jnp.sum(acc, axis=1, keepdims=True)

    c0 = count_ge(jnp.zeros((tq, 1), I32))
    t0 = jnp.where(c0 >= topk, 0, INT_MIN).astype(I32)

    def bit_step(i, t):
        cand = t | jnp.left_shift(jnp.int32(1), 30 - i)
        return jnp.where(count_ge(cand) >= topk, cand, t)

    thr = lax.fori_loop(0, 31, bit_step, t0)
    thr = jnp.maximum(thr, INT_MIN + 1)
    n_ge = count_ge(thr)
    n_gt = count_ge(thr + 1)
    n_tie_take = topk - n_gt

    @pl.when(jnp.max(n_ge) > topk)
    def _():
        ri = lax.broadcasted_iota(I32, (kb_w, kb_w), 0)
        ci = lax.broadcasted_iota(I32, (kb_w, kb_w), 1)
        upper = (ri <= ci).astype(BF16)

        def body(kb, seen):
            off = pl.multiple_of(kb * kb_w, kb_w)
            kblk = keys_ref[:, pl.ds(off, kb_w)]
            tie = kblk == thr
            rank = seen + jnp.dot(jnp.where(tie, 1.0, 0.0).astype(BF16), upper,
                                  preferred_element_type=F32)
            keys_ref[:, pl.ds(off, kb_w)] = jnp.where(tie & (rank > n_tie_take), INT_MIN, kblk)
            return rank[:, kb_w - 1:kb_w]

        lax.fori_loop(0, nkb, body, jnp.zeros((tq, 1), F32))

    q = q_ref[...]
    rows = B_GROUP * tq
    qs, slope_cols = [], []
    for n in range(B_KV_HEADS):
        heads = [n * B_GROUP + g for g in range(B_GROUP)]
        qs.append((jnp.concatenate([q[:, h * B_HEAD_DIM:(h + 1) * B_HEAD_DIM] for h in heads], axis=0)
                   * att_scale).astype(BF16))
        slope_cols.append(jnp.concatenate(
            [jnp.full((tq, 1), 2.0 ** (-(8.0 / B_HEADS) * (h + 1)), F32) for h in heads], axis=0))
    qpos = pos0 + qt * tq + lax.broadcasted_iota(I32, (tq, 1), 0)

    def attn_block(kb, carry):
        off = pl.multiple_of(kb * kb_w, kb_w)
        sel = keys_ref[:, pl.ds(off, kb_w)] >= thr
        dist = jnp.abs(qpos - (off + lane_k)).astype(F32)
        sel4 = jnp.concatenate([sel] * B_GROUP, axis=0)
        dist4 = jnp.concatenate([dist] * B_GROUP, axis=0)
        kblk = k_ref[0, pl.ds(off, kb_w), :].astype(BF16)
        vblk = v_ref[0, pl.ds(off, kb_w), :].astype(BF16)
        out = []
        for n in range(B_KV_HEADS):
            m_old, l_old, acc_old = carry[n]
            ksl = slice(n * B_HEAD_DIM, (n + 1) * B_HEAD_DIM)
            s = _dot_nt(qs[n], kblk[:, ksl], preferred_element_type=F32)
            s = jnp.where(sel4, s - slope_cols[n] * dist4, NEG_BIG)
            m_new = jnp.maximum(m_old, jnp.max(s, axis=1, keepdims=True))
            alpha = jnp.exp(m_old - m_new)
            p = jnp.exp(s - m_new)
            l_new = alpha * l_old + jnp.sum(p, axis=1, keepdims=True)
            acc_new = alpha * acc_old + jnp.dot(p.astype(BF16), vblk[:, ksl],
                                                preferred_element_type=F32)
            out.append((m_new, l_new, acc_new))
        return tuple(out)

    init = tuple((jnp.full((rows, 1), NEG_BIG, F32), jnp.zeros((rows, 1), F32),
                  jnp.zeros((rows, B_HEAD_DIM), F32)) for _ in range(B_KV_HEADS))
    fin = lax.fori_loop(0, nkb, attn_block, init)
    pieces = []
    for n in range(B_KV_HEADS):
        _, l_fin, acc_fin = fin[n]
        o_n = acc_fin / l_fin
        pieces += [o_n[g * tq:(g + 1) * tq] for g in range(B_GROUP)]
    o_ref[...] = jnp.concatenate(pieces, axis=1)


def _dsa(proj, bsz, t_len, k_all, v_all, kirep_all, kv_col, kirep_col, pos0):
    kb_w = 256
    nq = t_len // CHUNK
    s_pad = k_all.shape[1]
    assert s_pad % kb_w == 0 and pos0 + t_len <= s_pad and pos0 % CHUNK == 0
    return pl.pallas_call(
        functools.partial(_dsa_kernel, pos0=pos0, kb_w=kb_w,
                          topk=min(MAX_TOPK, (pos0 + t_len) // 4)),
        grid=(bsz, nq),
        in_specs=[
            pl.BlockSpec((CHUNK, 512), lambda b, t: (b * nq + t, C_Q // 512)),
            pl.BlockSpec((CHUNK, 256), lambda b, t: (b * nq + t, C_QI // 256)),
            pl.BlockSpec((CHUNK, 128), lambda b, t: (b * nq + t, C_KIWI // 128)),
            pl.BlockSpec((1, s_pad, 128), lambda b, t: (b, 0, kv_col[0])),
            pl.BlockSpec((1, s_pad, 128), lambda b, t: (b, 0, kv_col[1])),
            pl.BlockSpec((1, s_pad, 256), lambda b, t: (b, 0, kirep_col)),
        ],
        out_specs=pl.BlockSpec((CHUNK, B_WIDTH), lambda b, t: (b * nq + t, 0)),
        out_shape=jax.ShapeDtypeStruct((bsz * t_len, B_WIDTH), F32),
        scratch_shapes=[pltpu.VMEM((CHUNK, s_pad), I32)],
        compiler_params=pltpu.CompilerParams(
            dimension_semantics=("arbitrary", "arbitrary"), vmem_limit_bytes=VMEM_LIMIT),
        name="dsa",
    )(proj, proj, proj, k_all, v_all, kirep_all)


def _merge_kernel(x_ref, ya_ref, yb_ref, gd_ref, gab_ref, wpa_ref, wpb_ref, wo_ref, fnw_ref, o_ref):
    gd = gd_ref[...]
    yb = yb_ref[...] * (gd * _sigmoid(gd))
    pa = jnp.dot(ya_ref[...].astype(BF16), wpa_ref[...], preferred_element_type=F32)
    pb = jnp.dot(yb.astype(BF16), wpb_ref[...], preferred_element_type=F32)
    merged = _sigmoid(gab_ref[:, 0:D_MODEL]) * pa + _sigmoid(gab_ref[:, D_MODEL:2 * D_MODEL]) * pb
    out = x_ref[...] + jnp.dot(merged.astype(BF16), wo_ref[...], preferred_element_type=F32)
    ms = jnp.mean(out * out, axis=-1, keepdims=True)
    o_ref[...] = (out * lax.rsqrt(ms + NORM_EPS)) * fnw_ref[...]


def _merge(x2d, ya, yb, proj, w_pa, w_pb, w_o, final_w):
    m = x2d.shape[0]
    tm = min(512, m)
    full = lambda shape: pl.BlockSpec(shape, lambda i: (0, 0))
    return pl.pallas_call(
        _merge_kernel,
        grid=(m // tm,),
        in_specs=[
            pl.BlockSpec((tm, D_MODEL), lambda i: (i, 0)),
            pl.BlockSpec((tm, A_WIDTH), lambda i: (i, 0)),
            pl.BlockSpec((tm, B_WIDTH), lambda i: (i, 0)),
            pl.BlockSpec((tm, 512), lambda i: (i, C_GD // 512)),
            pl.BlockSpec((tm, 2048), lambda i: (i, C_GATES // 2048)),
            full((A_WIDTH, D_MODEL)), full((B_WIDTH, D_MODEL)), full((D_MODEL, D_MODEL)),
            full((1, D_MODEL)),
        ],
        out_specs=pl.BlockSpec((tm, D_MODEL), lambda i: (i, 0)),
        out_shape=jax.ShapeDtypeStruct((m, D_MODEL), F32),
        compiler_params=pltpu.CompilerParams(
            dimension_semantics=("arbitrary",), vmem_limit_bytes=VMEM_LIMIT),
        name="merge",
    )(x2d, ya, yb, proj, proj, w_pa.astype(BF16), w_pb.astype(BF16), w_o.astype(BF16),
      final_w.reshape(1, D_MODEL))


def _rwkv_order(row):
    return row[..., 0:2048], row[..., 2048:2176]


def _mixer(x, shift_prev, wkv_prev, past_k, past_v, past_ki, w_perm, norm_w, prm, w_pa, w_pb, w_o,
           final_w):
    bsz, t_len, _ = x.shape
    x2d = x.reshape(bsz * t_len, D_MODEL)
    proj = _proj(x2d, norm_w, w_perm)
    proj3 = proj.reshape(bsz, t_len, N_PAD)

    shift4, shiftw = _rwkv_order(shift_prev)
    ya, wkv_new = _rwkv(proj, bsz, t_len, shift4, shiftw, wkv_prev, prm)

    k_new = proj3[:, :, C_KD:C_KD + 128]
    v_new = proj3[:, :, C_VD:C_VD + 128]
    ki_new = proj3[:, :, C_KIWI:C_KIWI + IDX_DIM]
    past_len = 0 if past_k is None else past_k.shape[1]
    if past_len == 0:
        yb = _dsa(proj, bsz, t_len, proj3, proj3, proj3, (C_KD // 128, C_VD // 128), C_KIREP // 256, 0)
    else:
        s_tot = past_len + t_len
        s_pad = -(-s_tot // 256) * 256
        pad = lambda a: jnp.pad(a, ((0, 0), (0, s_pad - s_tot), (0, 0)))
        k_all = pad(jnp.concatenate([past_k.reshape(bsz, past_len, 128), k_new], axis=1))
        v_all = pad(jnp.concatenate([past_v.reshape(bsz, past_len, 128), v_new], axis=1))
        kirep_all = pad(jnp.concatenate(
            [jnp.tile(past_ki, (1, 1, IDX_HEADS)), proj3[:, :, C_KIREP:C_KIREP + 256]], axis=1))
        yb = _dsa(proj, bsz, t_len, k_all, v_all, kirep_all, (0, 0), 0, past_len)

    y = _merge(x2d, ya, yb, proj, w_pa, w_pb, w_o, final_w).reshape(bsz, t_len, D_MODEL)
    last = proj3[:, t_len - 1:t_len, :]
    shift_new = jnp.concatenate([last[..., C_RKVG:C_RKVG + 2048], last[..., C_WDAD:C_WDAD + 128]], axis=-1)
    kv_shape = (bsz, t_len, B_KV_HEADS, B_HEAD_DIM)
    return y, k_new.reshape(kv_shape), v_new.reshape(kv_shape), ki_new, wkv_new, shift_new


def kernel(x_prompt, x_sample, cache_k, cache_v, cache_kidx, state_wkv, state_shift, norm_w, w_in,
           shift_mu, decay_w0, decay_up, iclr_a0, iclr_up, k_k, k_a, r_k, gn_w, gn_b, w_pa, w_pb,
           w_o, final_norm_w):
    assert w_in.shape[0] == 1, "the final norm is fused into the (single) layer's merge kernel"
    bp = x_prompt.shape[0]
    w_perm = _permute_w_in(w_in[0])
    mu4, muw = _rwkv_order(shift_mu[0].reshape(1, RWKV_COLS))
    row = lambda a: a.reshape(1, A_WIDTH)
    prm = dict(mu4=mu4, muw=muw, w0=row(decay_w0[0]), dup=decay_up[0], a0=row(iclr_a0[0]),
               aup=iclr_up[0], kk=row(k_k[0]), ka=row(k_a[0]), rk=row(r_k[0]), gnw=row(gn_w[0]),
               gnb=row(gn_b[0]))
    common = (w_perm, norm_w[0], prm, w_pa[0], w_pb[0], w_o[0], final_norm_w)
    yp, kp, vp, kip, wkvp, shp = _mixer(
        x_prompt, jnp.zeros((bp, 1, RWKV_COLS), F32),
        jnp.zeros((bp, A_HEADS, A_HEAD_DIM, A_HEAD_DIM), F32), None, None, None, *common)
    ys, ks, vs, kis, wkvs, shs = _mixer(
        x_sample, state_shift[0], state_wkv[0], cache_k[0], cache_v[0], cache_kidx[0], *common)
    st = lambda a: a[None]
    return (yp, ys, st(kp), st(vp), st(kip), st(wkvp), st(shp),
            st(ks), st(vs), st(kis), st(wkvs), st(shs))
```

```python
import functools

import jax
import jax.numpy as jnp
from jax import lax
from jax.experimental import pallas as pl
from jax.experimental.pallas import tpu as pltpu

F32 = jnp.float32
BF16 = jnp.bfloat16
I32 = jnp.int32
HIGHEST = lax.Precision.HIGHEST

D_MODEL = 1024
CHUNK = 64
A_HEADS = 8
A_HEAD_DIM = 64
A_WIDTH = 512
LORA = 64
RWKV_COLS = 4 * A_WIDTH + 2 * LORA
B_HEADS = 8
B_KV_HEADS = 2
B_GROUP = 4
B_HEAD_DIM = 64
B_WIDTH = 512
B_KV_WIDTH = 128
IDX_HEADS = 8
IDX_DIM = 32
MAX_TOPK = 256
DSA_COLS = 1576
N_IN = 5800
NORM_EPS = 1e-6
GN_EPS = 64e-5

C_GATES = 0
C_RKVG = 2048
C_Q = 4096
C_GD = 4608
C_QI = 5120
C_KIREP = 5376
C_KD = 5632
C_VD = 5760
C_KIWI = 5888
C_WDAD = 6016
N_PAD = 6144

INT_MIN = -(2 ** 31)
NEG_BIG = -1e30
VMEM_LIMIT = 56 * 1024 * 1024


def _sigmoid(x):
    return 1.0 / (1.0 + jnp.exp(-x))


def _dot_nt(a, b, **kw):
    return lax.dot_general(a, b, (((1,), (1,)), ((), ())), **kw)


def _dot_tn(a, b, **kw):
    return lax.dot_general(a, b, (((0,), (0,)), ((), ())), **kw)


def _permute_w_in(w):
    d0 = RWKV_COLS
    g0 = RWKV_COLS + DSA_COLS
    ki = w[:, d0 + 1024:d0 + 1056]
    cols = [
        w[:, g0:g0 + 2048],
        w[:, 0:2048],
        w[:, d0:d0 + 512],
        w[:, d0 + 1064:d0 + 1576],
        w[:, d0 + 768:d0 + 1024],
        jnp.tile(ki, (1, IDX_HEADS)),
        w[:, d0 + 512:d0 + 640],
        w[:, d0 + 640:d0 + 768],
        w[:, d0 + 1024:d0 + 1064],
        jnp.zeros((w.shape[0], 128 - IDX_DIM - IDX_HEADS), w.dtype),
        w[:, 2048:2176],
    ]
    return jnp.concatenate(cols, axis=1).astype(BF16)


def _proj_kernel(x_ref, nw_ref, w_ref, o_ref):
    x = x_ref[...]
    ms = jnp.mean(x * x, axis=-1, keepdims=True)
    h = (x * lax.rsqrt(ms + NORM_EPS)) * nw_ref[...]
    o_ref[...] = jnp.dot(h.astype(BF16), w_ref[...], preferred_element_type=F32)


def _proj(x2d, norm_w, w_perm):
    m = x2d.shape[0]
    tm = min(512, m)
    tn = N_PAD // 2
    return pl.pallas_call(
        _proj_kernel,
        grid=(N_PAD // tn, m // tm),
        in_specs=[
            pl.BlockSpec((tm, D_MODEL), lambda j, i: (i, 0)),
            pl.BlockSpec((1, D_MODEL), lambda j, i: (0, 0)),
            pl.BlockSpec((D_MODEL, tn), lambda j, i: (0, j)),
        ],
        out_specs=pl.BlockSpec((tm, tn), lambda j, i: (i, j)),
        out_shape=jax.ShapeDtypeStruct((m, N_PAD), F32),
        compiler_params=pltpu.CompilerParams(
            dimension_semantics=("arbitrary", "arbitrary"), vmem_limit_bytes=VMEM_LIMIT),
        name="proj",
    )(x2d, norm_w.reshape(1, D_MODEL), w_perm)


def _split_bf16(x, terms):
    pieces = []
    for _ in range(terms):
        piece = x.astype(BF16)
        pieces.append(piece)
        x = x - piece.astype(F32)
    return pieces


def _dot_exact_rhs(a, b_bf16, terms):
    acc = None
    for piece in _split_bf16(a, terms):
        d = jnp.dot(piece, b_bf16, preferred_element_type=F32)
        acc = d if acc is None else acc + d
    return acc


def _dot_3pass(a, b):
    ah, al = _split_bf16(a, 2)
    bh, bl = _split_bf16(b, 2)
    dot = functools.partial(jnp.dot, preferred_element_type=F32)
    return dot(ah, bh) + dot(ah, bl) + dot(al, bh)


def _rwkv_kernel(p4_ref, pw_ref, sp4_ref, spw_ref, s0_ref, mu4_ref, muw_ref, w0_ref, dup_ref,
                 a0_ref, aup_ref, kk_ref, ka_ref, rk_ref, gnw_ref, gnb_ref, bd_ref, tri_ref,
                 ya_ref, sout_ref, c4_ref, cw_ref, sbd_ref, *, nch):
    c = CHUNK
    n = A_HEAD_DIM
    rows = nch * c
    n_pairs = A_HEADS // 2
    t_idx = pl.program_id(1)
    lane = lax.broadcasted_iota(I32, (c, 2 * n), 1)
    lo_half = lane < n
    row_c = lax.broadcasted_iota(I32, (c, 2 * n), 0)
    pos_in_head = jnp.where(lo_half, lane, lane - n)
    tri_strict = (pos_in_head < row_c).astype(F32)
    tri_incl = (pos_in_head <= row_c).astype(F32)
    eye2 = (pos_in_head == row_c).astype(F32)
    lane_sq = lax.broadcasted_iota(I32, (2 * n, 2 * n), 1)
    row_sq = lax.broadcasted_iota(I32, (2 * n, 2 * n), 0)
    same_head = (lane_sq < n) == (row_sq < n)
    dot = functools.partial(jnp.dot, preferred_element_type=F32)

    def bdiag(x):
        zero = jnp.zeros_like(x)
        return jnp.concatenate([jnp.where(lo_half, x, zero), jnp.where(lo_half, zero, x)], axis=0)

    @pl.when(t_idx == 0)
    def _():
        c4_ref[...] = sp4_ref[0]
        cw_ref[...] = spw_ref[0]
        zeros = jnp.zeros((n, n), F32)
        for j in range(n_pairs):
            sbd_ref[j] = jnp.concatenate(
                [jnp.concatenate([s0_ref[0, 2 * j], zeros], axis=1),
                 jnp.concatenate([zeros, s0_ref[0, 2 * j + 1]], axis=1)], axis=0)

    p4 = p4_ref[...]
    pw = pw_ref[...]
    row = lax.broadcasted_iota(I32, (rows, 1), 0)
    prev4 = jnp.where(row == 0, c4_ref[...], pltpu.roll(p4, 1, 0))
    prevw = jnp.where(row == 0, cw_ref[...], pltpu.roll(pw, 1, 0))
    c4_ref[...] = p4[rows - 1:rows, :]
    cw_ref[...] = pw[rows - 1:rows, :]
    ps4 = p4 + (prev4 - p4) * mu4_ref[...]
    psw = pw + (prevw - pw) * muw_ref[...]
    r = ps4[:, 0:512]
    k = ps4[:, 512:1024]
    v = ps4[:, 1024:1536]
    g = ps4[:, 1536:2048]
    wd = psw[:, 0:LORA]
    ad = psw[:, LORA:2 * LORA]
    bd = bd_ref[...]

    xw = w0_ref[...] + _dot_3pass(jnp.tanh(wd), dup_ref[...])
    z = -xw
    softplus = jnp.maximum(z, 0.0) + jnp.log(1.0 + jnp.exp(-jnp.abs(z)))
    lw = -jnp.exp(-softplus - 0.5)
    a = _sigmoid(a0_ref[...] + _dot_3pass(ad, aup_ref[...]))
    kkr = k * kk_ref[...]
    kkn = kkr / jnp.maximum(jnp.sqrt(_dot_exact_rhs(kkr * kkr, bd, 2)), 1e-12)
    kmod = k * (1.0 + (a - 1.0) * ka_ref[...])

    cum = None
    for piece in _split_bf16(lw, 3):
        d = dot(tri_ref[...], piece)
        cum = d if cum is None else cum + d
    pdec = jnp.exp(cum)
    pinv = jnp.exp(-cum)
    rt = (r * pdec).astype(BF16)
    at = (-kkn * jnp.exp(cum - lw)).astype(BF16)
    bt = (kkn * a * pinv).astype(BF16)
    kt = (kmod * pinv).astype(BF16)
    vb = v.astype(BF16)

    tiles = [(ci, j) for ci in range(nch) for j in range(n_pairs)]
    rsl = lambda ci: slice(ci * c, (ci + 1) * c)
    lsl = lambda j: slice(j * 2 * n, (j + 1) * 2 * n)
    lhs, bk, a_ak_rk, a_rb, xs, tinv = {}, {}, {}, {}, {}, {}
    for ci, j in tiles:
        rs, ls = rsl(ci), lsl(j)
        lhs[ci, j] = jnp.concatenate([at[rs, ls], rt[rs, ls]], axis=0)
        bk[ci, j] = jnp.concatenate([bt[rs, ls], kt[rs, ls]], axis=0)
        amat = _dot_nt(lhs[ci, j], jnp.concatenate([bdiag(bt[rs, ls]), bdiag(kt[rs, ls])], axis=0),
                       preferred_element_type=F32)
        xs[ci, j] = amat[0:c, 0:2 * n] * tri_strict
        a_ak_rk[ci, j] = jnp.concatenate([amat[0:c, 2 * n:4 * n] * tri_strict,
                                          amat[c:2 * c, 2 * n:4 * n] * tri_incl], axis=0).astype(BF16)
        a_rb[ci, j] = (amat[c:2 * c, 0:2 * n] * tri_incl).astype(BF16)
        tinv[ci, j] = eye2 + xs[ci, j]
    for _ in range(5):
        for t in tiles:
            xb = xs[t].astype(BF16)
            xs[t] = dot(xb, bdiag(xb))
        for t in tiles:
            tinv[t] = tinv[t] + dot(tinv[t].astype(BF16), bdiag(xs[t].astype(BF16)))
    akv = {}
    for ci, j in tiles:
        tinv[ci, j] = tinv[ci, j].astype(BF16)
        akv[ci, j] = dot(a_ak_rk[ci, j], bdiag(vb[rsl(ci), lsl(j)]))

    pairs = range(n_pairs)
    s_pair = [sbd_ref[j] for j in pairs]
    y_chunks = []
    for ci in range(nch):
        from_state = [_dot_nt(lhs[ci, j], s_pair[j].astype(BF16), preferred_element_type=F32)
                      for j in pairs]
        u = [dot(tinv[ci, j], bdiag((from_state[j][0:c] + akv[ci, j][0:c]).astype(BF16)))
             for j in pairs]
        uv_t = [jnp.transpose(jnp.concatenate([u[j], v[rsl(ci), lsl(j)]], axis=0)).astype(BF16)
                for j in pairs]
        upd = [dot(uv_t[j], bk[ci, j]) for j in pairs]
        s_pair = [(s_pair[j] + jnp.where(same_head, upd[j], 0.0))
                  * pdec[(ci + 1) * c - 1:(ci + 1) * c, lsl(j)] for j in pairs]
        y_chunks.append(jnp.concatenate(
            [from_state[j][c:2 * c] + akv[ci, j][c:2 * c] + dot(a_rb[ci, j], bdiag(u[j].astype(BF16)))
             for j in pairs], axis=1))
    for j in pairs:
        sbd_ref[j] = s_pair[j]
    y = y_chunks[0] if nch == 1 else jnp.concatenate(y_chunks, axis=0)

    inv_n = 1.0 / n
    mean = _dot_exact_rhs(y, bd, 2) * inv_n
    dlt = y - mean
    var = _dot_exact_rhs(dlt * dlt, bd, 2) * inv_n
    yn = dlt * lax.rsqrt(var + GN_EPS) * gnw_ref[...] + gnb_ref[...]
    yn = yn + _dot_exact_rhs(r * kmod * rk_ref[...], bd, 2) * v
    ya_ref[...] = yn * (g * _sigmoid(g))

    @pl.when(t_idx == pl.num_programs(1) - 1)
    def _():
        for j in range(n_pairs):
            s_pair = sbd_ref[j]
            sout_ref[0, 2 * j] = s_pair[0:n, 0:n]
            sout_ref[0, 2 * j + 1] = s_pair[n:2 * n, n:2 * n]


def _rwkv(proj, bsz, t_len, shift4, shiftw, wkv_prev, prm):
    nch = min(4, t_len // CHUNK)
    rows = nch * CHUNK
    nt = t_len // rows
    row1 = lambda width: pl.BlockSpec((1, width), lambda b, t: (0, 0))
    head = lax.broadcasted_iota(I32, (A_WIDTH, A_WIDTH), 0) // A_HEAD_DIM
    bd = (head == head.T).astype(BF16)
    ti = lax.broadcasted_iota(I32, (rows, rows), 0)
    si = lax.broadcasted_iota(I32, (rows, rows), 1)
    tri = ((ti // CHUNK == si // CHUNK) & (si <= ti)).astype(BF16)
    return pl.pallas_call(
        functools.partial(_rwkv_kernel, nch=nch),
        grid=(bsz, nt),
        in_specs=[
            pl.BlockSpec((rows, 2048), lambda b, t: (b * nt + t, C_RKVG // 2048)),
            pl.BlockSpec((rows, 128), lambda b, t: (b * nt + t, C_WDAD // 128)),
            pl.BlockSpec((1, 1, 2048), lambda b, t: (b, 0, 0)),
            pl.BlockSpec((1, 1, 128), lambda b, t: (b, 0, 0)),
            pl.BlockSpec((1, A_HEADS, A_HEAD_DIM, A_HEAD_DIM), lambda b, t: (b, 0, 0, 0)),
            row1(2048), row1(128), row1(A_WIDTH),
            pl.BlockSpec((LORA, A_WIDTH), lambda b, t: (0, 0)),
            row1(A_WIDTH),
            pl.BlockSpec((LORA, A_WIDTH), lambda b, t: (0, 0)),
            row1(A_WIDTH), row1(A_WIDTH), row1(A_WIDTH), row1(A_WIDTH), row1(A_WIDTH),
            pl.BlockSpec((A_WIDTH, A_WIDTH), lambda b, t: (0, 0)),
            pl.BlockSpec((rows, rows), lambda b, t: (0, 0)),
        ],
        out_specs=[
            pl.BlockSpec((rows, A_WIDTH), lambda b, t: (b * nt + t, 0)),
            pl.BlockSpec((1, A_HEADS, A_HEAD_DIM, A_HEAD_DIM), lambda b, t: (b, 0, 0, 0)),
        ],
        out_shape=[
            jax.ShapeDtypeStruct((bsz * t_len, A_WIDTH), F32),
            jax.ShapeDtypeStruct((bsz, A_HEADS, A_HEAD_DIM, A_HEAD_DIM), F32),
        ],
        scratch_shapes=[pltpu.VMEM((1, 2048), F32), pltpu.VMEM((1, 128), F32),
                        pltpu.VMEM((A_HEADS // 2, 2 * A_HEAD_DIM, 2 * A_HEAD_DIM), F32)],
        compiler_params=pltpu.CompilerParams(
            dimension_semantics=("arbitrary", "arbitrary"), vmem_limit_bytes=VMEM_LIMIT),
        name="rwkv",
    )(proj, proj, shift4, shiftw, wkv_prev, prm["mu4"], prm["muw"], prm["w0"], prm["dup"],
      prm["a0"], prm["aup"], prm["kk"], prm["ka"], prm["rk"], prm["gnw"], prm["gnb"], bd, tri)


def _dsa_kernel(q_ref, qi_ref, kiwi_ref, k_ref, v_ref, kirep_ref, posf_ref, o_ref, keys_ref, s_ref,
                macc_ref, lacc_ref, oacc_ref, *, pos0, kb_w, topk, tq):
    qt = pl.program_id(1)
    row_chunk = jnp.right_shift(lax.broadcasted_iota(I32, (tq, 1), 0), 6)
    n_adm = pos0 + qt * tq + (row_chunk + 1) * CHUNK
    nkb = (pos0 + (qt + 1) * tq + kb_w - 1) // kb_w
    idx_scale = (IDX_HEADS ** -0.5) * (IDX_DIM ** -0.5)
    att_scale = B_HEAD_DIM ** -0.5
    topk = float(topk)

    qi = qi_ref[...]
    wi = kiwi_ref[:, IDX_DIM:IDX_DIM + IDX_HEADS]
    lane_q = jnp.right_shift(lax.broadcasted_iota(I32, (1, IDX_HEADS * IDX_DIM), 1), 5)
    qis = jnp.concatenate([jnp.where(lane_q == h, qi, 0.0) for h in range(IDX_HEADS)],
                          axis=0).astype(BF16)
    wis = jnp.concatenate([wi[:, h:h + 1] for h in range(IDX_HEADS)], axis=0)
    lane_k = lax.broadcasted_iota(I32, (1, kb_w), 1)

    def score_block(kb, carry):
        off = pl.multiple_of(kb * kb_w, kb_w)
        kir = kirep_ref[0, pl.ds(off, kb_w), :].astype(BF16)
        s = _dot_nt(qis, kir, preferred_element_type=F32)
        s = jnp.maximum(s, 0.0) * wis
        isc = s[0:tq]
        for h in range(1, IDX_HEADS):
            isc = isc + s[h * tq:(h + 1) * tq]
        isc = isc * idx_scale
        bits = pltpu.bitcast(isc, I32)
        key = jnp.where(bits < 0, bits ^ 0x7FFFFFFF, bits)
        key = jnp.where(key == -1, 0, key)
        key = jnp.where(off + lane_k < n_adm, key, INT_MIN)
        keys_ref[:, pl.ds(off, kb_w)] = key
        return carry

    lax.fori_loop(0, nkb, score_block, 0)

    def count_ge(cand):
        def body(kb, acc):
            off = pl.multiple_of(kb * kb_w, kb_w)
            hit = jnp.where(keys_ref[:, pl.ds(off, kb_w)] >= cand, 1.0, 0.0)
            part = hit[:, 0:128]
            for j in range(1, kb_w // 128):
                part = part + hit[:, j * 128:(j + 1) * 128]
            return acc + part
        acc = lax.fori_loop(0, nkb, body, jnp.zeros((tq, 128), F32))
        return jnp.sum(acc, axis=1, keepdims=True)

    c0 = count_ge(jnp.zeros((tq, 1), I32))
    t0 = jnp.where(c0 >= topk, 0, INT_MIN).astype(I32)
    n0 = jnp.where(c0 >= topk, c0, n_adm.astype(F32))

    def bit_step(i, carry):
        t, n_t = carry
        cand = t | jnp.left_shift(jnp.int32(1), 30 - i)
        n_cand = count_ge(cand)
        take = n_cand >= topk
        return jnp.where(take, cand, t), jnp.where(take, n_cand, n_t)

    thr, n_ge = lax.fori_loop(0, 31, bit_step, (t0, n0))
    thr = jnp.maximum(thr, INT_MIN + 1)

    @pl.when(jnp.max(n_ge) > topk)
    def _():
        n_tie_take = topk - count_ge(thr + 1)
        ri = lax.broadcasted_iota(I32, (kb_w, kb_w), 0)
        ci = lax.broadcasted_iota(I32, (kb_w, kb_w), 1)
        upper = (ri <= ci).astype(BF16)

        def body(kb, seen):
            off = pl.multiple_of(kb * kb_w, kb_w)
            kblk = keys_ref[:, pl.ds(off, kb_w)]
            tie = kblk == thr
            rank = seen + jnp.dot(jnp.where(tie, 1.0, 0.0).astype(BF16), upper,
                                  preferred_element_type=F32)
            keys_ref[:, pl.ds(off, kb_w)] = jnp.where(tie & (rank > n_tie_take), INT_MIN, kblk)
            return rank[:, kb_w - 1:kb_w]

        lax.fori_loop(0, nkb, body, jnp.zeros((tq, 1), F32))

    q = q_ref[...]
    rows = B_GROUP * tq
    lane = lax.broadcasted_iota(I32, (tq, 128), 1)
    q_chunk = ((pos0 + qt * tq) // CHUNK + row_chunk).astype(F32)
    q_row = (lax.broadcasted_iota(I32, (tq, 128), 0) & (CHUNK - 1)).astype(F32)
    qpos = pos0 + qt * tq + lax.broadcasted_iota(I32, (tq, 1), 0)
    qaug, slope2 = [], []
    for n in range(B_KV_HEADS):
        q_parts, pos_parts, slope_parts = [], [], []
        keep = (lane >= n * B_HEAD_DIM) & (lane < (n + 1) * B_HEAD_DIM)
        for g in range(B_GROUP):
            h = n * B_GROUP + g
            slope = 2.0 ** (-(8.0 / B_HEADS) * (h + 1))
            slab = q[:, 128 * (h // 2):128 * (h // 2) + 128]
            if h % 2 != n:
                slab = pltpu.roll(slab, B_HEAD_DIM, 1)
            q_parts.append(jnp.where(keep, slab * att_scale, 0.0))
            pos_parts.append(jnp.where(lane == 0, CHUNK * slope,
                             jnp.where(lane == 1, slope,
                             jnp.where(lane == 2, -CHUNK * slope * q_chunk,
                             jnp.where(lane == 3, -slope * q_row, 0.0)))))
            slope_parts.append(jnp.full((tq, 1), 2.0 * slope, F32))
        qaug.append(jnp.concatenate([jnp.concatenate(q_parts, axis=0),
                                     jnp.concatenate(pos_parts, axis=0)], axis=1).astype(BF16))
        slope2.append(jnp.concatenate(slope_parts, axis=0))

    macc_ref[...] = jnp.full(macc_ref.shape, NEG_BIG, F32)
    lacc_ref[...] = jnp.zeros(lacc_ref.shape, F32)
    oacc_ref[...] = jnp.zeros(oacc_ref.shape, F32)

    def fold_lanes(x, op):
        part = x[:, 0:128]
        for j in range(1, kb_w // 128):
            part = op(part, x[:, j * 128:(j + 1) * 128])
        return part

    def score_pass(kb, own_chunk):
        off = pl.multiple_of(kb * kb_w, kb_w)
        sel = keys_ref[:, pl.ds(off, kb_w)] >= thr
        sel4 = jnp.concatenate([sel] * B_GROUP, axis=0)
        kaug = jnp.concatenate([k_ref[0, pl.ds(off, kb_w), :].astype(BF16),
                                posf_ref[pl.ds(off, kb_w), :]], axis=1)
        if own_chunk:
            ahead = jnp.maximum((off + lane_k) - qpos, 0).astype(F32)
            ahead4 = jnp.concatenate([ahead] * B_GROUP, axis=0)
        for n in range(B_KV_HEADS):
            s = _dot_nt(qaug[n], kaug, preferred_element_type=F32)
            if own_chunk:
                s = s - slope2[n] * ahead4
            s = jnp.where(sel4, s, NEG_BIG)
            s_ref[n, :, pl.ds(off, kb_w)] = s
            macc_ref[n] = jnp.maximum(macc_ref[n], fold_lanes(s, jnp.maximum))

    def score_body(kb, carry):
        score_pass(kb, False)
        return carry

    lax.fori_loop(0, nkb - 1, score_body, 0)
    score_pass(nkb - 1, True)

    for n in range(B_KV_HEADS):
        m = jnp.max(macc_ref[n], axis=1, keepdims=True)
        macc_ref[n] = jnp.broadcast_to(m, (rows, 128))

    def value_body(kb, carry):
        off = pl.multiple_of(kb * kb_w, kb_w)
        vblk = v_ref[0, pl.ds(off, kb_w), :].astype(BF16)
        for n in range(B_KV_HEADS):
            m_b = macc_ref[n]
            p = jnp.exp(s_ref[n, :, pl.ds(off, kb_w)] - jnp.concatenate([m_b] * (kb_w // 128), axis=1))
            lacc_ref[n] = lacc_ref[n] + fold_lanes(p, jnp.add)
            oacc_ref[n] = oacc_ref[n] + jnp.dot(p.astype(BF16), vblk, preferred_element_type=F32)
        return carry

    lax.fori_loop(0, nkb, value_body, 0)
    pieces = []
    for n in range(B_KV_HEADS):
        l_fin = jnp.sum(lacc_ref[n], axis=1, keepdims=True)
        o_n = oacc_ref[n][:, n * B_HEAD_DIM:(n + 1) * B_HEAD_DIM] / l_fin
        pieces += [o_n[g * tq:(g + 1) * tq] for g in range(B_GROUP)]
    o_ref[...] = jnp.concatenate(pieces, axis=1)


def _dsa(proj, bsz, t_len, k_all, v_all, kirep_all, kv_col, kirep_col, pos0):
    kb_w = 256
    tq = 2 * CHUNK if (t_len % (2 * CHUNK) == 0 and pos0 % (2 * CHUNK) == 0) else CHUNK
    nq = t_len // tq
    s_pad = k_all.shape[1]
    assert s_pad % kb_w == 0 and pos0 + t_len <= s_pad and pos0 % tq == 0 and kb_w % tq == 0
    rows = B_GROUP * tq
    kpos = lax.broadcasted_iota(I32, (s_pad, 128), 0)
    feat = lax.broadcasted_iota(I32, (s_pad, 128), 1)
    posf = jnp.where(feat == 0, kpos // CHUNK, jnp.where(feat == 1, kpos % CHUNK,
                     jnp.where(feat < 4, 1, 0))).astype(BF16)
    return pl.pallas_call(
        functools.partial(_dsa_kernel, pos0=pos0, kb_w=kb_w, tq=tq,
                          topk=min(MAX_TOPK, (pos0 + t_len) // 4)),
        grid=(bsz, nq),
        in_specs=[
            pl.BlockSpec((tq, 512), lambda b, t: (b * nq + t, C_Q // 512)),
            pl.BlockSpec((tq, 256), lambda b, t: (b * nq + t, C_QI // 256)),
            pl.BlockSpec((tq, 128), lambda b, t: (b * nq + t, C_KIWI // 128)),
            pl.BlockSpec((1, s_pad, 128), lambda b, t: (b, 0, kv_col[0])),
            pl.BlockSpec((1, s_pad, 128), lambda b, t: (b, 0, kv_col[1])),
            pl.BlockSpec((1, s_pad, 256), lambda b, t: (b, 0, kirep_col)),
            pl.BlockSpec((s_pad, 128), lambda b, t: (0, 0)),
        ],
        out_specs=pl.BlockSpec((tq, B_WIDTH), lambda b, t: (b * nq + t, 0)),
        out_shape=jax.ShapeDtypeStruct((bsz * t_len, B_WIDTH), F32),
        scratch_shapes=[
            pltpu.VMEM((tq, s_pad), I32),
            pltpu.VMEM((B_KV_HEADS, rows, s_pad), F32),
            pltpu.VMEM((B_KV_HEADS, rows, 128), F32),
            pltpu.VMEM((B_KV_HEADS, rows, 128), F32),
            pltpu.VMEM((B_KV_HEADS, rows, 128), F32),
        ],
        compiler_params=pltpu.CompilerParams(
            dimension_semantics=("arbitrary", "arbitrary"), vmem_limit_bytes=VMEM_LIMIT),
        name="dsa",
    )(proj, proj, proj, k_all, v_all, kirep_all, posf)


def _merge_kernel(x_ref, ya_ref, yb_ref, gd_ref, gab_ref, wpa_ref, wpb_ref, wo_ref, fnw_ref, o_ref):
    gd = gd_ref[...]
    yb = yb_ref[...] * (gd * _sigmoid(gd))
    pa = jnp.dot(ya_ref[...].astype(BF16), wpa_ref[...], preferred_element_type=F32)
    pb = jnp.dot(yb.astype(BF16), wpb_ref[...], preferred_element_type=F32)
    merged = _sigmoid(gab_ref[:, 0:D_MODEL]) * pa + _sigmoid(gab_ref[:, D_MODEL:2 * D_MODEL]) * pb
    out = x_ref[...] + jnp.dot(merged.astype(BF16), wo_ref[...], preferred_element_type=F32)
    ms = jnp.mean(out * out, axis=-1, keepdims=True)
    o_ref[...] = (out * lax.rsqrt(ms + NORM_EPS)) * fnw_ref[...]


def _merge(x2d, ya, yb, proj, w_pa, w_pb, w_o, final_w):
    m = x2d.shape[0]
    tm = min(512, m)
    full = lambda shape: pl.BlockSpec(shape, lambda i: (0, 0))
    return pl.pallas_call(
        _merge_kernel,
        grid=(m // tm,),
        in_specs=[
            pl.BlockSpec((tm, D_MODEL), lambda i: (i, 0)),
            pl.BlockSpec((tm, A_WIDTH), lambda i: (i, 0)),
            pl.BlockSpec((tm, B_WIDTH), lambda i: (i, 0)),
            pl.BlockSpec((tm, 512), lambda i: (i, C_GD // 512)),
            pl.BlockSpec((tm, 2048), lambda i: (i, C_GATES // 2048)),
            full((A_WIDTH, D_MODEL)), full((B_WIDTH, D_MODEL)), full((D_MODEL, D_MODEL)),
            full((1, D_MODEL)),
        ],
        out_specs=pl.BlockSpec((tm, D_MODEL), lambda i: (i, 0)),
        out_shape=jax.ShapeDtypeStruct((m, D_MODEL), F32),
        compiler_params=pltpu.CompilerParams(
            dimension_semantics=("arbitrary",), vmem_limit_bytes=VMEM_LIMIT),
        name="merge",
    )(x2d, ya, yb, proj, proj, w_pa.astype(BF16), w_pb.astype(BF16), w_o.astype(BF16),
      final_w.reshape(1, D_MODEL))


def _rwkv_order(row):
    return row[..., 0:2048], row[..., 2048:2176]


def _mixer(x, shift_prev, wkv_prev, past_k, past_v, past_ki, w_perm, norm_w, prm, w_pa, w_pb, w_o,
           final_w):
    bsz, t_len, _ = x.shape
    x2d = x.reshape(bsz * t_len, D_MODEL)
    proj = _proj(x2d, norm_w, w_perm)
    proj3 = proj.reshape(bsz, t_len, N_PAD)

    shift4, shiftw = _rwkv_order(shift_prev)
    ya, wkv_new = _rwkv(proj, bsz, t_len, shift4, shiftw, wkv_prev, prm)

    k_new = proj3[:, :, C_KD:C_KD + 128]
    v_new = proj3[:, :, C_VD:C_VD + 128]
    ki_new = proj3[:, :, C_KIWI:C_KIWI + IDX_DIM]
    past_len = 0 if past_k is None else past_k.shape[1]
    if past_len == 0:
        yb = _dsa(proj, bsz, t_len, proj3, proj3, proj3, (C_KD // 128, C_VD // 128), C_KIREP // 256, 0)
    else:
        s_tot = past_len + t_len
        s_pad = -(-s_tot // 256) * 256
        pad = lambda a: jnp.pad(a, ((0, 0), (0, s_pad - s_tot), (0, 0)))
        k_all = pad(jnp.concatenate([past_k.reshape(bsz, past_len, 128), k_new], axis=1))
        v_all = pad(jnp.concatenate([past_v.reshape(bsz, past_len, 128), v_new], axis=1))
        kirep_all = pad(jnp.concatenate(
            [jnp.tile(past_ki, (1, 1, IDX_HEADS)), proj3[:, :, C_KIREP:C_KIREP + 256]], axis=1))
        yb = _dsa(proj, bsz, t_len, k_all, v_all, kirep_all, (0, 0), 0, past_len)

    y = _merge(x2d, ya, yb, proj, w_pa, w_pb, w_o, final_w).reshape(bsz, t_len, D_MODEL)
    last = proj3[:, t_len - 1:t_len, :]
    shift_new = jnp.concatenate([last[..., C_RKVG:C_RKVG + 2048], last[..., C_WDAD:C_WDAD + 128]], axis=-1)
    kv_shape = (bsz, t_len, B_KV_HEADS, B_HEAD_DIM)
    return y, k_new.reshape(kv_shape), v_new.reshape(kv_shape), ki_new, wkv_new, shift_new


def kernel(x_prompt, x_sample, cache_k, cache_v, cache_kidx, state_wkv, state_shift, norm_w, w_in,
           shift_mu, decay_w0, decay_up, iclr_a0, iclr_up, k_k, k_a, r_k, gn_w, gn_b, w_pa, w_pb,
           w_o, final_norm_w):
    assert w_in.shape[0] == 1, "the final norm is fused into the (single) layer's merge kernel"
    bp = x_prompt.shape[0]
    w_perm = _permute_w_in(w_in[0])
    mu4, muw = _rwkv_order(shift_mu[0].reshape(1, RWKV_COLS))
    row = lambda a: a.reshape(1, A_WIDTH)
    prm = dict(mu4=mu4, muw=muw, w0=row(decay_w0[0]), dup=decay_up[0], a0=row(iclr_a0[0]),
               aup=iclr_up[0], kk=row(k_k[0]), ka=row(k_a[0]), rk=row(r_k[0]), gnw=row(gn_w[0]),
               gnb=row(gn_b[0]))
    common = (w_perm, norm_w[0], prm, w_pa[0], w_pb[0], w_o[0], final_norm_w)
    yp, kp, vp, kip, wkvp, shp = _mixer(
        x_prompt, jnp.zeros((bp, 1, RWKV_COLS), F32),
        jnp.zeros((bp, A_HEADS, A_HEAD_DIM, A_HEAD_DIM), F32), None, None, None, *common)
    ys, ks, vs, kis, wkvs, shs = _mixer(
        x_sample, state_shift[0], state_wkv[0], cache_k[0], cache_v[0], cache_kidx[0], *common)
    st = lambda a: a[None]
    return (yp, ys, st(kp), st(vp), st(kip), st(wkvp), st(shp),
            st(ks), st(vs), st(kis), st(wkvs), st(shs))
```

```python
import functools

import jax
import jax.numpy as jnp
from jax import lax
from jax.experimental import pallas as pl
from jax.experimental.pallas import tpu as pltpu

F32 = jnp.float32
BF16 = jnp.bfloat16
I32 = jnp.int32
HIGHEST = lax.Precision.HIGHEST

D_MODEL = 1024
CHUNK = 64
A_HEADS = 8
A_HEAD_DIM = 64
A_WIDTH = 512
LORA = 64
RWKV_COLS = 4 * A_WIDTH + 2 * LORA
B_HEADS = 8
B_KV_HEADS = 2
B_GROUP = 4
B_HEAD_DIM = 64
B_WIDTH = 512
B_KV_WIDTH = 128
IDX_HEADS = 8
IDX_DIM = 32
MAX_TOPK = 256
DSA_COLS = 1576
N_IN = 5800
NORM_EPS = 1e-6
GN_EPS = 64e-5

C_GATES = 0
C_RKVG = 2048
C_Q = 4096
C_GD = 4608
C_QI = 5120
C_KIREP = 5376
C_KD = 5632
C_VD = 5760
C_KIWI = 5888
C_WDAD = 6016
N_PAD = 6144

KEY_BLOCK = 512
INT_MIN = -(2 ** 31)
LOWEST_FINITE_KEY = INT_MIN + 0x00800000
NEG_BIG = -1e30
VMEM_LIMIT = 56 * 1024 * 1024


def _sigmoid(x):
    return 1.0 / (1.0 + jnp.exp(-x))


def _dot_nt(a, b, **kw):
    return lax.dot_general(a, b, (((1,), (1,)), ((), ())), **kw)


def _dot_tn(a, b, **kw):
    return lax.dot_general(a, b, (((0,), (0,)), ((), ())), **kw)


def _permute_w_in(w):
    d0 = RWKV_COLS
    g0 = RWKV_COLS + DSA_COLS
    ki = w[:, d0 + 1024:d0 + 1056]
    cols = [
        w[:, g0:g0 + 2048],
        w[:, 0:2048],
        w[:, d0:d0 + 512],
        w[:, d0 + 1064:d0 + 1576],
        w[:, d0 + 768:d0 + 1024],
        jnp.tile(ki, (1, IDX_HEADS)),
        w[:, d0 + 512:d0 + 640],
        w[:, d0 + 640:d0 + 768],
        w[:, d0 + 1024:d0 + 1064],
        jnp.zeros((w.shape[0], 128 - IDX_DIM - IDX_HEADS), w.dtype),
        w[:, 2048:2176],
    ]
    return jnp.concatenate(cols, axis=1).astype(BF16)


def _proj_kernel(x_ref, nw_ref, w_ref, o_ref):
    x = x_ref[...]
    ms = jnp.mean(x * x, axis=-1, keepdims=True)
    h = (x * lax.rsqrt(ms + NORM_EPS)) * nw_ref[...]
    o_ref[...] = jnp.dot(h.astype(BF16), w_ref[...], preferred_element_type=F32)


def _proj(x2d, norm_w, w_perm):
    m = x2d.shape[0]
    tm = min(512, m)
    tn = N_PAD // 2
    return pl.pallas_call(
        _proj_kernel,
        grid=(N_PAD // tn, m // tm),
        in_specs=[
            pl.BlockSpec((tm, D_MODEL), lambda j, i: (i, 0)),
            pl.BlockSpec((1, D_MODEL), lambda j, i: (0, 0)),
            pl.BlockSpec((D_MODEL, tn), lambda j, i: (0, j)),
        ],
        out_specs=pl.BlockSpec((tm, tn), lambda j, i: (i, j)),
        out_shape=jax.ShapeDtypeStruct((m, N_PAD), F32),
        compiler_params=pltpu.CompilerParams(
            dimension_semantics=("arbitrary", "arbitrary"), vmem_limit_bytes=VMEM_LIMIT),
        name="proj",
    )(x2d, norm_w.reshape(1, D_MODEL), w_perm)


def _split_bf16(x, terms):
    pieces = []
    for _ in range(terms):
        piece = x.astype(BF16)
        pieces.append(piece)
        x = x - piece.astype(F32)
    return pieces


def _dot_exact_rhs(a, b_bf16, terms):
    acc = None
    for piece in _split_bf16(a, terms):
        d = jnp.dot(piece, b_bf16, preferred_element_type=F32)
        acc = d if acc is None else acc + d
    return acc


def _dot_3pass(a, b):
    ah, al = _split_bf16(a, 2)
    bh, bl = _split_bf16(b, 2)
    dot = functools.partial(jnp.dot, preferred_element_type=F32)
    return dot(ah, bh) + dot(ah, bl) + dot(al, bh)


def _rwkv_kernel(p4_ref, pw_ref, sp4_ref, spw_ref, s0_ref, mu4_ref, muw_ref, w0_ref, dup_ref,
                 a0_ref, aup_ref, kk_ref, ka_ref, rk_ref, gnw_ref, gnb_ref, bd_ref, tri_ref,
                 ya_ref, sout_ref, c4_ref, cw_ref, sbd_ref, *, nch):
    c = CHUNK
    n = A_HEAD_DIM
    rows = nch * c
    n_pairs = A_HEADS // 2
    t_idx = pl.program_id(1)
    lane = lax.broadcasted_iota(I32, (c, 2 * n), 1)
    lo_half = lane < n
    row_c = lax.broadcasted_iota(I32, (c, 2 * n), 0)
    pos_in_head = jnp.where(lo_half, lane, lane - n)
    tri_strict = (pos_in_head < row_c).astype(F32)
    tri_incl = (pos_in_head <= row_c).astype(F32)
    eye2 = (pos_in_head == row_c).astype(F32)
    lane_sq = lax.broadcasted_iota(I32, (2 * n, 2 * n), 1)
    row_sq = lax.broadcasted_iota(I32, (2 * n, 2 * n), 0)
    same_head = (lane_sq < n) == (row_sq < n)
    dot = functools.partial(jnp.dot, preferred_element_type=F32)

    def bdiag(x):
        zero = jnp.zeros_like(x)
        return jnp.concatenate([jnp.where(lo_half, x, zero), jnp.where(lo_half, zero, x)], axis=0)

    @pl.when(t_idx == 0)
    def _():
        c4_ref[...] = sp4_ref[0]
        cw_ref[...] = spw_ref[0]
        zeros = jnp.zeros((n, n), F32)
        for j in range(n_pairs):
            sbd_ref[j] = jnp.concatenate(
                [jnp.concatenate([s0_ref[0, 2 * j], zeros], axis=1),
                 jnp.concatenate([zeros, s0_ref[0, 2 * j + 1]], axis=1)], axis=0)

    p4 = p4_ref[...]
    pw = pw_ref[...]
    row = lax.broadcasted_iota(I32, (rows, 1), 0)
    prev4 = jnp.where(row == 0, c4_ref[...], pltpu.roll(p4, 1, 0))
    prevw = jnp.where(row == 0, cw_ref[...], pltpu.roll(pw, 1, 0))
    c4_ref[...] = p4[rows - 1:rows, :]
    cw_ref[...] = pw[rows - 1:rows, :]
    ps4 = p4 + (prev4 - p4) * mu4_ref[...]
    psw = pw + (prevw - pw) * muw_ref[...]
    r = ps4[:, 0:512]
    k = ps4[:, 512:1024]
    v = ps4[:, 1024:1536]
    g = ps4[:, 1536:2048]
    wd = psw[:, 0:LORA]
    ad = psw[:, LORA:2 * LORA]
    bd = bd_ref[...]

    xw = w0_ref[...] + _dot_3pass(jnp.tanh(wd), dup_ref[...])
    z = -xw
    softplus = jnp.maximum(z, 0.0) + jnp.log(1.0 + jnp.exp(-jnp.abs(z)))
    lw = -jnp.exp(-softplus - 0.5)
    a = _sigmoid(a0_ref[...] + _dot_3pass(ad, aup_ref[...]))
    kkr = k * kk_ref[...]
    kkn = kkr / jnp.maximum(jnp.sqrt(_dot_exact_rhs(kkr * kkr, bd, 2)), 1e-12)
    kmod = k * (1.0 + (a - 1.0) * ka_ref[...])

    cum = None
    for piece in _split_bf16(lw, 3):
        d = dot(tri_ref[...], piece)
        cum = d if cum is None else cum + d
    pdec = jnp.exp(cum)
    pinv = jnp.exp(-cum)
    rt = (r * pdec).astype(BF16)
    at = (-kkn * jnp.exp(cum - lw)).astype(BF16)
    bt = (kkn * a * pinv).astype(BF16)
    kt = (kmod * pinv).astype(BF16)
    vb = v.astype(BF16)

    tiles = [(ci, j) for ci in range(nch) for j in range(n_pairs)]
    rsl = lambda ci: slice(ci * c, (ci + 1) * c)
    lsl = lambda j: slice(j * 2 * n, (j + 1) * 2 * n)
    lhs, bk, a_ak_rk, a_rb, xs, tinv = {}, {}, {}, {}, {}, {}
    for ci, j in tiles:
        rs, ls = rsl(ci), lsl(j)
        lhs[ci, j] = jnp.concatenate([at[rs, ls], rt[rs, ls]], axis=0)
        bk[ci, j] = jnp.concatenate([bt[rs, ls], kt[rs, ls]], axis=0)
        amat = _dot_nt(lhs[ci, j], jnp.concatenate([bdiag(bt[rs, ls]), bdiag(kt[rs, ls])], axis=0),
                       preferred_element_type=F32)
        xs[ci, j] = amat[0:c, 0:2 * n] * tri_strict
        a_ak_rk[ci, j] = jnp.concatenate([amat[0:c, 2 * n:4 * n] * tri_strict,
                                          amat[c:2 * c, 2 * n:4 * n] * tri_incl], axis=0).astype(BF16)
        a_rb[ci, j] = (amat[c:2 * c, 0:2 * n] * tri_incl).astype(BF16)
        tinv[ci, j] = eye2 + xs[ci, j]
    for _ in range(5):
        for t in tiles:
            xb = xs[t].astype(BF16)
            xs[t] = dot(xb, bdiag(xb))
        for t in tiles:
            tinv[t] = tinv[t] + dot(tinv[t].astype(BF16), bdiag(xs[t].astype(BF16)))
    akv = {}
    for ci, j in tiles:
        tinv[ci, j] = tinv[ci, j].astype(BF16)
        akv[ci, j] = dot(a_ak_rk[ci, j], bdiag(vb[rsl(ci), lsl(j)]))

    pairs = range(n_pairs)
    s_pair = [sbd_ref[j] for j in pairs]
    y_chunks = []
    for ci in range(nch):
        from_state = [_dot_nt(lhs[ci, j], s_pair[j].astype(BF16), preferred_element_type=F32)
                      for j in pairs]
        u = [dot(tinv[ci, j], bdiag((from_state[j][0:c] + akv[ci, j][0:c]).astype(BF16)))
             for j in pairs]
        uv_t = [jnp.transpose(jnp.concatenate([u[j], v[rsl(ci), lsl(j)]], axis=0)).astype(BF16)
                for j in pairs]
        upd = [dot(uv_t[j], bk[ci, j]) for j in pairs]
        s_pair = [(s_pair[j] + jnp.where(same_head, upd[j], 0.0))
                  * pdec[(ci + 1) * c - 1:(ci + 1) * c, lsl(j)] for j in pairs]
        y_chunks.append(jnp.concatenate(
            [from_state[j][c:2 * c] + akv[ci, j][c:2 * c] + dot(a_rb[ci, j], bdiag(u[j].astype(BF16)))
             for j in pairs], axis=1))
    for j in pairs:
        sbd_ref[j] = s_pair[j]
    y = y_chunks[0] if nch == 1 else jnp.concatenate(y_chunks, axis=0)

    inv_n = 1.0 / n
    mean = _dot_exact_rhs(y, bd, 2) * inv_n
    dlt = y - mean
    var = _dot_exact_rhs(dlt * dlt, bd, 2) * inv_n
    yn = dlt * lax.rsqrt(var + GN_EPS) * gnw_ref[...] + gnb_ref[...]
    yn = yn + _dot_exact_rhs(r * kmod * rk_ref[...], bd, 2) * v
    ya_ref[...] = yn * (g * _sigmoid(g))

    @pl.when(t_idx == pl.num_programs(1) - 1)
    def _():
        for j in range(n_pairs):
            s_pair = sbd_ref[j]
            sout_ref[0, 2 * j] = s_pair[0:n, 0:n]
            sout_ref[0, 2 * j + 1] = s_pair[n:2 * n, n:2 * n]


def _rwkv(proj, bsz, t_len, shift4, shiftw, wkv_prev, prm):
    nch = min(4, t_len // CHUNK)
    rows = nch * CHUNK
    nt = t_len // rows
    row1 = lambda width: pl.BlockSpec((1, width), lambda b, t: (0, 0))
    head = lax.broadcasted_iota(I32, (A_WIDTH, A_WIDTH), 0) // A_HEAD_DIM
    bd = (head == head.T).astype(BF16)
    ti = lax.broadcasted_iota(I32, (rows, rows), 0)
    si = lax.broadcasted_iota(I32, (rows, rows), 1)
    tri = ((ti // CHUNK == si // CHUNK) & (si <= ti)).astype(BF16)
    return pl.pallas_call(
        functools.partial(_rwkv_kernel, nch=nch),
        grid=(bsz, nt),
        in_specs=[
            pl.BlockSpec((rows, 2048), lambda b, t: (b * nt + t, C_RKVG // 2048)),
            pl.BlockSpec((rows, 128), lambda b, t: (b * nt + t, C_WDAD // 128)),
            pl.BlockSpec((1, 1, 2048), lambda b, t: (b, 0, 0)),
            pl.BlockSpec((1, 1, 128), lambda b, t: (b, 0, 0)),
            pl.BlockSpec((1, A_HEADS, A_HEAD_DIM, A_HEAD_DIM), lambda b, t: (b, 0, 0, 0)),
            row1(2048), row1(128), row1(A_WIDTH),
            pl.BlockSpec((LORA, A_WIDTH), lambda b, t: (0, 0)),
            row1(A_WIDTH),
            pl.BlockSpec((LORA, A_WIDTH), lambda b, t: (0, 0)),
            row1(A_WIDTH), row1(A_WIDTH), row1(A_WIDTH), row1(A_WIDTH), row1(A_WIDTH),
            pl.BlockSpec((A_WIDTH, A_WIDTH), lambda b, t: (0, 0)),
            pl.BlockSpec((rows, rows), lambda b, t: (0, 0)),
        ],
        out_specs=[
            pl.BlockSpec((rows, A_WIDTH), lambda b, t: (b * nt + t, 0)),
            pl.BlockSpec((1, A_HEADS, A_HEAD_DIM, A_HEAD_DIM), lambda b, t: (b, 0, 0, 0)),
        ],
        out_shape=[
            jax.ShapeDtypeStruct((bsz * t_len, A_WIDTH), F32),
            jax.ShapeDtypeStruct((bsz, A_HEADS, A_HEAD_DIM, A_HEAD_DIM), F32),
        ],
        scratch_shapes=[pltpu.VMEM((1, 2048), F32), pltpu.VMEM((1, 128), F32),
                        pltpu.VMEM((A_HEADS // 2, 2 * A_HEAD_DIM, 2 * A_HEAD_DIM), F32)],
        compiler_params=pltpu.CompilerParams(
            dimension_semantics=("arbitrary", "arbitrary"), vmem_limit_bytes=VMEM_LIMIT),
        name="rwkv",
    )(proj, proj, shift4, shiftw, wkv_prev, prm["mu4"], prm["muw"], prm["w0"], prm["dup"],
      prm["a0"], prm["aup"], prm["kk"], prm["ka"], prm["rk"], prm["gnw"], prm["gnb"], bd, tri)


def _dsa_kernel(q_ref, qi_ref, kiwi_ref, k_ref, v_ref, kirep_ref, posf_ref, o_ref, keys_ref, s_ref,
                macc_ref, lacc_ref, oacc_ref, *, pos0, kb_w, topk, tq, th):
    qt = pl.program_id(1)
    tile_pos = pos0 + qt * tq
    row_chunk = jnp.right_shift(lax.broadcasted_iota(I32, (tq, 1), 0), 6)
    n_adm = tile_pos + (row_chunk + 1) * CHUNK
    nkb = (tile_pos + tq + kb_w - 1) // kb_w
    idx_scale = (IDX_HEADS ** -0.5) * (IDX_DIM ** -0.5)
    att_scale = B_HEAD_DIM ** -0.5
    topk = float(topk)
    lane_q = jnp.right_shift(lax.broadcasted_iota(I32, (1, IDX_HEADS * IDX_DIM), 1), 5)
    lane_k = lax.broadcasted_iota(I32, (1, kb_w), 1)

    def index_part(r0):
        qi = qi_ref[r0:r0 + th, :]
        wi = kiwi_ref[r0:r0 + th, IDX_DIM:IDX_DIM + IDX_HEADS]
        qis = jnp.concatenate([jnp.where(lane_q == h, qi, 0.0) for h in range(IDX_HEADS)],
                              axis=0).astype(BF16)
        wis = jnp.concatenate([wi[:, h:h + 1] for h in range(IDX_HEADS)], axis=0)
        n_adm_part = n_adm[r0:r0 + th]

        def score_block(kb, carry):
            off = pl.multiple_of(kb * kb_w, kb_w)
            kir = kirep_ref[0, pl.ds(off, kb_w), :].astype(BF16)
            s = _dot_nt(qis, kir, preferred_element_type=F32)
            s = jnp.maximum(s, 0.0) * wis
            isc = s[0:th]
            for h in range(1, IDX_HEADS):
                isc = isc + s[h * th:(h + 1) * th]
            isc = isc * idx_scale
            keys_ref[r0:r0 + th, pl.ds(off, kb_w)] = jnp.where(off + lane_k < n_adm_part, isc, -jnp.inf)
            return carry

        lax.fori_loop(0, nkb, score_block, 0)

    for r0 in range(0, tq, th):
        index_part(r0)

    def key_to_score(key):
        return pltpu.bitcast(jnp.where(key < 0, key ^ 0x7FFFFFFF, key), F32)

    def count_ge(cand_key):
        accs = []
        for r0 in range(0, tq, th):
            cand_part = key_to_score(cand_key[r0:r0 + th])

            def body(kb, acc, r0=r0, cand_part=cand_part):
                off = pl.multiple_of(kb * kb_w, kb_w)
                kblk = keys_ref[r0:r0 + th, pl.ds(off, kb_w)]
                for j in range(kb_w // 128):
                    acc = acc + jnp.where(kblk[:, j * 128:(j + 1) * 128] >= cand_part, 1.0, 0.0)
                return acc

            accs.append(lax.fori_loop(0, nkb, body, jnp.zeros((th, 128), F32)))
        acc = accs[0] if len(accs) == 1 else jnp.concatenate(accs, axis=0)
        return jnp.sum(acc, axis=1, keepdims=True)

    c0 = count_ge(jnp.zeros((tq, 128), I32))
    t0 = jnp.where(c0 >= topk, jnp.zeros((tq, 128), I32), jnp.full((tq, 128), INT_MIN, I32))
    n0 = jnp.where(c0 >= topk, c0, n_adm.astype(F32))

    def bit_step(i, carry):
        t, n_t = carry
        cand = t | jnp.left_shift(jnp.int32(1), 30 - i)
        n_cand = count_ge(cand)
        take = n_cand >= topk
        return jnp.where(take, cand, t), jnp.where(take, n_cand, n_t)

    thr_key, n_ge = lax.fori_loop(0, 31, bit_step, (t0, n0))
    thr_key = jnp.maximum(thr_key, LOWEST_FINITE_KEY)
    thr = key_to_score(thr_key)[:, 0:1]

    @pl.when(jnp.max(n_ge) > topk)
    def _():
        n_tie_take = topk - count_ge(thr_key + 1)
        ri = lax.broadcasted_iota(I32, (kb_w, kb_w), 0)
        ci = lax.broadcasted_iota(I32, (kb_w, kb_w), 1)
        upper = (ri <= ci).astype(BF16)

        def body(kb, seen):
            off = pl.multiple_of(kb * kb_w, kb_w)
            kblk = keys_ref[:, pl.ds(off, kb_w)]
            tie = kblk == thr
            rank = seen + jnp.dot(jnp.where(tie, 1.0, 0.0).astype(BF16), upper,
                                  preferred_element_type=F32)
            keys_ref[:, pl.ds(off, kb_w)] = jnp.where(tie & (rank > n_tie_take), -jnp.inf, kblk)
            return rank[:, kb_w - 1:kb_w]

        lax.fori_loop(0, nkb, body, jnp.zeros((tq, 1), F32))

    rows = B_GROUP * th
    lane = lax.broadcasted_iota(I32, (th, 128), 1)

    def fold_lanes(x, op):
        part = x[:, 0:128]
        for j in range(1, kb_w // 128):
            part = op(part, x[:, j * 128:(j + 1) * 128])
        return part

    def attend_part(r0):
        q = q_ref[r0:r0 + th, :]
        thr_part = thr[r0:r0 + th]
        q_chunk = (tile_pos // CHUNK + row_chunk[r0:r0 + th]).astype(F32)
        q_row = (lax.broadcasted_iota(I32, (th, 128), 0) & (CHUNK - 1)).astype(F32)
        qpos = tile_pos + r0 + lax.broadcasted_iota(I32, (th, 1), 0)
        qaug, slope2 = [], []
        for n in range(B_KV_HEADS):
            q_parts, pos_parts, slope_parts = [], [], []
            keep = (lane >= n * B_HEAD_DIM) & (lane < (n + 1) * B_HEAD_DIM)
            for g in range(B_GROUP):
                h = n * B_GROUP + g
                slope = 2.0 ** (-(8.0 / B_HEADS) * (h + 1))
                slab = q[:, 128 * (h // 2):128 * (h // 2) + 128]
                if h % 2 != n:
                    slab = pltpu.roll(slab, B_HEAD_DIM, 1)
                q_parts.append(jnp.where(keep, slab * att_scale, 0.0))
                pos_parts.append(jnp.where(lane == 0, CHUNK * slope,
                                 jnp.where(lane == 1, slope,
                                 jnp.where(lane == 2, -CHUNK * slope * q_chunk,
                                 jnp.where(lane == 3, -slope * q_row, 0.0)))))
                slope_parts.append(jnp.full((th, 1), 2.0 * slope, F32))
            qaug.append(jnp.concatenate([jnp.concatenate(q_parts, axis=0),
                                         jnp.concatenate(pos_parts, axis=0)], axis=1).astype(BF16))
            slope2.append(jnp.concatenate(slope_parts, axis=0))

        macc_ref[...] = jnp.full(macc_ref.shape, NEG_BIG, F32)
        lacc_ref[...] = jnp.zeros(lacc_ref.shape, F32)
        oacc_ref[...] = jnp.zeros(oacc_ref.shape, F32)

        def score_pass(kb, own_chunk):
            off = pl.multiple_of(kb * kb_w, kb_w)
            sel = keys_ref[r0:r0 + th, pl.ds(off, kb_w)] >= thr_part
            sel4 = jnp.concatenate([sel] * B_GROUP, axis=0)
            kaug = jnp.concatenate([k_ref[0, pl.ds(off, kb_w), :].astype(BF16),
                                    posf_ref[pl.ds(off, kb_w), :]], axis=1)
            if own_chunk:
                ahead = jnp.maximum((off + lane_k) - qpos, 0).astype(F32)
                ahead4 = jnp.concatenate([ahead] * B_GROUP, axis=0)
            for n in range(B_KV_HEADS):
                s = _dot_nt(qaug[n], kaug, preferred_element_type=F32)
                if own_chunk:
                    s = s - slope2[n] * ahead4
                s = jnp.where(sel4, s, NEG_BIG)
                s_ref[n, :, pl.ds(off, kb_w)] = s
                macc_ref[n] = jnp.maximum(macc_ref[n], fold_lanes(s, jnp.maximum))

        def score_body(kb, carry):
            score_pass(kb, False)
            return carry

        lax.fori_loop(0, nkb - 1, score_body, 0)
        score_pass(nkb - 1, True)

        for n in range(B_KV_HEADS):
            m = jnp.max(macc_ref[n], axis=1, keepdims=True)
            macc_ref[n] = jnp.broadcast_to(m, (rows, 128))

        def value_body(kb, carry):
            off = pl.multiple_of(kb * kb_w, kb_w)
            vblk = v_ref[0, pl.ds(off, kb_w), :].astype(BF16)
            for n in range(B_KV_HEADS):
                m_b = macc_ref[n]
                p = jnp.exp(s_ref[n, :, pl.ds(off, kb_w)]
                            - jnp.concatenate([m_b] * (kb_w // 128), axis=1))
                lacc_ref[n] = lacc_ref[n] + fold_lanes(p, jnp.add)
                oacc_ref[n] = oacc_ref[n] + jnp.dot(p.astype(BF16), vblk, preferred_element_type=F32)
            return carry

        lax.fori_loop(0, nkb, value_body, 0)
        pieces = []
        for n in range(B_KV_HEADS):
            l_fin = jnp.sum(lacc_ref[n], axis=1, keepdims=True)
            o_n = oacc_ref[n][:, n * B_HEAD_DIM:(n + 1) * B_HEAD_DIM] / l_fin
            pieces += [o_n[g * th:(g + 1) * th] for g in range(B_GROUP)]
        o_ref[r0:r0 + th, :] = jnp.concatenate(pieces, axis=1)

    for r0 in range(0, tq, th):
        attend_part(r0)


def _dsa(proj, bsz, t_len, k_all, v_all, kirep_all, kv_col, kirep_col, pos0):
    kb_w = KEY_BLOCK
    tq = next(c for c in (4 * CHUNK, 2 * CHUNK, CHUNK) if t_len % c == 0 and pos0 % c == 0)
    th = min(tq, 2 * CHUNK)
    nq = t_len // tq
    s_pad = k_all.shape[1]
    assert s_pad % kb_w == 0 and pos0 + t_len <= s_pad and pos0 % tq == 0 and kb_w % tq == 0
    rows = B_GROUP * th
    kpos = lax.broadcasted_iota(I32, (s_pad, 128), 0)
    feat = lax.broadcasted_iota(I32, (s_pad, 128), 1)
    posf = jnp.where(feat == 0, kpos // CHUNK, jnp.where(feat == 1, kpos % CHUNK,
                     jnp.where(feat < 4, 1, 0))).astype(BF16)
    return pl.pallas_call(
        functools.partial(_dsa_kernel, pos0=pos0, kb_w=kb_w, tq=tq, th=th,
                          topk=min(MAX_TOPK, (pos0 + t_len) // 4)),
        grid=(bsz, nq),
        in_specs=[
            pl.BlockSpec((tq, 512), lambda b, t: (b * nq + t, C_Q // 512)),
            pl.BlockSpec((tq, 256), lambda b, t: (b * nq + t, C_QI // 256)),
            pl.BlockSpec((tq, 128), lambda b, t: (b * nq + t, C_KIWI // 128)),
            pl.BlockSpec((1, s_pad, 128), lambda b, t: (b, 0, kv_col[0])),
            pl.BlockSpec((1, s_pad, 128), lambda b, t: (b, 0, kv_col[1])),
            pl.BlockSpec((1, s_pad, 256), lambda b, t: (b, 0, kirep_col)),
            pl.BlockSpec((s_pad, 128), lambda b, t: (0, 0)),
        ],
        out_specs=pl.BlockSpec((tq, B_WIDTH), lambda b, t: (b * nq + t, 0)),
        out_shape=jax.ShapeDtypeStruct((bsz * t_len, B_WIDTH), F32),
        scratch_shapes=[
            pltpu.VMEM((tq, s_pad), F32),
            pltpu.VMEM((B_KV_HEADS, rows, s_pad), F32),
            pltpu.VMEM((B_KV_HEADS, rows, 128), F32),
            pltpu.VMEM((B_KV_HEADS, rows, 128), F32),
            pltpu.VMEM((B_KV_HEADS, rows, 128), F32),
        ],
        compiler_params=pltpu.CompilerParams(
            dimension_semantics=("arbitrary", "arbitrary"), vmem_limit_bytes=VMEM_LIMIT),
        name="dsa",
    )(proj, proj, proj, k_all, v_all, kirep_all, posf)


def _merge_kernel(x_ref, ya_ref, yb_ref, gd_ref, gab_ref, wpa_ref, wpb_ref, wo_ref, fnw_ref, o_ref):
    gd = gd_ref[...]
    yb = yb_ref[...] * (gd * _sigmoid(gd))
    pa = jnp.dot(ya_ref[...].astype(BF16), wpa_ref[...], preferred_element_type=F32)
    pb = jnp.dot(yb.astype(BF16), wpb_ref[...], preferred_element_type=F32)
    merged = _sigmoid(gab_ref[:, 0:D_MODEL]) * pa + _sigmoid(gab_ref[:, D_MODEL:2 * D_MODEL]) * pb
    out = x_ref[...] + jnp.dot(merged.astype(BF16), wo_ref[...], preferred_element_type=F32)
    ms = jnp.mean(out * out, axis=-1, keepdims=True)
    o_ref[...] = (out * lax.rsqrt(ms + NORM_EPS)) * fnw_ref[...]


def _merge(x2d, ya, yb, proj, w_pa, w_pb, w_o, final_w):
    m = x2d.shape[0]
    tm = min(512, m)
    full = lambda shape: pl.BlockSpec(shape, lambda i: (0, 0))
    return pl.pallas_call(
        _merge_kernel,
        grid=(m // tm,),
        in_specs=[
            pl.BlockSpec((tm, D_MODEL), lambda i: (i, 0)),
            pl.BlockSpec((tm, A_WIDTH), lambda i: (i, 0)),
            pl.BlockSpec((tm, B_WIDTH), lambda i: (i, 0)),
            pl.BlockSpec((tm, 512), lambda i: (i, C_GD // 512)),
            pl.BlockSpec((tm, 2048), lambda i: (i, C_GATES // 2048)),
            full((A_WIDTH, D_MODEL)), full((B_WIDTH, D_MODEL)), full((D_MODEL, D_MODEL)),
            full((1, D_MODEL)),
        ],
        out_specs=pl.BlockSpec((tm, D_MODEL), lambda i: (i, 0)),
        out_shape=jax.ShapeDtypeStruct((m, D_MODEL), F32),
        compiler_params=pltpu.CompilerParams(
            dimension_semantics=("arbitrary",), vmem_limit_bytes=VMEM_LIMIT),
        name="merge",
    )(x2d, ya, yb, proj, proj, w_pa.astype(BF16), w_pb.astype(BF16), w_o.astype(BF16),
      final_w.reshape(1, D_MODEL))


def _rwkv_order(row):
    return row[..., 0:2048], row[..., 2048:2176]


def _mixer(x, shift_prev, wkv_prev, past_k, past_v, past_ki, w_perm, norm_w, prm, w_pa, w_pb, w_o,
           final_w):
    bsz, t_len, _ = x.shape
    x2d = x.reshape(bsz * t_len, D_MODEL)
    proj = _proj(x2d, norm_w, w_perm)
    proj3 = proj.reshape(bsz, t_len, N_PAD)

    shift4, shiftw = _rwkv_order(shift_prev)
    ya, wkv_new = _rwkv(proj, bsz, t_len, shift4, shiftw, wkv_prev, prm)

    k_new = proj3[:, :, C_KD:C_KD + 128]
    v_new = proj3[:, :, C_VD:C_VD + 128]
    ki_new = proj3[:, :, C_KIWI:C_KIWI + IDX_DIM]
    past_len = 0 if past_k is None else past_k.shape[1]
    if past_len == 0:
        yb = _dsa(proj, bsz, t_len, proj3, proj3, proj3, (C_KD // 128, C_VD // 128), C_KIREP // 256, 0)
    else:
        s_tot = past_len + t_len
        s_pad = -(-s_tot // KEY_BLOCK) * KEY_BLOCK
        pad = lambda a: jnp.pad(a, ((0, 0), (0, s_pad - s_tot), (0, 0)))
        k_all = pad(jnp.concatenate([past_k.reshape(bsz, past_len, 128), k_new], axis=1))
        v_all = pad(jnp.concatenate([past_v.reshape(bsz, past_len, 128), v_new], axis=1))
        kirep_all = pad(jnp.concatenate(
            [jnp.tile(past_ki, (1, 1, IDX_HEADS)), proj3[:, :, C_KIREP:C_KIREP + 256]], axis=1))
        yb = _dsa(proj, bsz, t_len, k_all, v_all, kirep_all, (0, 0), 0, past_len)

    y = _merge(x2d, ya, yb, proj, w_pa, w_pb, w_o, final_w).reshape(bsz, t_len, D_MODEL)
    last = proj3[:, t_len - 1:t_len, :]
    shift_new = jnp.concatenate([last[..., C_RKVG:C_RKVG + 2048], last[..., C_WDAD:C_WDAD + 128]], axis=-1)
    kv_shape = (bsz, t_len, B_KV_HEADS, B_HEAD_DIM)
    return y, k_new.reshape(kv_shape), v_new.reshape(kv_shape), ki_new, wkv_new, shift_new


def kernel(x_prompt, x_sample, cache_k, cache_v, cache_kidx, state_wkv, state_shift, norm_w, w_in,
           shift_mu, decay_w0, decay_up, iclr_a0, iclr_up, k_k, k_a, r_k, gn_w, gn_b, w_pa, w_pb,
           w_o, final_norm_w):
    assert w_in.shape[0] == 1, "the final norm is fused into the (single) layer's merge kernel"
    bp = x_prompt.shape[0]
    w_perm = _permute_w_in(w_in[0])
    mu4, muw = _rwkv_order(shift_mu[0].reshape(1, RWKV_COLS))
    row = lambda a: a.reshape(1, A_WIDTH)
    prm = dict(mu4=mu4, muw=muw, w0=row(decay_w0[0]), dup=decay_up[0], a0=row(iclr_a0[0]),
               aup=iclr_up[0], kk=row(k_k[0]), ka=row(k_a[0]), rk=row(r_k[0]), gnw=row(gn_w[0]),
               gnb=row(gn_b[0]))
    common = (w_perm, norm_w[0], prm, w_pa[0], w_pb[0], w_o[0], final_norm_w)
    yp, kp, vp, kip, wkvp, shp = _mixer(
        x_prompt, jnp.zeros((bp, 1, RWKV_COLS), F32),
        jnp.zeros((bp, A_HEADS, A_HEAD_DIM, A_HEAD_DIM), F32), None, None, None, *common)
    ys, ks, vs, kis, wkvs, shs = _mixer(
        x_sample, state_shift[0], state_wkv[0], cache_k[0], cache_v[0], cache_kidx[0], *common)
    st = lambda a: a[None]
    return (yp, ys, st(kp), st(vp), st(kip), st(wkvp), st(shp),
            st(ks), st(vs), st(kis), st(wkvs), st(shs))
```

```python
import functools

import jax
import jax.numpy as jnp
from jax import lax
from jax.experimental import pallas as pl
from jax.experimental.pallas import tpu as pltpu

F32 = jnp.float32
BF16 = jnp.bfloat16
I32 = jnp.int32
HIGHEST = lax.Precision.HIGHEST

D_MODEL = 1024
CHUNK = 64
A_HEADS = 8
A_HEAD_DIM = 64
A_WIDTH = 512
LORA = 64
RWKV_COLS = 4 * A_WIDTH + 2 * LORA
B_HEADS = 8
B_KV_HEADS = 2
B_GROUP = 4
B_HEAD_DIM = 64
B_WIDTH = 512
B_KV_WIDTH = 128
IDX_HEADS = 8
IDX_DIM = 32
MAX_TOPK = 256
DSA_COLS = 1576
N_IN = 5800
NORM_EPS = 1e-6
GN_EPS = 64e-5

C_GATES = 0
C_RKVG = 2048
C_Q = 4096
C_GD = 4608
C_QI = 5120
C_KD = 5376
C_VD = 5504
C_KIWI = 5632
C_WDAD = 5760
N_PAD = 5888

KEY_BLOCK = 512
INT_MIN = -(2 ** 31)
LOWEST_FINITE_KEY = INT_MIN + 0x00800000
NEG_BIG = -1e30
VMEM_LIMIT = 56 * 1024 * 1024


def _sigmoid(x):
    return 1.0 / (1.0 + jnp.exp(-x))


def _dot_nt(a, b, **kw):
    return lax.dot_general(a, b, (((1,), (1,)), ((), ())), **kw)


def _dot_tn(a, b, **kw):
    return lax.dot_general(a, b, (((0,), (0,)), ((), ())), **kw)


def _permute_w_in(w):
    d0 = RWKV_COLS
    g0 = RWKV_COLS + DSA_COLS
    cols = [
        w[:, g0:g0 + 2048],
        w[:, 0:2048],
        w[:, d0:d0 + 512],
        w[:, d0 + 1064:d0 + 1576],
        w[:, d0 + 768:d0 + 1024],
        w[:, d0 + 512:d0 + 640],
        w[:, d0 + 640:d0 + 768],
        w[:, d0 + 1024:d0 + 1064],
        jnp.zeros((w.shape[0], 128 - IDX_DIM - IDX_HEADS), w.dtype),
        w[:, 2048:2176],
    ]
    return jnp.concatenate(cols, axis=1).astype(BF16)


def _proj_kernel(x_ref, nw_ref, w_ref, o_ref):
    x = x_ref[...]
    ms = jnp.mean(x * x, axis=-1, keepdims=True)
    h = (x * lax.rsqrt(ms + NORM_EPS)) * nw_ref[...]
    o_ref[...] = jnp.dot(h.astype(BF16), w_ref[...], preferred_element_type=F32)


def _proj(x2d, norm_w, w_perm):
    m = x2d.shape[0]
    tm = min(512, m)
    tn = N_PAD // 2
    return pl.pallas_call(
        _proj_kernel,
        grid=(N_PAD // tn, m // tm),
        in_specs=[
            pl.BlockSpec((tm, D_MODEL), lambda j, i: (i, 0)),
            pl.BlockSpec((1, D_MODEL), lambda j, i: (0, 0)),
            pl.BlockSpec((D_MODEL, tn), lambda j, i: (0, j)),
        ],
        out_specs=pl.BlockSpec((tm, tn), lambda j, i: (i, j)),
        out_shape=jax.ShapeDtypeStruct((m, N_PAD), F32),
        compiler_params=pltpu.CompilerParams(
            dimension_semantics=("arbitrary", "arbitrary"), vmem_limit_bytes=VMEM_LIMIT),
        name="proj",
    )(x2d, norm_w.reshape(1, D_MODEL), w_perm)


def _split_bf16(x, terms):
    pieces = []
    for _ in range(terms):
        piece = x.astype(BF16)
        pieces.append(piece)
        x = x - piece.astype(F32)
    return pieces


def _dot_exact_rhs(a, b_bf16, terms):
    acc = None
    for piece in _split_bf16(a, terms):
        d = jnp.dot(piece, b_bf16, preferred_element_type=F32)
        acc = d if acc is None else acc + d
    return acc


def _dot_3pass(a, b):
    ah, al = _split_bf16(a, 2)
    bh, bl = _split_bf16(b, 2)
    dot = functools.partial(jnp.dot, preferred_element_type=F32)
    return dot(ah, bh) + dot(ah, bl) + dot(al, bh)


def _rwkv_kernel(p4_ref, pw_ref, sp4_ref, spw_ref, s0_ref, mu4_ref, muw_ref, w0_ref, dup_ref,
                 a0_ref, aup_ref, kk_ref, ka_ref, rk_ref, gnw_ref, gnb_ref, bd_ref, tri_ref,
                 ya_ref, sout_ref, c4_ref, cw_ref, sbd_ref, *, nch):
    c = CHUNK
    n = A_HEAD_DIM
    rows = nch * c
    n_pairs = A_HEADS // 2
    t_idx = pl.program_id(1)
    lane = lax.broadcasted_iota(I32, (c, 2 * n), 1)
    lo_half = lane < n
    row_c = lax.broadcasted_iota(I32, (c, 2 * n), 0)
    pos_in_head = jnp.where(lo_half, lane, lane - n)
    tri_strict = (pos_in_head < row_c).astype(F32)
    tri_incl = (pos_in_head <= row_c).astype(F32)
    eye2 = (pos_in_head == row_c).astype(F32)
    lane_sq = lax.broadcasted_iota(I32, (2 * n, 2 * n), 1)
    row_sq = lax.broadcasted_iota(I32, (2 * n, 2 * n), 0)
    same_head = (lane_sq < n) == (row_sq < n)
    dot = functools.partial(jnp.dot, preferred_element_type=F32)

    def bdiag(x):
        zero = jnp.zeros_like(x)
        return jnp.concatenate([jnp.where(lo_half, x, zero), jnp.where(lo_half, zero, x)], axis=0)

    @pl.when(t_idx == 0)
    def _():
        c4_ref[...] = sp4_ref[0]
        cw_ref[...] = spw_ref[0]
        zeros = jnp.zeros((n, n), F32)
        for j in range(n_pairs):
            sbd_ref[j] = jnp.concatenate(
                [jnp.concatenate([s0_ref[0, 2 * j], zeros], axis=1),
                 jnp.concatenate([zeros, s0_ref[0, 2 * j + 1]], axis=1)], axis=0)

    p4 = p4_ref[...]
    pw = pw_ref[...]
    row = lax.broadcasted_iota(I32, (rows, 1), 0)
    prev4 = jnp.where(row == 0, c4_ref[...], pltpu.roll(p4, 1, 0))
    prevw = jnp.where(row == 0, cw_ref[...], pltpu.roll(pw, 1, 0))
    c4_ref[...] = p4[rows - 1:rows, :]
    cw_ref[...] = pw[rows - 1:rows, :]
    ps4 = p4 + (prev4 - p4) * mu4_ref[...]
    psw = pw + (prevw - pw) * muw_ref[...]
    r = ps4[:, 0:512]
    k = ps4[:, 512:1024]
    v = ps4[:, 1024:1536]
    g = ps4[:, 1536:2048]
    wd = psw[:, 0:LORA]
    ad = psw[:, LORA:2 * LORA]
    bd = bd_ref[...]

    xw = w0_ref[...] + _dot_3pass(jnp.tanh(wd), dup_ref[...])
    z = -xw
    softplus = jnp.maximum(z, 0.0) + jnp.log(1.0 + jnp.exp(-jnp.abs(z)))
    lw = -jnp.exp(-softplus - 0.5)
    a = _sigmoid(a0_ref[...] + _dot_3pass(ad, aup_ref[...]))
    kkr = k * kk_ref[...]
    kkn = kkr / jnp.maximum(jnp.sqrt(_dot_exact_rhs(kkr * kkr, bd, 2)), 1e-12)
    kmod = k * (1.0 + (a - 1.0) * ka_ref[...])

    cum = None
    for piece in _split_bf16(lw, 3):
        d = dot(tri_ref[...], piece)
        cum = d if cum is None else cum + d
    pdec = jnp.exp(cum)
    pinv = jnp.exp(-cum)
    rt = (r * pdec).astype(BF16)
    at = (-kkn * jnp.exp(cum - lw)).astype(BF16)
    bt = (kkn * a * pinv).astype(BF16)
    kt = (kmod * pinv).astype(BF16)
    vb = v.astype(BF16)

    tiles = [(ci, j) for ci in range(nch) for j in range(n_pairs)]
    rsl = lambda ci: slice(ci * c, (ci + 1) * c)
    lsl = lambda j: slice(j * 2 * n, (j + 1) * 2 * n)
    lhs, bk, a_ak_rk, a_rb, xs, tinv = {}, {}, {}, {}, {}, {}
    for ci, j in tiles:
        rs, ls = rsl(ci), lsl(j)
        lhs[ci, j] = jnp.concatenate([at[rs, ls], rt[rs, ls]], axis=0)
        bk[ci, j] = jnp.concatenate([bt[rs, ls], kt[rs, ls]], axis=0)
        amat = _dot_nt(lhs[ci, j], jnp.concatenate([bdiag(bt[rs, ls]), bdiag(kt[rs, ls])], axis=0),
                       preferred_element_type=F32)
        xs[ci, j] = amat[0:c, 0:2 * n] * tri_strict
        a_ak_rk[ci, j] = jnp.concatenate([amat[0:c, 2 * n:4 * n] * tri_strict,
                                          amat[c:2 * c, 2 * n:4 * n] * tri_incl], axis=0).astype(BF16)
        a_rb[ci, j] = (amat[c:2 * c, 0:2 * n] * tri_incl).astype(BF16)
        tinv[ci, j] = eye2 + xs[ci, j]
    for t in tiles:
        xb = xs[t].astype(BF16)
        xs[t] = dot(xb, bdiag(xb))
    for _ in range(4):
        for t in tiles:
            xb = xs[t].astype(BF16)
            both = dot(xb, jnp.concatenate([bdiag(tinv[t].astype(BF16)), bdiag(xb)], axis=1))
            tinv[t] = tinv[t] + both[:, 0:2 * n]
            xs[t] = both[:, 2 * n:4 * n]
    for t in tiles:
        tinv[t] = tinv[t] + dot(xs[t].astype(BF16), bdiag(tinv[t].astype(BF16)))
    akv = {}
    for ci, j in tiles:
        tinv[ci, j] = tinv[ci, j].astype(BF16)
        akv[ci, j] = dot(a_ak_rk[ci, j], bdiag(vb[rsl(ci), lsl(j)]))

    pairs = range(n_pairs)
    s_pair = [sbd_ref[j] for j in pairs]
    y_chunks = []
    for ci in range(nch):
        from_state = [_dot_nt(lhs[ci, j], s_pair[j].astype(BF16), preferred_element_type=F32)
                      for j in pairs]
        u = [dot(tinv[ci, j], bdiag((from_state[j][0:c] + akv[ci, j][0:c]).astype(BF16)))
             for j in pairs]
        uv_t = [jnp.transpose(jnp.concatenate([u[j], v[rsl(ci), lsl(j)]], axis=0)).astype(BF16)
                for j in pairs]
        upd = [dot(uv_t[j], bk[ci, j]) for j in pairs]
        s_pair = [(s_pair[j] + jnp.where(same_head, upd[j], 0.0))
                  * pdec[(ci + 1) * c - 1:(ci + 1) * c, lsl(j)] for j in pairs]
        y_chunks.append(jnp.concatenate(
            [from_state[j][c:2 * c] + akv[ci, j][c:2 * c] + dot(a_rb[ci, j], bdiag(u[j].astype(BF16)))
             for j in pairs], axis=1))
    for j in pairs:
        sbd_ref[j] = s_pair[j]
    y = y_chunks[0] if nch == 1 else jnp.concatenate(y_chunks, axis=0)

    inv_n = 1.0 / n
    mean = _dot_exact_rhs(y, bd, 2) * inv_n
    dlt = y - mean
    var = _dot_exact_rhs(dlt * dlt, bd, 2) * inv_n
    yn = dlt * lax.rsqrt(var + GN_EPS) * gnw_ref[...] + gnb_ref[...]
    yn = yn + _dot_exact_rhs(r * kmod * rk_ref[...], bd, 2) * v
    ya_ref[...] = yn * (g * _sigmoid(g))

    @pl.when(t_idx == pl.num_programs(1) - 1)
    def _():
        for j in range(n_pairs):
            s_pair = sbd_ref[j]
            sout_ref[0, 2 * j] = s_pair[0:n, 0:n]
            sout_ref[0, 2 * j + 1] = s_pair[n:2 * n, n:2 * n]


def _rwkv(proj, bsz, t_len, shift4, shiftw, wkv_prev, prm):
    nch = min(4, t_len // CHUNK)
    rows = nch * CHUNK
    nt = t_len // rows
    row1 = lambda width: pl.BlockSpec((1, width), lambda b, t: (0, 0))
    head = lax.broadcasted_iota(I32, (A_WIDTH, A_WIDTH), 0) // A_HEAD_DIM
    bd = (head == head.T).astype(BF16)
    ti = lax.broadcasted_iota(I32, (rows, rows), 0)
    si = lax.broadcasted_iota(I32, (rows, rows), 1)
    tri = ((ti // CHUNK == si // CHUNK) & (si <= ti)).astype(BF16)
    return pl.pallas_call(
        functools.partial(_rwkv_kernel, nch=nch),
        grid=(bsz, nt),
        in_specs=[
            pl.BlockSpec((rows, 2048), lambda b, t: (b * nt + t, C_RKVG // 2048)),
            pl.BlockSpec((rows, 128), lambda b, t: (b * nt + t, C_WDAD // 128)),
            pl.BlockSpec((1, 1, 2048), lambda b, t: (b, 0, 0)),
            pl.BlockSpec((1, 1, 128), lambda b, t: (b, 0, 0)),
            pl.BlockSpec((1, A_HEADS, A_HEAD_DIM, A_HEAD_DIM), lambda b, t: (b, 0, 0, 0)),
            row1(2048), row1(128), row1(A_WIDTH),
            pl.BlockSpec((LORA, A_WIDTH), lambda b, t: (0, 0)),
            row1(A_WIDTH),
            pl.BlockSpec((LORA, A_WIDTH), lambda b, t: (0, 0)),
            row1(A_WIDTH), row1(A_WIDTH), row1(A_WIDTH), row1(A_WIDTH), row1(A_WIDTH),
            pl.BlockSpec((A_WIDTH, A_WIDTH), lambda b, t: (0, 0)),
            pl.BlockSpec((rows, rows), lambda b, t: (0, 0)),
        ],
        out_specs=[
            pl.BlockSpec((rows, A_WIDTH), lambda b, t: (b * nt + t, 0)),
            pl.BlockSpec((1, A_HEADS, A_HEAD_DIM, A_HEAD_DIM), lambda b, t: (b, 0, 0, 0)),
        ],
        out_shape=[
            jax.ShapeDtypeStruct((bsz * t_len, A_WIDTH), F32),
            jax.ShapeDtypeStruct((bsz, A_HEADS, A_HEAD_DIM, A_HEAD_DIM), F32),
        ],
        scratch_shapes=[pltpu.VMEM((1, 2048), F32), pltpu.VMEM((1, 128), F32),
                        pltpu.VMEM((A_HEADS // 2, 2 * A_HEAD_DIM, 2 * A_HEAD_DIM), F32)],
        compiler_params=pltpu.CompilerParams(
            dimension_semantics=("arbitrary", "arbitrary"), vmem_limit_bytes=VMEM_LIMIT),
        name="rwkv",
    )(proj, proj, shift4, shiftw, wkv_prev, prm["mu4"], prm["muw"], prm["w0"], prm["dup"],
      prm["a0"], prm["aup"], prm["kk"], prm["ka"], prm["rk"], prm["gnw"], prm["gnb"], bd, tri)


def _dsa_kernel(q_ref, qi_ref, kiwi_ref, k_ref, v_ref, ki_ref, posf_ref, o_ref, keys_ref, s_ref,
                macc_ref, oacc_ref, *, pos0, kb_w, topk, tq, th):
    qt = pl.program_id(1)
    tile_pos = pos0 + qt * tq
    row_chunk = jnp.right_shift(lax.broadcasted_iota(I32, (tq, 1), 0), 6)
    n_adm = tile_pos + (row_chunk + 1) * CHUNK
    nkb = (tile_pos + tq + kb_w - 1) // kb_w
    idx_scale = (IDX_HEADS ** -0.5) * (IDX_DIM ** -0.5)
    att_scale = B_HEAD_DIM ** -0.5
    topk = float(topk)
    lane_q128 = lax.broadcasted_iota(I32, (1, 128), 1)
    lane_k = lax.broadcasted_iota(I32, (1, kb_w), 1)

    def index_part(r0):
        qi = qi_ref[r0:r0 + th, :]
        wi = kiwi_ref[r0:r0 + th, IDX_DIM:IDX_DIM + IDX_HEADS]
        heads = []
        for h in range(IDX_HEADS):
            slab = qi[:, 128 * (h // 4):128 * (h // 4) + 128]
            if h % 4:
                slab = pltpu.roll(slab, 128 - IDX_DIM * (h % 4), 1)
            heads.append(jnp.where(lane_q128 < IDX_DIM, slab, 0.0))
        qis = jnp.concatenate(heads, axis=0).astype(BF16)
        wis = jnp.concatenate([wi[:, h:h + 1] for h in range(IDX_HEADS)], axis=0)
        n_adm_part = n_adm[r0:r0 + th]

        def score_block(kb, carry):
            off = pl.multiple_of(kb * kb_w, kb_w)
            kir = ki_ref[0, pl.ds(off, kb_w), :].astype(BF16)
            s = _dot_nt(qis, kir, preferred_element_type=F32)
            s = jnp.maximum(s, 0.0) * wis
            isc = s[0:th]
            for h in range(1, IDX_HEADS):
                isc = isc + s[h * th:(h + 1) * th]
            isc = isc * idx_scale
            keys_ref[r0:r0 + th, pl.ds(off, kb_w)] = jnp.where(off + lane_k < n_adm_part, isc, -jnp.inf)
            return carry

        lax.fori_loop(0, nkb, score_block, 0)

    for r0 in range(0, tq, th):
        index_part(r0)

    def key_to_score(key):
        return pltpu.bitcast(jnp.where(key < 0, key ^ 0x7FFFFFFF, key), F32)

    def count_ge(cand_key):
        accs = []
        for r0 in range(0, tq, th):
            cand_part = key_to_score(cand_key[r0:r0 + th])

            def body(kb, acc, r0=r0, cand_part=cand_part):
                off = pl.multiple_of(kb * kb_w, kb_w)
                kblk = keys_ref[r0:r0 + th, pl.ds(off, kb_w)]
                for j in range(kb_w // 128):
                    acc = acc + jnp.where(kblk[:, j * 128:(j + 1) * 128] >= cand_part, 1.0, 0.0)
                return acc

            accs.append(lax.fori_loop(0, nkb, body, jnp.zeros((th, 128), F32)))
        acc = accs[0] if len(accs) == 1 else jnp.concatenate(accs, axis=0)
        return jnp.sum(acc, axis=1, keepdims=True)

    c0 = count_ge(jnp.zeros((tq, 128), I32))
    t0 = jnp.where(c0 >= topk, jnp.zeros((tq, 128), I32), jnp.full((tq, 128), INT_MIN, I32))
    n0 = jnp.where(c0 >= topk, c0, n_adm.astype(F32))

    def bit_step(i, carry):
        t, n_t = carry
        cand = t | jnp.left_shift(jnp.int32(1), 30 - i)
        n_cand = count_ge(cand)
        take = n_cand >= topk
        return jnp.where(take, cand, t), jnp.where(take, n_cand, n_t)

    thr_key, n_ge = lax.fori_loop(0, 31, bit_step, (t0, n0))
    thr_key = jnp.maximum(thr_key, LOWEST_FINITE_KEY)
    thr = key_to_score(thr_key)[:, 0:1]

    @pl.when(jnp.max(n_ge) > topk)
    def _():
        n_tie_take = topk - count_ge(thr_key + 1)
        ri = lax.broadcasted_iota(I32, (kb_w, kb_w), 0)
        ci = lax.broadcasted_iota(I32, (kb_w, kb_w), 1)
        upper = (ri <= ci).astype(BF16)

        def body(kb, seen):
            off = pl.multiple_of(kb * kb_w, kb_w)
            kblk = keys_ref[:, pl.ds(off, kb_w)]
            tie = kblk == thr
            rank = seen + jnp.dot(jnp.where(tie, 1.0, 0.0).astype(BF16), upper,
                                  preferred_element_type=F32)
            keys_ref[:, pl.ds(off, kb_w)] = jnp.where(tie & (rank > n_tie_take), -jnp.inf, kblk)
            return rank[:, kb_w - 1:kb_w]

        lax.fori_loop(0, nkb, body, jnp.zeros((tq, 1), F32))

    rows = B_GROUP * th
    lane = lax.broadcasted_iota(I32, (th, 128), 1)

    def fold_lanes(x, op):
        part = x[:, 0:128]
        for j in range(1, kb_w // 128):
            part = op(part, x[:, j * 128:(j + 1) * 128])
        return part

    def attend_part(r0):
        q = q_ref[r0:r0 + th, :]
        thr_part = thr[r0:r0 + th]
        q_chunk = (tile_pos // CHUNK + row_chunk[r0:r0 + th]).astype(F32)
        q_row = (lax.broadcasted_iota(I32, (th, 128), 0) & (CHUNK - 1)).astype(F32)
        qpos = tile_pos + r0 + lax.broadcasted_iota(I32, (th, 1), 0)
        qaug, slope2, own_half = [], [], []
        for n in range(B_KV_HEADS):
            q_parts, slope_parts = [], []
            keep = (lane >= n * B_HEAD_DIM) & (lane < (n + 1) * B_HEAD_DIM)
            own_half.append((lane_q128 >= n * B_HEAD_DIM) & (lane_q128 < (n + 1) * B_HEAD_DIM))
            pos_lane = lane - (1 - n) * B_HEAD_DIM
            for g in range(B_GROUP):
                h = n * B_GROUP + g
                slope = 2.0 ** (-(8.0 / B_HEADS) * (h + 1))
                slab = q[:, 128 * (h // 2):128 * (h // 2) + 128]
                if h % 2 != n:
                    slab = pltpu.roll(slab, B_HEAD_DIM, 1)
                pos_feat = jnp.where(pos_lane == 0, CHUNK * slope,
                           jnp.where(pos_lane == 1, slope,
                           jnp.where(pos_lane == 2, -CHUNK * slope * q_chunk,
                           jnp.where(pos_lane == 3, -slope * q_row, 0.0))))
                q_parts.append(jnp.where(keep, slab * att_scale, pos_feat))
                slope_parts.append(jnp.full((th, 1), 2.0 * slope, F32))
            qaug.append(jnp.concatenate(q_parts, axis=0).astype(BF16))
            slope2.append(jnp.concatenate(slope_parts, axis=0))

        macc_ref[...] = jnp.full(macc_ref.shape, NEG_BIG, F32)
        oacc_ref[...] = jnp.zeros(oacc_ref.shape, F32)

        def score_pass(kb, own_chunk):
            off = pl.multiple_of(kb * kb_w, kb_w)
            sel = keys_ref[r0:r0 + th, pl.ds(off, kb_w)] >= thr_part
            sel4 = jnp.concatenate([sel] * B_GROUP, axis=0)
            kblk = k_ref[0, pl.ds(off, kb_w), :].astype(BF16)
            pblk = posf_ref[pl.ds(off, kb_w), :]
            if own_chunk:
                ahead = jnp.maximum((off + lane_k) - qpos, 0).astype(F32)
                ahead4 = jnp.concatenate([ahead] * B_GROUP, axis=0)
            for n in range(B_KV_HEADS):
                kaug = jnp.where(own_half[n], kblk, pblk)
                s = _dot_nt(qaug[n], kaug, preferred_element_type=F32)
                if own_chunk:
                    s = s - slope2[n] * ahead4
                s = jnp.where(sel4, s, NEG_BIG)
                s_ref[n, :, pl.ds(off, kb_w)] = s
                macc_ref[n] = jnp.maximum(macc_ref[n], fold_lanes(s, jnp.maximum))

        def score_body(kb, carry):
            score_pass(kb, False)
            return carry

        lax.fori_loop(0, nkb - 1, score_body, 0)
        score_pass(nkb - 1, True)

        for n in range(B_KV_HEADS):
            m = jnp.max(macc_ref[n], axis=1, keepdims=True)
            macc_ref[n] = jnp.broadcast_to(m, (rows, 128))

        ones_blk = jnp.ones((kb_w, 128), BF16)

        def value_body(kb, carry):
            off = pl.multiple_of(kb * kb_w, kb_w)
            vaug = jnp.concatenate([v_ref[0, pl.ds(off, kb_w), :].astype(BF16), ones_blk], axis=1)
            for n in range(B_KV_HEADS):
                m_b = macc_ref[n]
                p = jnp.exp(s_ref[n, :, pl.ds(off, kb_w)]
                            - jnp.concatenate([m_b] * (kb_w // 128), axis=1))
                oacc_ref[n] = oacc_ref[n] + jnp.dot(p.astype(BF16), vaug, preferred_element_type=F32)
            return carry

        lax.fori_loop(0, nkb, value_body, 0)
        pieces = []
        for n in range(B_KV_HEADS):
            acc = oacc_ref[n]
            o_n = acc[:, n * B_HEAD_DIM:(n + 1) * B_HEAD_DIM] / acc[:, 128:129]
            pieces += [o_n[g * th:(g + 1) * th] for g in range(B_GROUP)]
        o_ref[r0:r0 + th, :] = jnp.concatenate(pieces, axis=1)

    for r0 in range(0, tq, th):
        attend_part(r0)


def _dsa(proj, bsz, t_len, k_all, v_all, ki_all, key_cols, pos0):
    kb_w = KEY_BLOCK
    tq = next(c for c in (4 * CHUNK, 2 * CHUNK, CHUNK) if t_len % c == 0 and pos0 % c == 0)
    th = min(tq, 2 * CHUNK)
    nq = t_len // tq
    s_pad = k_all.shape[1]
    assert s_pad % kb_w == 0 and pos0 + t_len <= s_pad and pos0 % tq == 0 and kb_w % tq == 0
    rows = B_GROUP * th
    kpos = lax.broadcasted_iota(I32, (s_pad, 128), 0)
    feat = lax.broadcasted_iota(I32, (s_pad, 128), 1)
    feat = feat % B_HEAD_DIM
    posf = jnp.where(feat == 0, kpos // CHUNK, jnp.where(feat == 1, kpos % CHUNK,
                     jnp.where(feat < 4, 1, 0))).astype(BF16)
    return pl.pallas_call(
        functools.partial(_dsa_kernel, pos0=pos0, kb_w=kb_w, tq=tq, th=th,
                          topk=min(MAX_TOPK, (pos0 + t_len) // 4)),
        grid=(bsz, nq),
        in_specs=[
            pl.BlockSpec((tq, 512), lambda b, t: (b * nq + t, C_Q // 512)),
            pl.BlockSpec((tq, 256), lambda b, t: (b * nq + t, C_QI // 256)),
            pl.BlockSpec((tq, 128), lambda b, t: (b * nq + t, C_KIWI // 128)),
            pl.BlockSpec((1, s_pad, 128), lambda b, t: (b, 0, key_cols[0])),
            pl.BlockSpec((1, s_pad, 128), lambda b, t: (b, 0, key_cols[1])),
            pl.BlockSpec((1, s_pad, 128), lambda b, t: (b, 0, key_cols[2])),
            pl.BlockSpec((s_pad, 128), lambda b, t: (0, 0)),
        ],
        out_specs=pl.BlockSpec((tq, B_WIDTH), lambda b, t: (b * nq + t, 0)),
        out_shape=jax.ShapeDtypeStruct((bsz * t_len, B_WIDTH), F32),
        scratch_shapes=[
            pltpu.VMEM((tq, s_pad), F32),
            pltpu.VMEM((B_KV_HEADS, rows, s_pad), F32),
            pltpu.VMEM((B_KV_HEADS, rows, 128), F32),
            pltpu.VMEM((B_KV_HEADS, rows, 256), F32),
        ],
        compiler_params=pltpu.CompilerParams(
            dimension_semantics=("arbitrary", "arbitrary"), vmem_limit_bytes=VMEM_LIMIT),
        name="dsa",
    )(proj, proj, proj, k_all, v_all, ki_all, posf)


def _merge_kernel(x_ref, ya_ref, yb_ref, gd_ref, gab_ref, wpa_ref, wpb_ref, wo_ref, fnw_ref, o_ref):
    gd = gd_ref[...]
    yb = yb_ref[...] * (gd * _sigmoid(gd))
    pa = jnp.dot(ya_ref[...].astype(BF16), wpa_ref[...], preferred_element_type=F32)
    pb = jnp.dot(yb.astype(BF16), wpb_ref[...], preferred_element_type=F32)
    merged = _sigmoid(gab_ref[:, 0:D_MODEL]) * pa + _sigmoid(gab_ref[:, D_MODEL:2 * D_MODEL]) * pb
    out = x_ref[...] + jnp.dot(merged.astype(BF16), wo_ref[...], preferred_element_type=F32)
    ms = jnp.mean(out * out, axis=-1, keepdims=True)
    o_ref[...] = (out * lax.rsqrt(ms + NORM_EPS)) * fnw_ref[...]


def _merge(x2d, ya, yb, proj, w_pa, w_pb, w_o, final_w):
    m = x2d.shape[0]
    tm = min(512, m)
    full = lambda shape: pl.BlockSpec(shape, lambda i: (0, 0))
    return pl.pallas_call(
        _merge_kernel,
        grid=(m // tm,),
        in_specs=[
            pl.BlockSpec((tm, D_MODEL), lambda i: (i, 0)),
            pl.BlockSpec((tm, A_WIDTH), lambda i: (i, 0)),
            pl.BlockSpec((tm, B_WIDTH), lambda i: (i, 0)),
            pl.BlockSpec((tm, 512), lambda i: (i, C_GD // 512)),
            pl.BlockSpec((tm, 2048), lambda i: (i, C_GATES // 2048)),
            full((A_WIDTH, D_MODEL)), full((B_WIDTH, D_MODEL)), full((D_MODEL, D_MODEL)),
            full((1, D_MODEL)),
        ],
        out_specs=pl.BlockSpec((tm, D_MODEL), lambda i: (i, 0)),
        out_shape=jax.ShapeDtypeStruct((m, D_MODEL), F32),
        compiler_params=pltpu.CompilerParams(
            dimension_semantics=("arbitrary",), vmem_limit_bytes=VMEM_LIMIT),
        name="merge",
    )(x2d, ya, yb, proj, proj, w_pa.astype(BF16), w_pb.astype(BF16), w_o.astype(BF16),
      final_w.reshape(1, D_MODEL))


def _rwkv_order(row):
    return row[..., 0:2048], row[..., 2048:2176]


def _mixer(x, shift_prev, wkv_prev, past_k, past_v, past_ki, w_perm, norm_w, prm, w_pa, w_pb, w_o,
           final_w):
    bsz, t_len, _ = x.shape
    x2d = x.reshape(bsz * t_len, D_MODEL)
    proj = _proj(x2d, norm_w, w_perm)
    proj3 = proj.reshape(bsz, t_len, N_PAD)

    shift4, shiftw = _rwkv_order(shift_prev)
    ya, wkv_new = _rwkv(proj, bsz, t_len, shift4, shiftw, wkv_prev, prm)

    k_new = proj3[:, :, C_KD:C_KD + 128]
    v_new = proj3[:, :, C_VD:C_VD + 128]
    ki_new = proj3[:, :, C_KIWI:C_KIWI + IDX_DIM]
    past_len = 0 if past_k is None else past_k.shape[1]
    if past_len == 0:
        yb = _dsa(proj, bsz, t_len, proj3, proj3, proj3, (C_KD // 128, C_VD // 128, C_KIWI // 128), 0)
    else:
        s_tot = past_len + t_len
        s_pad = -(-s_tot // KEY_BLOCK) * KEY_BLOCK
        pad = lambda a: jnp.pad(a, ((0, 0), (0, s_pad - s_tot), (0, 0)))
        k_all = pad(jnp.concatenate([past_k.reshape(bsz, past_len, 128), k_new], axis=1))
        v_all = pad(jnp.concatenate([past_v.reshape(bsz, past_len, 128), v_new], axis=1))
        ki_all = pad(jnp.concatenate(
            [jnp.pad(past_ki, ((0, 0), (0, 0), (0, 128 - IDX_DIM))), proj3[:, :, C_KIWI:C_KIWI + 128]],
            axis=1))
        yb = _dsa(proj, bsz, t_len, k_all, v_all, ki_all, (0, 0, 0), past_len)

    y = _merge(x2d, ya, yb, proj, w_pa, w_pb, w_o, final_w).reshape(bsz, t_len, D_MODEL)
    last = proj3[:, t_len - 1:t_len, :]
    shift_new = jnp.concatenate([last[..., C_RKVG:C_RKVG + 2048], last[..., C_WDAD:C_WDAD + 128]], axis=-1)
    kv_shape = (bsz, t_len, B_KV_HEADS, B_HEAD_DIM)
    return y, k_new.reshape(kv_shape), v_new.reshape(kv_shape), ki_new, wkv_new, shift_new


def kernel(x_prompt, x_sample, cache_k, cache_v, cache_kidx, state_wkv, state_shift, norm_w, w_in,
           shift_mu, decay_w0, decay_up, iclr_a0, iclr_up, k_k, k_a, r_k, gn_w, gn_b, w_pa, w_pb,
           w_o, final_norm_w):
    assert w_in.shape[0] == 1, "the final norm is fused into the (single) layer's merge kernel"
    bp = x_prompt.shape[0]
    w_perm = _permute_w_in(w_in[0])
    mu4, muw = _rwkv_order(shift_mu[0].reshape(1, RWKV_COLS))
    row = lambda a: a.reshape(1, A_WIDTH)
    prm = dict(mu4=mu4, muw=muw, w0=row(decay_w0[0]), dup=decay_up[0], a0=row(iclr_a0[0]),
               aup=iclr_up[0], kk=row(k_k[0]), ka=row(k_a[0]), rk=row(r_k[0]), gnw=row(gn_w[0]),
               gnb=row(gn_b[0]))
    common = (w_perm, norm_w[0], prm, w_pa[0], w_pb[0], w_o[0], final_norm_w)
    yp, kp, vp, kip, wkvp, shp = _mixer(
        x_prompt, jnp.zeros((bp, 1, RWKV_COLS), F32),
        jnp.zeros((bp, A_HEADS, A_HEAD_DIM, A_HEAD_DIM), F32), None, None, None, *common)
    ys, ks, vs, kis, wkvs, shs = _mixer(
        x_sample, state_shift[0], state_wkv[0], cache_k[0], cache_v[0], cache_kidx[0], *common)
    st = lambda a: a[None]
    return (yp, ys, st(kp), st(vp), st(kip), st(wkvp), st(shp),
            st(ks), st(vs), st(kis), st(wkvs), st(shs))
```

```python
import functools

import jax
import jax.numpy as jnp
from jax import lax
from jax.experimental import pallas as pl
from jax.experimental.pallas import tpu as pltpu

F32 = jnp.float32
BF16 = jnp.bfloat16
I32 = jnp.int32
HIGHEST = lax.Precision.HIGHEST

D_MODEL = 1024
CHUNK = 64
A_HEADS = 8
A_HEAD_DIM = 64
A_WIDTH = 512
LORA = 64
RWKV_COLS = 4 * A_WIDTH + 2 * LORA
B_HEADS = 8
B_KV_HEADS = 2
B_GROUP = 4
B_HEAD_DIM = 64
B_WIDTH = 512
B_KV_WIDTH = 128
IDX_HEADS = 8
IDX_DIM = 32
MAX_TOPK = 256
DSA_COLS = 1576
N_IN = 5800
NORM_EPS = 1e-6
GN_EPS = 64e-5

C_GATES = 0
C_RKVG = 2048
C_Q = 4096
C_GD = 4608
C_QI = 5120
C_KD = 5376
C_VD = 5504
C_KIWI = 5632
C_WDAD = 5760
N_PAD = 5888

KEY_BLOCK = 512
INT_MIN = -(2 ** 31)
LOWEST_FINITE_KEY = INT_MIN + 0x00800000
BRACKET_BITS = 25
BRACKET_SPAN = 1 << BRACKET_BITS
NEG_BIG = -1e30
VMEM_LIMIT = 56 * 1024 * 1024


def _sigmoid(x):
    return 1.0 / (1.0 + jnp.exp(-x))


def _dot_nt(a, b, **kw):
    return lax.dot_general(a, b, (((1,), (1,)), ((), ())), **kw)


def _dot_tn(a, b, **kw):
    return lax.dot_general(a, b, (((0,), (0,)), ((), ())), **kw)


def _permute_w_in(w):
    d0 = RWKV_COLS
    g0 = RWKV_COLS + DSA_COLS
    cols = [
        w[:, g0:g0 + 2048],
        w[:, 0:2048],
        w[:, d0:d0 + 512],
        w[:, d0 + 1064:d0 + 1576],
        w[:, d0 + 768:d0 + 1024],
        w[:, d0 + 512:d0 + 640],
        w[:, d0 + 640:d0 + 768],
        w[:, d0 + 1024:d0 + 1064],
        jnp.zeros((w.shape[0], 128 - IDX_DIM - IDX_HEADS), w.dtype),
        w[:, 2048:2176],
    ]
    return jnp.concatenate(cols, axis=1).astype(BF16)


def _proj_kernel(x_ref, nw_ref, w_ref, o_ref):
    x = x_ref[...]
    ms = jnp.mean(x * x, axis=-1, keepdims=True)
    h = (x * lax.rsqrt(ms + NORM_EPS)) * nw_ref[...]
    o_ref[...] = jnp.dot(h.astype(BF16), w_ref[...], preferred_element_type=F32)


def _proj(x2d, norm_w, w_perm):
    m = x2d.shape[0]
    tm = min(512, m)
    tn = N_PAD // 2
    return pl.pallas_call(
        _proj_kernel,
        grid=(N_PAD // tn, m // tm),
        in_specs=[
            pl.BlockSpec((tm, D_MODEL), lambda j, i: (i, 0)),
            pl.BlockSpec((1, D_MODEL), lambda j, i: (0, 0)),
            pl.BlockSpec((D_MODEL, tn), lambda j, i: (0, j)),
        ],
        out_specs=pl.BlockSpec((tm, tn), lambda j, i: (i, j)),
        out_shape=jax.ShapeDtypeStruct((m, N_PAD), F32),
        compiler_params=pltpu.CompilerParams(
            dimension_semantics=("arbitrary", "arbitrary"), vmem_limit_bytes=VMEM_LIMIT),
        name="proj",
    )(x2d, norm_w.reshape(1, D_MODEL), w_perm)


def _split_bf16(x, terms):
    pieces = []
    for _ in range(terms):
        piece = x.astype(BF16)
        pieces.append(piece)
        x = x - piece.astype(F32)
    return pieces


def _dot_exact_rhs(a, b_bf16, terms):
    acc = None
    for piece in _split_bf16(a, terms):
        d = jnp.dot(piece, b_bf16, preferred_element_type=F32)
        acc = d if acc is None else acc + d
    return acc


def _dot_3pass(a, b):
    ah, al = _split_bf16(a, 2)
    bh, bl = _split_bf16(b, 2)
    dot = functools.partial(jnp.dot, preferred_element_type=F32)
    return dot(ah, bh) + dot(ah, bl) + dot(al, bh)


def _rwkv_kernel(p4_ref, pw_ref, sp4_ref, spw_ref, s0_ref, mu4_ref, muw_ref, w0_ref, dup_ref,
                 a0_ref, aup_ref, kk_ref, ka_ref, rk_ref, gnw_ref, gnb_ref, bd_ref, tri_ref,
                 ya_ref, sout_ref, c4_ref, cw_ref, sbd_ref, *, nch):
    c = CHUNK
    n = A_HEAD_DIM
    rows = nch * c
    n_pairs = A_HEADS // 2
    t_idx = pl.program_id(1)
    lane = lax.broadcasted_iota(I32, (c, 2 * n), 1)
    lo_half = lane < n
    row_c = lax.broadcasted_iota(I32, (c, 2 * n), 0)
    pos_in_head = jnp.where(lo_half, lane, lane - n)
    tri_strict = (pos_in_head < row_c).astype(F32)
    tri_incl = (pos_in_head <= row_c).astype(F32)
    eye2 = (pos_in_head == row_c).astype(F32)
    lane_sq = lax.broadcasted_iota(I32, (2 * n, 2 * n), 1)
    row_sq = lax.broadcasted_iota(I32, (2 * n, 2 * n), 0)
    same_head = (lane_sq < n) == (row_sq < n)
    dot = functools.partial(jnp.dot, preferred_element_type=F32)

    def bdiag(x):
        zero = jnp.zeros_like(x)
        return jnp.concatenate([jnp.where(lo_half, x, zero), jnp.where(lo_half, zero, x)], axis=0)

    @pl.when(t_idx == 0)
    def _():
        c4_ref[...] = sp4_ref[0]
        cw_ref[...] = spw_ref[0]
        zeros = jnp.zeros((n, n), F32)
        for j in range(n_pairs):
            sbd_ref[j] = jnp.concatenate(
                [jnp.concatenate([s0_ref[0, 2 * j], zeros], axis=1),
                 jnp.concatenate([zeros, s0_ref[0, 2 * j + 1]], axis=1)], axis=0)

    p4 = p4_ref[...]
    pw = pw_ref[...]
    row = lax.broadcasted_iota(I32, (rows, 1), 0)
    prev4 = jnp.where(row == 0, c4_ref[...], pltpu.roll(p4, 1, 0))
    prevw = jnp.where(row == 0, cw_ref[...], pltpu.roll(pw, 1, 0))
    c4_ref[...] = p4[rows - 1:rows, :]
    cw_ref[...] = pw[rows - 1:rows, :]
    ps4 = p4 + (prev4 - p4) * mu4_ref[...]
    psw = pw + (prevw - pw) * muw_ref[...]
    r = ps4[:, 0:512]
    k = ps4[:, 512:1024]
    v = ps4[:, 1024:1536]
    g = ps4[:, 1536:2048]
    wd = psw[:, 0:LORA]
    ad = psw[:, LORA:2 * LORA]
    bd = bd_ref[...]

    xw = w0_ref[...] + _dot_3pass(jnp.tanh(wd), dup_ref[...])
    z = -xw
    softplus = jnp.maximum(z, 0.0) + jnp.log(1.0 + jnp.exp(-jnp.abs(z)))
    lw = -jnp.exp(-softplus - 0.5)
    a = _sigmoid(a0_ref[...] + _dot_3pass(ad, aup_ref[...]))
    kkr = k * kk_ref[...]
    kkn = kkr / jnp.maximum(jnp.sqrt(_dot_exact_rhs(kkr * kkr, bd, 2)), 1e-12)
    kmod = k * (1.0 + (a - 1.0) * ka_ref[...])

    cum = None
    for piece in _split_bf16(lw, 3):
        d = dot(tri_ref[...], piece)
        cum = d if cum is None else cum + d
    pdec = jnp.exp(cum)
    pinv = jnp.exp(-cum)
    rt = (r * pdec).astype(BF16)
    at = (-kkn * jnp.exp(cum - lw)).astype(BF16)
    bt = (kkn * a * pinv).astype(BF16)
    kt = (kmod * pinv).astype(BF16)
    vb = v.astype(BF16)

    tiles = [(ci, j) for ci in range(nch) for j in range(n_pairs)]
    rsl = lambda ci: slice(ci * c, (ci + 1) * c)
    lsl = lambda j: slice(j * 2 * n, (j + 1) * 2 * n)
    lhs, bk, a_ak_rk, a_rb, xs, tinv = {}, {}, {}, {}, {}, {}
    for ci, j in tiles:
        rs, ls = rsl(ci), lsl(j)
        lhs[ci, j] = jnp.concatenate([at[rs, ls], rt[rs, ls]], axis=0)
        bk[ci, j] = jnp.concatenate([bt[rs, ls], kt[rs, ls]], axis=0)
        amat = _dot_nt(lhs[ci, j], jnp.concatenate([bdiag(bt[rs, ls]), bdiag(kt[rs, ls])], axis=0),
                       preferred_element_type=F32)
        xs[ci, j] = amat[0:c, 0:2 * n] * tri_strict
        a_ak_rk[ci, j] = jnp.concatenate([amat[0:c, 2 * n:4 * n] * tri_strict,
                                          amat[c:2 * c, 2 * n:4 * n] * tri_incl], axis=0).astype(BF16)
        a_rb[ci, j] = (amat[c:2 * c, 0:2 * n] * tri_incl).astype(BF16)
        tinv[ci, j] = eye2 + xs[ci, j]
    for t in tiles:
        xb = xs[t].astype(BF16)
        xs[t] = dot(xb, bdiag(xb))
    for _ in range(4):
        for t in tiles:
            xb = xs[t].astype(BF16)
            both = dot(xb, jnp.concatenate([bdiag(tinv[t].astype(BF16)), bdiag(xb)], axis=1))
            tinv[t] = tinv[t] + both[:, 0:2 * n]
            xs[t] = both[:, 2 * n:4 * n]
    for t in tiles:
        tinv[t] = tinv[t] + dot(xs[t].astype(BF16), bdiag(tinv[t].astype(BF16)))
    akv = {}
    for ci, j in tiles:
        tinv[ci, j] = tinv[ci, j].astype(BF16)
        akv[ci, j] = dot(a_ak_rk[ci, j], bdiag(vb[rsl(ci), lsl(j)]))

    pairs = range(n_pairs)
    s_pair = [sbd_ref[j] for j in pairs]
    y_chunks = []
    for ci in range(nch):
        from_state = [_dot_nt(lhs[ci, j], s_pair[j].astype(BF16), preferred_element_type=F32)
                      for j in pairs]
        u = [dot(tinv[ci, j], bdiag((from_state[j][0:c] + akv[ci, j][0:c]).astype(BF16)))
             for j in pairs]
        uv_t = [jnp.transpose(jnp.concatenate([u[j], v[rsl(ci), lsl(j)]], axis=0)).astype(BF16)
                for j in pairs]
        upd = [dot(uv_t[j], bk[ci, j]) for j in pairs]
        s_pair = [(s_pair[j] + jnp.where(same_head, upd[j], 0.0))
                  * pdec[(ci + 1) * c - 1:(ci + 1) * c, lsl(j)] for j in pairs]
        y_chunks.append(jnp.concatenate(
            [from_state[j][c:2 * c] + akv[ci, j][c:2 * c] + dot(a_rb[ci, j], bdiag(u[j].astype(BF16)))
             for j in pairs], axis=1))
    for j in pairs:
        sbd_ref[j] = s_pair[j]
    y = y_chunks[0] if nch == 1 else jnp.concatenate(y_chunks, axis=0)

    inv_n = 1.0 / n
    mean = _dot_exact_rhs(y, bd, 2) * inv_n
    dlt = y - mean
    var = _dot_exact_rhs(dlt * dlt, bd, 2) * inv_n
    yn = dlt * lax.rsqrt(var + GN_EPS) * gnw_ref[...] + gnb_ref[...]
    yn = yn + _dot_exact_rhs(r * kmod * rk_ref[...], bd, 2) * v
    ya_ref[...] = yn * (g * _sigmoid(g))

    @pl.when(t_idx == pl.num_programs(1) - 1)
    def _():
        for j in range(n_pairs):
            s_pair = sbd_ref[j]
            sout_ref[0, 2 * j] = s_pair[0:n, 0:n]
            sout_ref[0, 2 * j + 1] = s_pair[n:2 * n, n:2 * n]


def _rwkv(proj, bsz, t_len, shift4, shiftw, wkv_prev, prm):
    nch = min(4, t_len // CHUNK)
    rows = nch * CHUNK
    nt = t_len // rows
    row1 = lambda width: pl.BlockSpec((1, width), lambda b, t: (0, 0))
    head = lax.broadcasted_iota(I32, (A_WIDTH, A_WIDTH), 0) // A_HEAD_DIM
    bd = (head == head.T).astype(BF16)
    ti = lax.broadcasted_iota(I32, (rows, rows), 0)
    si = lax.broadcasted_iota(I32, (rows, rows), 1)
    tri = ((ti // CHUNK == si // CHUNK) & (si <= ti)).astype(BF16)
    return pl.pallas_call(
        functools.partial(_rwkv_kernel, nch=nch),
        grid=(bsz, nt),
        in_specs=[
            pl.BlockSpec((rows, 2048), lambda b, t: (b * nt + t, C_RKVG // 2048)),
            pl.BlockSpec((rows, 128), lambda b, t: (b * nt + t, C_WDAD // 128)),
            pl.BlockSpec((1, 1, 2048), lambda b, t: (b, 0, 0)),
            pl.BlockSpec((1, 1, 128), lambda b, t: (b, 0, 0)),
            pl.BlockSpec((1, A_HEADS, A_HEAD_DIM, A_HEAD_DIM), lambda b, t: (b, 0, 0, 0)),
            row1(2048), row1(128), row1(A_WIDTH),
            pl.BlockSpec((LORA, A_WIDTH), lambda b, t: (0, 0)),
            row1(A_WIDTH),
            pl.BlockSpec((LORA, A_WIDTH), lambda b, t: (0, 0)),
            row1(A_WIDTH), row1(A_WIDTH), row1(A_WIDTH), row1(A_WIDTH), row1(A_WIDTH),
            pl.BlockSpec((A_WIDTH, A_WIDTH), lambda b, t: (0, 0)),
            pl.BlockSpec((rows, rows), lambda b, t: (0, 0)),
        ],
        out_specs=[
            pl.BlockSpec((rows, A_WIDTH), lambda b, t: (b * nt + t, 0)),
            pl.BlockSpec((1, A_HEADS, A_HEAD_DIM, A_HEAD_DIM), lambda b, t: (b, 0, 0, 0)),
        ],
        out_shape=[
            jax.ShapeDtypeStruct((bsz * t_len, A_WIDTH), F32),
            jax.ShapeDtypeStruct((bsz, A_HEADS, A_HEAD_DIM, A_HEAD_DIM), F32),
        ],
        scratch_shapes=[pltpu.VMEM((1, 2048), F32), pltpu.VMEM((1, 128), F32),
                        pltpu.VMEM((A_HEADS // 2, 2 * A_HEAD_DIM, 2 * A_HEAD_DIM), F32)],
        compiler_params=pltpu.CompilerParams(
            dimension_semantics=("arbitrary", "arbitrary"), vmem_limit_bytes=VMEM_LIMIT),
        name="rwkv",
    )(proj, proj, shift4, shiftw, wkv_prev, prm["mu4"], prm["muw"], prm["w0"], prm["dup"],
      prm["a0"], prm["aup"], prm["kk"], prm["ka"], prm["rk"], prm["gnw"], prm["gnb"], bd, tri)


def _dsa_kernel(q_ref, qi_ref, kiwi_ref, k_ref, v_ref, ki_ref, posf_ref, o_ref, keys_ref, s_ref,
                macc_ref, oacc_ref, *, pos0, kb_w, topk, tq, th):
    qt = pl.program_id(1)
    tile_pos = pos0 + qt * tq
    row_chunk = jnp.right_shift(lax.broadcasted_iota(I32, (tq, 1), 0), 6)
    n_adm = tile_pos + (row_chunk + 1) * CHUNK
    nkb = (tile_pos + tq + kb_w - 1) // kb_w
    idx_scale = (IDX_HEADS ** -0.5) * (IDX_DIM ** -0.5)
    att_scale = B_HEAD_DIM ** -0.5
    topk = float(topk)
    lane_q128 = lax.broadcasted_iota(I32, (1, 128), 1)
    lane_k = lax.broadcasted_iota(I32, (1, kb_w), 1)

    def index_part(r0):
        qi = qi_ref[r0:r0 + th, :]
        wi = kiwi_ref[r0:r0 + th, IDX_DIM:IDX_DIM + IDX_HEADS]
        heads = []
        for h in range(IDX_HEADS):
            slab = qi[:, 128 * (h // 4):128 * (h // 4) + 128]
            if h % 4:
                slab = pltpu.roll(slab, 128 - IDX_DIM * (h % 4), 1)
            heads.append(jnp.where(lane_q128 < IDX_DIM, slab, 0.0))
        qis = jnp.concatenate(heads, axis=0).astype(BF16)
        wis = jnp.concatenate([wi[:, h:h + 1] for h in range(IDX_HEADS)], axis=0)
        n_adm_part = n_adm[r0:r0 + th]

        def score_block(kb, top):
            off = pl.multiple_of(kb * kb_w, kb_w)
            kir = ki_ref[0, pl.ds(off, kb_w), :].astype(BF16)
            s = _dot_nt(qis, kir, preferred_element_type=F32)
            s = jnp.maximum(s, 0.0) * wis
            isc = s[0:th]
            for h in range(1, IDX_HEADS):
                isc = isc + s[h * th:(h + 1) * th]
            isc = jnp.where(off + lane_k < n_adm_part, isc * idx_scale, -jnp.inf)
            keys_ref[r0:r0 + th, pl.ds(off, kb_w)] = isc
            for j in range(kb_w // 128):
                top = jnp.maximum(top, isc[:, j * 128:(j + 1) * 128])
            return top

        top = lax.fori_loop(0, nkb, score_block, jnp.full((th, 128), -jnp.inf, F32))
        top = jnp.broadcast_to(jnp.max(top, axis=1, keepdims=True), (th, 128))
        bits = pltpu.bitcast(top, I32)
        return jnp.where(bits < 0, bits ^ 0x7FFFFFFF, bits)[:, 0:1]

    top_parts = [index_part(r0) for r0 in range(0, tq, th)]
    top_key = top_parts[0] if len(top_parts) == 1 else jnp.concatenate(top_parts, axis=0)

    def key_to_score(key):
        return pltpu.bitcast(jnp.where(key < 0, key ^ 0x7FFFFFFF, key), F32)

    def count_ge(cand_key):
        accs = []
        for r0 in range(0, tq, th):
            cand_part = key_to_score(cand_key[r0:r0 + th])

            def body(kb, acc, r0=r0, cand_part=cand_part):
                off = pl.multiple_of(kb * kb_w, kb_w)
                kblk = keys_ref[r0:r0 + th, pl.ds(off, kb_w)]
                for j in range(kb_w // 128):
                    acc = acc + jnp.where(kblk[:, j * 128:(j + 1) * 128] >= cand_part, 1.0, 0.0)
                return acc

            accs.append(lax.fori_loop(0, nkb, body, jnp.zeros((th, 128), F32)))
        acc = accs[0] if len(accs) == 1 else jnp.concatenate(accs, axis=0)
        return jnp.sum(acc, axis=1, keepdims=True)

    lo_near = jnp.broadcast_to(top_key, (tq, 128)) - (BRACKET_SPAN - 1)
    n_near = count_ge(lo_near)
    in_range = (top_key >= LOWEST_FINITE_KEY + BRACKET_SPAN) & (top_key <= 0x7F800000)
    near = jnp.min(jnp.where((n_near >= topk) & in_range, 1.0, 0.0)) > 0.0
    lo = jnp.where(near, lo_near, LOWEST_FINITE_KEY)
    n_lo = jnp.where(near, n_near, n_adm.astype(F32))
    n_bits = jnp.where(near, BRACKET_BITS, 32)

    def bit_step(i, carry):
        d, n_t = carry
        d_cand = d | jnp.left_shift(jnp.int32(1), n_bits - 1 - i)
        n_cand = count_ge(lo + d_cand)
        take = n_cand >= topk
        return jnp.where(take, d_cand, d), jnp.where(take, n_cand, n_t)

    d_fin, n_ge = lax.fori_loop(0, n_bits, bit_step, (jnp.zeros((tq, 128), I32), n_lo))
    thr_key = lo + d_fin
    thr = key_to_score(thr_key)[:, 0:1]

    @pl.when(jnp.max(n_ge) > topk)
    def _():
        n_tie_take = topk - count_ge(thr_key + 1)
        ri = lax.broadcasted_iota(I32, (kb_w, kb_w), 0)
        ci = lax.broadcasted_iota(I32, (kb_w, kb_w), 1)
        upper = (ri <= ci).astype(BF16)

        def body(kb, seen):
            off = pl.multiple_of(kb * kb_w, kb_w)
            kblk = keys_ref[:, pl.ds(off, kb_w)]
            tie = kblk == thr
            rank = seen + jnp.dot(jnp.where(tie, 1.0, 0.0).astype(BF16), upper,
                                  preferred_element_type=F32)
            keys_ref[:, pl.ds(off, kb_w)] = jnp.where(tie & (rank > n_tie_take), -jnp.inf, kblk)
            return rank[:, kb_w - 1:kb_w]

        lax.fori_loop(0, nkb, body, jnp.zeros((tq, 1), F32))

    rows = B_GROUP * th
    lane = lax.broadcasted_iota(I32, (th, 128), 1)

    def fold_lanes(x, op):
        part = x[:, 0:128]
        for j in range(1, kb_w // 128):
            part = op(part, x[:, j * 128:(j + 1) * 128])
        return part

    def attend_part(r0):
        q = q_ref[r0:r0 + th, :]
        thr_part = thr[r0:r0 + th]
        q_chunk = (tile_pos // CHUNK + row_chunk[r0:r0 + th]).astype(F32)
        q_row = (lax.broadcasted_iota(I32, (th, 128), 0) & (CHUNK - 1)).astype(F32)
        qpos = tile_pos + r0 + lax.broadcasted_iota(I32, (th, 1), 0)
        qaug, slope2, own_half = [], [], []
        for n in range(B_KV_HEADS):
            q_parts, slope_parts = [], []
            keep = (lane >= n * B_HEAD_DIM) & (lane < (n + 1) * B_HEAD_DIM)
            own_half.append((lane_q128 >= n * B_HEAD_DIM) & (lane_q128 < (n + 1) * B_HEAD_DIM))
            pos_lane = lane - (1 - n) * B_HEAD_DIM
            for g in range(B_GROUP):
                h = n * B_GROUP + g
                slope = 2.0 ** (-(8.0 / B_HEADS) * (h + 1))
                slab = q[:, 128 * (h // 2):128 * (h // 2) + 128]
                if h % 2 != n:
                    slab = pltpu.roll(slab, B_HEAD_DIM, 1)
                pos_feat = jnp.where(pos_lane == 0, CHUNK * slope,
                           jnp.where(pos_lane == 1, slope,
                           jnp.where(pos_lane == 2, -CHUNK * slope * q_chunk,
                           jnp.where(pos_lane == 3, -slope * q_row, 0.0))))
                q_parts.append(jnp.where(keep, slab * att_scale, pos_feat))
                slope_parts.append(jnp.full((th, 1), 2.0 * slope, F32))
            qaug.append(jnp.concatenate(q_parts, axis=0).astype(BF16))
            slope2.append(jnp.concatenate(slope_parts, axis=0))

        macc_ref[...] = jnp.full(macc_ref.shape, NEG_BIG, F32)
        oacc_ref[...] = jnp.zeros(oacc_ref.shape, F32)

        def score_pass(kb, own_chunk):
            off = pl.multiple_of(kb * kb_w, kb_w)
            sel = keys_ref[r0:r0 + th, pl.ds(off, kb_w)] >= thr_part
            sel4 = jnp.concatenate([sel] * B_GROUP, axis=0)
            kblk = k_ref[0, pl.ds(off, kb_w), :].astype(BF16)
            pblk = posf_ref[pl.ds(off, kb_w), :]
            if own_chunk:
                ahead = jnp.maximum((off + lane_k) - qpos, 0).astype(F32)
                ahead4 = jnp.concatenate([ahead] * B_GROUP, axis=0)
            for n in range(B_KV_HEADS):
                kaug = jnp.where(own_half[n], kblk, pblk)
                s = _dot_nt(qaug[n], kaug, preferred_element_type=F32)
                if own_chunk:
                    s = s - slope2[n] * ahead4
                s = jnp.where(sel4, s, NEG_BIG)
                s_ref[n, :, pl.ds(off, kb_w)] = s
                macc_ref[n] = jnp.maximum(macc_ref[n], fold_lanes(s, jnp.maximum))

        def score_body(kb, carry):
            score_pass(kb, False)
            return carry

        lax.fori_loop(0, nkb - 1, score_body, 0)
        score_pass(nkb - 1, True)

        for n in range(B_KV_HEADS):
            m = jnp.max(macc_ref[n], axis=1, keepdims=True)
            macc_ref[n] = jnp.broadcast_to(m, (rows, 128))

        ones_blk = jnp.ones((kb_w, 128), BF16)

        def value_body(kb, carry):
            off = pl.multiple_of(kb * kb_w, kb_w)
            vaug = jnp.concatenate([v_ref[0, pl.ds(off, kb_w), :].astype(BF16), ones_blk], axis=1)
            for n in range(B_KV_HEADS):
                m_b = macc_ref[n]
                p = jnp.exp(s_ref[n, :, pl.ds(off, kb_w)]
                            - jnp.concatenate([m_b] * (kb_w // 128), axis=1))
                oacc_ref[n] = oacc_ref[n] + jnp.dot(p.astype(BF16), vaug, preferred_element_type=F32)
            return carry

        lax.fori_loop(0, nkb, value_body, 0)
        pieces = []
        for n in range(B_KV_HEADS):
            acc = oacc_ref[n]
            o_n = acc[:, n * B_HEAD_DIM:(n + 1) * B_HEAD_DIM] / acc[:, 128:129]
            pieces += [o_n[g * th:(g + 1) * th] for g in range(B_GROUP)]
        o_ref[r0:r0 + th, :] = jnp.concatenate(pieces, axis=1)

    for r0 in range(0, tq, th):
        attend_part(r0)


def _dsa(proj, bsz, t_len, k_all, v_all, ki_all, key_cols, pos0):
    kb_w = KEY_BLOCK
    tq = next(c for c in (4 * CHUNK, 2 * CHUNK, CHUNK) if t_len % c == 0 and pos0 % c == 0)
    th = min(tq, 2 * CHUNK)
    nq = t_len // tq
    s_pad = k_all.shape[1]
    assert s_pad % kb_w == 0 and pos0 + t_len <= s_pad and pos0 % tq == 0 and kb_w % tq == 0
    rows = B_GROUP * th
    kpos = lax.broadcasted_iota(I32, (s_pad, 128), 0)
    feat = lax.broadcasted_iota(I32, (s_pad, 128), 1)
    feat = feat % B_HEAD_DIM
    posf = jnp.where(feat == 0, kpos // CHUNK, jnp.where(feat == 1, kpos % CHUNK,
                     jnp.where(feat < 4, 1, 0))).astype(BF16)
    return pl.pallas_call(
        functools.partial(_dsa_kernel, pos0=pos0, kb_w=kb_w, tq=tq, th=th,
                          topk=min(MAX_TOPK, (pos0 + t_len) // 4)),
        grid=(bsz, nq),
        in_specs=[
            pl.BlockSpec((tq, 512), lambda b, t: (b * nq + t, C_Q // 512)),
            pl.BlockSpec((tq, 256), lambda b, t: (b * nq + t, C_QI // 256)),
            pl.BlockSpec((tq, 128), lambda b, t: (b * nq + t, C_KIWI // 128)),
            pl.BlockSpec((1, s_pad, 128), lambda b, t: (b, 0, key_cols[0])),
            pl.BlockSpec((1, s_pad, 128), lambda b, t: (b, 0, key_cols[1])),
            pl.BlockSpec((1, s_pad, 128), lambda b, t: (b, 0, key_cols[2])),
            pl.BlockSpec((s_pad, 128), lambda b, t: (0, 0)),
        ],
        out_specs=pl.BlockSpec((tq, B_WIDTH), lambda b, t: (b * nq + t, 0)),
        out_shape=jax.ShapeDtypeStruct((bsz * t_len, B_WIDTH), F32),
        scratch_shapes=[
            pltpu.VMEM((tq, s_pad), F32),
            pltpu.VMEM((B_KV_HEADS, rows, s_pad), F32),
            pltpu.VMEM((B_KV_HEADS, rows, 128), F32),
            pltpu.VMEM((B_KV_HEADS, rows, 256), F32),
        ],
        compiler_params=pltpu.CompilerParams(
            dimension_semantics=("arbitrary", "arbitrary"), vmem_limit_bytes=VMEM_LIMIT),
        name="dsa",
    )(proj, proj, proj, k_all, v_all, ki_all, posf)


def _merge_kernel(x_ref, ya_ref, yb_ref, gd_ref, gab_ref, wpa_ref, wpb_ref, wo_ref, fnw_ref, o_ref):
    gd = gd_ref[...]
    yb = yb_ref[...] * (gd * _sigmoid(gd))
    pa = jnp.dot(ya_ref[...].astype(BF16), wpa_ref[...], preferred_element_type=F32)
    pb = jnp.dot(yb.astype(BF16), wpb_ref[...], preferred_element_type=F32)
    merged = _sigmoid(gab_ref[:, 0:D_MODEL]) * pa + _sigmoid(gab_ref[:, D_MODEL:2 * D_MODEL]) * pb
    out = x_ref[...] + jnp.dot(merged.astype(BF16), wo_ref[...], preferred_element_type=F32)
    ms = jnp.mean(out * out, axis=-1, keepdims=True)
    o_ref[...] = (out * lax.rsqrt(ms + NORM_EPS)) * fnw_ref[...]


def _merge(x2d, ya, yb, proj, w_pa, w_pb, w_o, final_w):
    m = x2d.shape[0]
    tm = min(512, m)
    full = lambda shape: pl.BlockSpec(shape, lambda i: (0, 0))
    return pl.pallas_call(
        _merge_kernel,
        grid=(m // tm,),
        in_specs=[
            pl.BlockSpec((tm, D_MODEL), lambda i: (i, 0)),
            pl.BlockSpec((tm, A_WIDTH), lambda i: (i, 0)),
            pl.BlockSpec((tm, B_WIDTH), lambda i: (i, 0)),
            pl.BlockSpec((tm, 512), lambda i: (i, C_GD // 512)),
            pl.BlockSpec((tm, 2048), lambda i: (i, C_GATES // 2048)),
            full((A_WIDTH, D_MODEL)), full((B_WIDTH, D_MODEL)), full((D_MODEL, D_MODEL)),
            full((1, D_MODEL)),
        ],
        out_specs=pl.BlockSpec((tm, D_MODEL), lambda i: (i, 0)),
        out_shape=jax.ShapeDtypeStruct((m, D_MODEL), F32),
        compiler_params=pltpu.CompilerParams(
            dimension_semantics=("arbitrary",), vmem_limit_bytes=VMEM_LIMIT),
        name="merge",
    )(x2d, ya, yb, proj, proj, w_pa.astype(BF16), w_pb.astype(BF16), w_o.astype(BF16),
      final_w.reshape(1, D_MODEL))


def _rwkv_order(row):
    return row[..., 0:2048], row[..., 2048:2176]


def _mixer(x, shift_prev, wkv_prev, past_k, past_v, past_ki, w_perm, norm_w, prm, w_pa, w_pb, w_o,
           final_w):
    bsz, t_len, _ = x.shape
    x2d = x.reshape(bsz * t_len, D_MODEL)
    proj = _proj(x2d, norm_w, w_perm)
    proj3 = proj.reshape(bsz, t_len, N_PAD)

    shift4, shiftw = _rwkv_order(shift_prev)
    ya, wkv_new = _rwkv(proj, bsz, t_len, shift4, shiftw, wkv_prev, prm)

    k_new = proj3[:, :, C_KD:C_KD + 128]
    v_new = proj3[:, :, C_VD:C_VD + 128]
    ki_new = proj3[:, :, C_KIWI:C_KIWI + IDX_DIM]
    past_len = 0 if past_k is None else past_k.shape[1]
    if past_len == 0:
        yb = _dsa(proj, bsz, t_len, proj3, proj3, proj3, (C_KD // 128, C_VD // 128, C_KIWI // 128), 0)
    else:
        s_tot = past_len + t_len
        s_pad = -(-s_tot // KEY_BLOCK) * KEY_BLOCK
        pad = lambda a: jnp.pad(a.astype(BF16), ((0, 0), (0, s_pad - s_tot), (0, 0)))
        k_all = pad(jnp.concatenate([past_k.reshape(bsz, past_len, 128), k_new], axis=1))
        v_all = pad(jnp.concatenate([past_v.reshape(bsz, past_len, 128), v_new], axis=1))
        ki_all = pad(jnp.concatenate(
            [jnp.pad(past_ki, ((0, 0), (0, 0), (0, 128 - IDX_DIM))), proj3[:, :, C_KIWI:C_KIWI + 128]],
            axis=1))
        yb = _dsa(proj, bsz, t_len, k_all, v_all, ki_all, (0, 0, 0), past_len)

    y = _merge(x2d, ya, yb, proj, w_pa, w_pb, w_o, final_w).reshape(bsz, t_len, D_MODEL)
    last = proj3[:, t_len - 1:t_len, :]
    shift_new = jnp.concatenate([last[..., C_RKVG:C_RKVG + 2048], last[..., C_WDAD:C_WDAD + 128]], axis=-1)
    kv_shape = (bsz, t_len, B_KV_HEADS, B_HEAD_DIM)
    return y, k_new.reshape(kv_shape), v_new.reshape(kv_shape), ki_new, wkv_new, shift_new


def kernel(x_prompt, x_sample, cache_k, cache_v, cache_kidx, state_wkv, state_shift, norm_w, w_in,
           shift_mu, decay_w0, decay_up, iclr_a0, iclr_up, k_k, k_a, r_k, gn_w, gn_b, w_pa, w_pb,
           w_o, final_norm_w):
    assert w_in.shape[0] == 1, "the final norm is fused into the (single) layer's merge kernel"
    bp = x_prompt.shape[0]
    w_perm = _permute_w_in(w_in[0])
    mu4, muw = _rwkv_order(shift_mu[0].reshape(1, RWKV_COLS))
    row = lambda a: a.reshape(1, A_WIDTH)
    prm = dict(mu4=mu4, muw=muw, w0=row(decay_w0[0]), dup=decay_up[0], a0=row(iclr_a0[0]),
               aup=iclr_up[0], kk=row(k_k[0]), ka=row(k_a[0]), rk=row(r_k[0]), gnw=row(gn_w[0]),
               gnb=row(gn_b[0]))
    common = (w_perm, norm_w[0], prm, w_pa[0], w_pb[0], w_o[0], final_norm_w)
    yp, kp, vp, kip, wkvp, shp = _mixer(
        x_prompt, jnp.zeros((bp, 1, RWKV_COLS), F32),
        jnp.zeros((bp, A_HEADS, A_HEAD_DIM, A_HEAD_DIM), F32), None, None, None, *common)
    ys, ks, vs, kis, wkvs, shs = _mixer(
        x_sample, state_shift[0], state_wkv[0], cache_k[0], cache_v[0], cache_kidx[0], *common)
    st = lambda a: a[None]
    return (yp, ys, st(kp), st(vp), st(kip), st(wkvp), st(shp),
            st(ks), st(vs), st(kis), st(wkvs), st(shs))
```

```python
import functools

import jax
import jax.numpy as jnp
from jax import lax
from jax.experimental import pallas as pl
from jax.experimental.pallas import tpu as pltpu

F32 = jnp.float32
BF16 = jnp.bfloat16
I32 = jnp.int32
HIGHEST = lax.Precision.HIGHEST

D_MODEL = 1024
CHUNK = 64
A_HEADS = 8
A_HEAD_DIM = 64
A_WIDTH = 512
LORA = 64
RWKV_COLS = 4 * A_WIDTH + 2 * LORA
B_HEADS = 8
B_KV_HEADS = 2
B_GROUP = 4
B_HEAD_DIM = 64
B_WIDTH = 512
B_KV_WIDTH = 128
IDX_HEADS = 8
IDX_DIM = 32
MAX_TOPK = 256
DSA_COLS = 1576
N_IN = 5800
NORM_EPS = 1e-6
GN_EPS = 64e-5

C_GATES = 0
C_RKVG = 2048
C_Q = 4096
C_GD = 4608
C_QI = 5120
C_KD = 5376
C_VD = 5504
C_KIWI = 5632
C_WDAD = 5760
N_PAD = 5888

KEY_BLOCK = 512
INT_MIN = -(2 ** 31)
LOWEST_FINITE_KEY = INT_MIN + 0x00800000
NEG_BIG = -1e30
VMEM_LIMIT = 56 * 1024 * 1024


def _sigmoid(x):
    return 1.0 / (1.0 + jnp.exp(-x))


def _dot_nt(a, b, **kw):
    return lax.dot_general(a, b, (((1,), (1,)), ((), ())), **kw)


def _dot_tn(a, b, **kw):
    return lax.dot_general(a, b, (((0,), (0,)), ((), ())), **kw)


def _permute_w_in(w):
    d0 = RWKV_COLS
    g0 = RWKV_COLS + DSA_COLS
    cols = [
        w[:, g0:g0 + 2048],
        w[:, 0:2048],
        w[:, d0:d0 + 512],
        w[:, d0 + 1064:d0 + 1576],
        w[:, d0 + 768:d0 + 1024],
        w[:, d0 + 512:d0 + 640],
        w[:, d0 + 640:d0 + 768],
        w[:, d0 + 1024:d0 + 1064],
        jnp.zeros((w.shape[0], 128 - IDX_DIM - IDX_HEADS), w.dtype),
        w[:, 2048:2176],
    ]
    return jnp.concatenate(cols, axis=1).astype(BF16)


def _proj_kernel(x_ref, nw_ref, w_ref, o_ref):
    x = x_ref[...]
    ms = jnp.mean(x * x, axis=-1, keepdims=True)
    h = (x * lax.rsqrt(ms + NORM_EPS)) * nw_ref[...]
    o_ref[...] = jnp.dot(h.astype(BF16), w_ref[...], preferred_element_type=F32)


def _proj(x2d, norm_w, w_perm):
    m = x2d.shape[0]
    tm = min(512, m)
    tn = N_PAD // 2
    return pl.pallas_call(
        _proj_kernel,
        grid=(N_PAD // tn, m // tm),
        in_specs=[
            pl.BlockSpec((tm, D_MODEL), lambda j, i: (i, 0)),
            pl.BlockSpec((1, D_MODEL), lambda j, i: (0, 0)),
            pl.BlockSpec((D_MODEL, tn), lambda j, i: (0, j)),
        ],
        out_specs=pl.BlockSpec((tm, tn), lambda j, i: (i, j)),
        out_shape=jax.ShapeDtypeStruct((m, N_PAD), F32),
        compiler_params=pltpu.CompilerParams(
            dimension_semantics=("arbitrary", "arbitrary"), vmem_limit_bytes=VMEM_LIMIT),
        name="proj",
    )(x2d, norm_w.reshape(1, D_MODEL), w_perm)


def _split_bf16(x, terms):
    pieces = []
    for _ in range(terms):
        piece = x.astype(BF16)
        pieces.append(piece)
        x = x - piece.astype(F32)
    return pieces


def _dot_exact_rhs(a, b_bf16, terms):
    acc = None
    for piece in _split_bf16(a, terms):
        d = jnp.dot(piece, b_bf16, preferred_element_type=F32)
        acc = d if acc is None else acc + d
    return acc


def _dot_3pass(a, b):
    ah, al = _split_bf16(a, 2)
    bh, bl = _split_bf16(b, 2)
    dot = functools.partial(jnp.dot, preferred_element_type=F32)
    return dot(ah, bh) + dot(ah, bl) + dot(al, bh)


def _rwkv_kernel(p4_ref, pw_ref, sp4_ref, spw_ref, s0_ref, mu4_ref, muw_ref, w0_ref, dup_ref,
                 a0_ref, aup_ref, kk_ref, ka_ref, rk_ref, gnw_ref, gnb_ref, bd_ref, tri_ref,
                 ya_ref, sout_ref, c4_ref, cw_ref, sbd_ref, *, nch):
    c = CHUNK
    n = A_HEAD_DIM
    rows = nch * c
    n_pairs = A_HEADS // 2
    t_idx = pl.program_id(1)
    lane = lax.broadcasted_iota(I32, (c, 2 * n), 1)
    lo_half = lane < n
    row_c = lax.broadcasted_iota(I32, (c, 2 * n), 0)
    pos_in_head = jnp.where(lo_half, lane, lane - n)
    tri_strict = (pos_in_head < row_c).astype(F32)
    tri_incl = (pos_in_head <= row_c).astype(F32)
    eye2 = (pos_in_head == row_c).astype(F32)
    lane_sq = lax.broadcasted_iota(I32, (2 * n, 2 * n), 1)
    row_sq = lax.broadcasted_iota(I32, (2 * n, 2 * n), 0)
    same_head = (lane_sq < n) == (row_sq < n)
    dot = functools.partial(jnp.dot, preferred_element_type=F32)

    def bdiag(x):
        zero = jnp.zeros_like(x)
        return jnp.concatenate([jnp.where(lo_half, x, zero), jnp.where(lo_half, zero, x)], axis=0)

    @pl.when(t_idx == 0)
    def _():
        c4_ref[...] = sp4_ref[0]
        cw_ref[...] = spw_ref[0]
        zeros = jnp.zeros((n, n), F32)
        for j in range(n_pairs):
            sbd_ref[j] = jnp.concatenate(
                [jnp.concatenate([s0_ref[0, 2 * j], zeros], axis=1),
                 jnp.concatenate([zeros, s0_ref[0, 2 * j + 1]], axis=1)], axis=0)

    p4 = p4_ref[...]
    pw = pw_ref[...]
    row = lax.broadcasted_iota(I32, (rows, 1), 0)
    prev4 = jnp.where(row == 0, c4_ref[...], pltpu.roll(p4, 1, 0))
    prevw = jnp.where(row == 0, cw_ref[...], pltpu.roll(pw, 1, 0))
    c4_ref[...] = p4[rows - 1:rows, :]
    cw_ref[...] = pw[rows - 1:rows, :]
    ps4 = p4 + (prev4 - p4) * mu4_ref[...]
    psw = pw + (prevw - pw) * muw_ref[...]
    r = ps4[:, 0:512]
    k = ps4[:, 512:1024]
    v = ps4[:, 1024:1536]
    g = ps4[:, 1536:2048]
    wd = psw[:, 0:LORA]
    ad = psw[:, LORA:2 * LORA]
    bd = bd_ref[...]

    xw = w0_ref[...] + _dot_3pass(jnp.tanh(wd), dup_ref[...])
    z = -xw
    softplus = jnp.maximum(z, 0.0) + jnp.log(1.0 + jnp.exp(-jnp.abs(z)))
    lw = -jnp.exp(-softplus - 0.5)
    a = _sigmoid(a0_ref[...] + _dot_3pass(ad, aup_ref[...]))
    kkr = k * kk_ref[...]
    kkn = kkr / jnp.maximum(jnp.sqrt(_dot_exact_rhs(kkr * kkr, bd, 2)), 1e-12)
    kmod = k * (1.0 + (a - 1.0) * ka_ref[...])

    cum = None
    for piece in _split_bf16(lw, 3):
        d = dot(tri_ref[...], piece)
        cum = d if cum is None else cum + d
    pdec = jnp.exp(cum)
    pinv = jnp.exp(-cum)
    rt = (r * pdec).astype(BF16)
    at = (-kkn * jnp.exp(cum - lw)).astype(BF16)
    bt = (kkn * a * pinv).astype(BF16)
    kt = (kmod * pinv).astype(BF16)
    vb = v.astype(BF16)

    tiles = [(ci, j) for ci in range(nch) for j in range(n_pairs)]
    rsl = lambda ci: slice(ci * c, (ci + 1) * c)
    lsl = lambda j: slice(j * 2 * n, (j + 1) * 2 * n)
    lhs, bk, a_ak_rk, a_rb, xs, tinv = {}, {}, {}, {}, {}, {}
    for ci, j in tiles:
        rs, ls = rsl(ci), lsl(j)
        lhs[ci, j] = jnp.concatenate([at[rs, ls], rt[rs, ls]], axis=0)
        bk[ci, j] = jnp.concatenate([bt[rs, ls], kt[rs, ls]], axis=0)
        amat = _dot_nt(lhs[ci, j], jnp.concatenate([bdiag(bt[rs, ls]), bdiag(kt[rs, ls])], axis=0),
                       preferred_element_type=F32)
        xs[ci, j] = amat[0:c, 0:2 * n] * tri_strict
        a_ak_rk[ci, j] = jnp.concatenate([amat[0:c, 2 * n:4 * n] * tri_strict,
                                          amat[c:2 * c, 2 * n:4 * n] * tri_incl], axis=0).astype(BF16)
        a_rb[ci, j] = (amat[c:2 * c, 0:2 * n] * tri_incl).astype(BF16)
        tinv[ci, j] = eye2 + xs[ci, j]
    for t in tiles:
        xb = xs[t].astype(BF16)
        xs[t] = dot(xb, bdiag(xb))
    for _ in range(4):
        for t in tiles:
            xb = xs[t].astype(BF16)
            both = dot(xb, jnp.concatenate([bdiag(tinv[t].astype(BF16)), bdiag(xb)], axis=1))
            tinv[t] = tinv[t] + both[:, 0:2 * n]
            xs[t] = both[:, 2 * n:4 * n]
    for t in tiles:
        tinv[t] = tinv[t] + dot(xs[t].astype(BF16), bdiag(tinv[t].astype(BF16)))
    akv = {}
    for ci, j in tiles:
        tinv[ci, j] = tinv[ci, j].astype(BF16)
        akv[ci, j] = dot(a_ak_rk[ci, j], bdiag(vb[rsl(ci), lsl(j)]))

    pairs = range(n_pairs)
    s_pair = [sbd_ref[j] for j in pairs]
    y_chunks = []
    for ci in range(nch):
        from_state = [_dot_nt(lhs[ci, j], s_pair[j].astype(BF16), preferred_element_type=F32)
                      for j in pairs]
        u = [dot(tinv[ci, j], bdiag((from_state[j][0:c] + akv[ci, j][0:c]).astype(BF16)))
             for j in pairs]
        uv_t = [jnp.transpose(jnp.concatenate([u[j], v[rsl(ci), lsl(j)]], axis=0)).astype(BF16)
                for j in pairs]
        upd = [dot(uv_t[j], bk[ci, j]) for j in pairs]
        s_pair = [(s_pair[j] + jnp.where(same_head, upd[j], 0.0))
                  * pdec[(ci + 1) * c - 1:(ci + 1) * c, lsl(j)] for j in pairs]
        y_chunks.append(jnp.concatenate(
            [from_state[j][c:2 * c] + akv[ci, j][c:2 * c] + dot(a_rb[ci, j], bdiag(u[j].astype(BF16)))
             for j in pairs], axis=1))
    for j in pairs:
        sbd_ref[j] = s_pair[j]
    y = y_chunks[0] if nch == 1 else jnp.concatenate(y_chunks, axis=0)

    inv_n = 1.0 / n
    mean = _dot_exact_rhs(y, bd, 2) * inv_n
    dlt = y - mean
    var = _dot_exact_rhs(dlt * dlt, bd, 2) * inv_n
    yn = dlt * lax.rsqrt(var + GN_EPS) * gnw_ref[...] + gnb_ref[...]
    yn = yn + _dot_exact_rhs(r * kmod * rk_ref[...], bd, 2) * v
    ya_ref[...] = yn * (g * _sigmoid(g))

    @pl.when(t_idx == pl.num_programs(1) - 1)
    def _():
        for j in range(n_pairs):
            s_pair = sbd_ref[j]
            sout_ref[0, 2 * j] = s_pair[0:n, 0:n]
            sout_ref[0, 2 * j + 1] = s_pair[n:2 * n, n:2 * n]


def _rwkv(proj, bsz, t_len, shift4, shiftw, wkv_prev, prm):
    nch = min(4, t_len // CHUNK)
    rows = nch * CHUNK
    nt = t_len // rows
    row1 = lambda width: pl.BlockSpec((1, width), lambda b, t: (0, 0))
    head = lax.broadcasted_iota(I32, (A_WIDTH, A_WIDTH), 0) // A_HEAD_DIM
    bd = (head == head.T).astype(BF16)
    ti = lax.broadcasted_iota(I32, (rows, rows), 0)
    si = lax.broadcasted_iota(I32, (rows, rows), 1)
    tri = ((ti // CHUNK == si // CHUNK) & (si <= ti)).astype(BF16)
    return pl.pallas_call(
        functools.partial(_rwkv_kernel, nch=nch),
        grid=(bsz, nt),
        in_specs=[
            pl.BlockSpec((rows, 2048), lambda b, t: (b * nt + t, C_RKVG // 2048)),
            pl.BlockSpec((rows, 128), lambda b, t: (b * nt + t, C_WDAD // 128)),
            pl.BlockSpec((1, 1, 2048), lambda b, t: (b, 0, 0)),
            pl.BlockSpec((1, 1, 128), lambda b, t: (b, 0, 0)),
            pl.BlockSpec((1, A_HEADS, A_HEAD_DIM, A_HEAD_DIM), lambda b, t: (b, 0, 0, 0)),
            row1(2048), row1(128), row1(A_WIDTH),
            pl.BlockSpec((LORA, A_WIDTH), lambda b, t: (0, 0)),
            row1(A_WIDTH),
            pl.BlockSpec((LORA, A_WIDTH), lambda b, t: (0, 0)),
            row1(A_WIDTH), row1(A_WIDTH), row1(A_WIDTH), row1(A_WIDTH), row1(A_WIDTH),
            pl.BlockSpec((A_WIDTH, A_WIDTH), lambda b, t: (0, 0)),
            pl.BlockSpec((rows, rows), lambda b, t: (0, 0)),
        ],
        out_specs=[
            pl.BlockSpec((rows, A_WIDTH), lambda b, t: (b * nt + t, 0)),
            pl.BlockSpec((1, A_HEADS, A_HEAD_DIM, A_HEAD_DIM), lambda b, t: (b, 0, 0, 0)),
        ],
        out_shape=[
            jax.ShapeDtypeStruct((bsz * t_len, A_WIDTH), F32),
            jax.ShapeDtypeStruct((bsz, A_HEADS, A_HEAD_DIM, A_HEAD_DIM), F32),
        ],
        scratch_shapes=[pltpu.VMEM((1, 2048), F32), pltpu.VMEM((1, 128), F32),
                        pltpu.VMEM((A_HEADS // 2, 2 * A_HEAD_DIM, 2 * A_HEAD_DIM), F32)],
        compiler_params=pltpu.CompilerParams(
            dimension_semantics=("arbitrary", "arbitrary"), vmem_limit_bytes=VMEM_LIMIT),
        name="rwkv",
    )(proj, proj, shift4, shiftw, wkv_prev, prm["mu4"], prm["muw"], prm["w0"], prm["dup"],
      prm["a0"], prm["aup"], prm["kk"], prm["ka"], prm["rk"], prm["gnw"], prm["gnb"], bd, tri)


def _dsa_kernel(q_ref, qi_ref, kiwi_ref, k_ref, v_ref, ki_ref, *rest, pos0, kb_w, topk, tq, th, n_cache):
    if n_cache:
        kc_ref, vc_ref, kic_ref, posf_ref, o_ref, keys_ref, s_ref, macc_ref, oacc_ref = rest
    else:
        posf_ref, o_ref, keys_ref, s_ref, macc_ref, oacc_ref = rest
    qt = pl.program_id(1)
    tile_pos = pos0 + qt * tq
    row_chunk = jnp.right_shift(lax.broadcasted_iota(I32, (tq, 1), 0), 6)
    n_adm = tile_pos + (row_chunk + 1) * CHUNK
    nkb = n_cache + 1 if n_cache else (tile_pos + tq + kb_w - 1) // kb_w
    idx_scale = (IDX_HEADS ** -0.5) * (IDX_DIM ** -0.5)
    att_scale = B_HEAD_DIM ** -0.5
    topk = float(topk)
    lane_q128 = lax.broadcasted_iota(I32, (1, 128), 1)
    lane_k = lax.broadcasted_iota(I32, (1, kb_w), 1)

    def loaders(all_ref, cache_ref):
        if n_cache:
            def new_rows():
                new = all_ref[0].astype(BF16)
                return jnp.concatenate([new, jnp.zeros((kb_w - tq, new.shape[1]), BF16)], axis=0)
            return (lambda off: cache_ref[0, pl.ds(off, kb_w), :].astype(BF16)), new_rows
        at = lambda off: all_ref[0, pl.ds(off, kb_w), :].astype(BF16)
        return at, (lambda: at(last_off))

    last_off = n_cache * kb_w if n_cache else pl.multiple_of((nkb - 1) * kb_w, kb_w)

    def for_blocks(body, block_at, last_block):
        def step(kb, carry):
            off = pl.multiple_of(kb * kb_w, kb_w)
            body(off, block_at(off), False)
            return carry

        lax.fori_loop(0, nkb - 1, step, 0)
        body(last_off, last_block(), True)

    k_blocks = loaders(k_ref, kc_ref if n_cache else None)
    v_blocks = loaders(v_ref, vc_ref if n_cache else None)
    ki_blocks = loaders(ki_ref, kic_ref if n_cache else None)

    def index_part(r0):
        qi = qi_ref[r0:r0 + th, :]
        wi = kiwi_ref[r0:r0 + th, IDX_DIM:IDX_DIM + IDX_HEADS]
        heads = []
        for h in range(IDX_HEADS):
            slab = qi[:, 128 * (h // 4):128 * (h // 4) + 128]
            if h % 4:
                slab = pltpu.roll(slab, 128 - IDX_DIM * (h % 4), 1)
            heads.append(jnp.where(lane_q128 < IDX_DIM, slab, 0.0))
        qis = jnp.concatenate(heads, axis=0).astype(BF16)
        wis = jnp.concatenate([wi[:, h:h + 1] for h in range(IDX_HEADS)], axis=0)
        n_adm_part = n_adm[r0:r0 + th]

        def score_block(off, kir, _):
            s = _dot_nt(qis[:, 0:kir.shape[1]], kir, preferred_element_type=F32)
            s = jnp.maximum(s, 0.0) * wis
            isc = s[0:th]
            for h in range(1, IDX_HEADS):
                isc = isc + s[h * th:(h + 1) * th]
            isc = isc * idx_scale
            keys_ref[r0:r0 + th, pl.ds(off, kb_w)] = jnp.where(off + lane_k < n_adm_part, isc, -jnp.inf)

        for_blocks(score_block, *ki_blocks)

    for r0 in range(0, tq, th):
        index_part(r0)

    def key_to_score(key):
        return pltpu.bitcast(jnp.where(key < 0, key ^ 0x7FFFFFFF, key), F32)

    def count_ge(cand_key):
        accs = []
        for r0 in range(0, tq, th):
            cand_part = key_to_score(cand_key[r0:r0 + th])

            def body(kb, acc, r0=r0, cand_part=cand_part):
                off = pl.multiple_of(kb * kb_w, kb_w)
                kblk = keys_ref[r0:r0 + th, pl.ds(off, kb_w)]
                for j in range(kb_w // 128):
                    acc = acc + jnp.where(kblk[:, j * 128:(j + 1) * 128] >= cand_part, 1.0, 0.0)
                return acc

            accs.append(lax.fori_loop(0, nkb, body, jnp.zeros((th, 128), F32)))
        acc = accs[0] if len(accs) == 1 else jnp.concatenate(accs, axis=0)
        return jnp.sum(acc, axis=1, keepdims=True)

    c0 = count_ge(jnp.zeros((tq, 128), I32))
    t0 = jnp.where(c0 >= topk, jnp.zeros((tq, 128), I32), jnp.full((tq, 128), INT_MIN, I32))
    n0 = jnp.where(c0 >= topk, c0, n_adm.astype(F32))

    def bit_step(i, carry):
        t, n_t = carry
        cand = t | jnp.left_shift(jnp.int32(1), 30 - i)
        n_cand = count_ge(cand)
        take = n_cand >= topk
        return jnp.where(take, cand, t), jnp.where(take, n_cand, n_t)

    thr_key, n_ge = lax.fori_loop(0, 31, bit_step, (t0, n0))
    thr_key = jnp.maximum(thr_key, LOWEST_FINITE_KEY)
    thr = key_to_score(thr_key)[:, 0:1]

    @pl.when(jnp.max(n_ge) > topk)
    def _():
        n_tie_take = topk - count_ge(thr_key + 1)
        ri = lax.broadcasted_iota(I32, (kb_w, kb_w), 0)
        ci = lax.broadcasted_iota(I32, (kb_w, kb_w), 1)
        upper = (ri <= ci).astype(BF16)

        def body(kb, seen):
            off = pl.multiple_of(kb * kb_w, kb_w)
            kblk = keys_ref[:, pl.ds(off, kb_w)]
            tie = kblk == thr
            rank = seen + jnp.dot(jnp.where(tie, 1.0, 0.0).astype(BF16), upper,
                                  preferred_element_type=F32)
            keys_ref[:, pl.ds(off, kb_w)] = jnp.where(tie & (rank > n_tie_take), -jnp.inf, kblk)
            return rank[:, kb_w - 1:kb_w]

        lax.fori_loop(0, nkb, body, jnp.zeros((tq, 1), F32))

    rows = B_GROUP * th
    lane = lax.broadcasted_iota(I32, (th, 128), 1)

    def fold_lanes(x, op):
        part = x[:, 0:128]
        for j in range(1, kb_w // 128):
            part = op(part, x[:, j * 128:(j + 1) * 128])
        return part

    def attend_part(r0):
        q = q_ref[r0:r0 + th, :]
        thr_part = thr[r0:r0 + th]
        q_chunk = (tile_pos // CHUNK + row_chunk[r0:r0 + th]).astype(F32)
        q_row = (lax.broadcasted_iota(I32, (th, 128), 0) & (CHUNK - 1)).astype(F32)
        qpos = tile_pos + r0 + lax.broadcasted_iota(I32, (th, 1), 0)
        qaug, slope2, own_half = [], [], []
        for n in range(B_KV_HEADS):
            q_parts, slope_parts = [], []
            keep = (lane >= n * B_HEAD_DIM) & (lane < (n + 1) * B_HEAD_DIM)
            own_half.append((lane_q128 >= n * B_HEAD_DIM) & (lane_q128 < (n + 1) * B_HEAD_DIM))
            pos_lane = lane - (1 - n) * B_HEAD_DIM
            for g in range(B_GROUP):
                h = n * B_GROUP + g
                slope = 2.0 ** (-(8.0 / B_HEADS) * (h + 1))
                slab = q[:, 128 * (h // 2):128 * (h // 2) + 128]
                if h % 2 != n:
                    slab = pltpu.roll(slab, B_HEAD_DIM, 1)
                pos_feat = jnp.where(pos_lane == 0, CHUNK * slope,
                           jnp.where(pos_lane == 1, slope,
                           jnp.where(pos_lane == 2, -CHUNK * slope * q_chunk,
                           jnp.where(pos_lane == 3, -slope * q_row, 0.0))))
                q_parts.append(jnp.where(keep, slab * att_scale, pos_feat))
                slope_parts.append(jnp.full((th, 1), 2.0 * slope, F32))
            qaug.append(jnp.concatenate(q_parts, axis=0).astype(BF16))
            slope2.append(jnp.concatenate(slope_parts, axis=0))

        macc_ref[...] = jnp.full(macc_ref.shape, NEG_BIG, F32)
        oacc_ref[...] = jnp.zeros(oacc_ref.shape, F32)

        def score_pass(off, kblk, own_chunk):
            sel = keys_ref[r0:r0 + th, pl.ds(off, kb_w)] >= thr_part
            sel4 = jnp.concatenate([sel] * B_GROUP, axis=0)
            pblk = posf_ref[pl.ds(off, kb_w), :]
            if own_chunk:
                ahead = jnp.maximum((off + lane_k) - qpos, 0).astype(F32)
                ahead4 = jnp.concatenate([ahead] * B_GROUP, axis=0)
            for n in range(B_KV_HEADS):
                kaug = jnp.where(own_half[n], kblk, pblk)
                s = _dot_nt(qaug[n], kaug, preferred_element_type=F32)
                if own_chunk:
                    s = s - slope2[n] * ahead4
                s = jnp.where(sel4, s, NEG_BIG)
                s_ref[n, :, pl.ds(off, kb_w)] = s
                macc_ref[n] = jnp.maximum(macc_ref[n], fold_lanes(s, jnp.maximum))

        for_blocks(score_pass, *k_blocks)

        for n in range(B_KV_HEADS):
            m = jnp.max(macc_ref[n], axis=1, keepdims=True)
            macc_ref[n] = jnp.broadcast_to(m, (rows, 128))

        ones_blk = jnp.ones((kb_w, 128), BF16)

        def value_pass(off, vblk, _):
            vaug = jnp.concatenate([vblk, ones_blk], axis=1)
            for n in range(B_KV_HEADS):
                m_b = macc_ref[n]
                p = jnp.exp(s_ref[n, :, pl.ds(off, kb_w)]
                            - jnp.concatenate([m_b] * (kb_w // 128), axis=1))
                oacc_ref[n] = oacc_ref[n] + jnp.dot(p.astype(BF16), vaug, preferred_element_type=F32)

        for_blocks(value_pass, *v_blocks)
        pieces = []
        for n in range(B_KV_HEADS):
            acc = oacc_ref[n]
            o_n = acc[:, n * B_HEAD_DIM:(n + 1) * B_HEAD_DIM] / acc[:, 128:129]
            pieces += [o_n[g * th:(g + 1) * th] for g in range(B_GROUP)]
        o_ref[r0:r0 + th, :] = jnp.concatenate(pieces, axis=1)

    for r0 in range(0, tq, th):
        attend_part(r0)


def _dsa(proj, bsz, t_len, cache=None):
    kb_w = KEY_BLOCK
    pos0 = 0 if cache is None else cache[0].shape[1]
    tq = next(c for c in (8 * CHUNK, 4 * CHUNK, 2 * CHUNK, CHUNK) if t_len % c == 0 and pos0 % c == 0)
    th = min(tq, 2 * CHUNK)
    nq = t_len // tq
    n_cache = pos0 // kb_w
    s_pad = pos0 + kb_w if cache is not None else t_len
    assert s_pad % kb_w == 0 and pos0 % kb_w == 0 and kb_w % tq == 0
    assert cache is None or t_len == tq
    rows = B_GROUP * th
    proj3 = proj.reshape(bsz, t_len, N_PAD)
    key_cols = (C_KD // 128, C_VD // 128, C_KIWI // 128)
    if cache is None:
        key_specs = [pl.BlockSpec((1, s_pad, 128), functools.partial(lambda b, t, c: (b, 0, c), c=c))
                     for c in key_cols]
        key_args = (proj3, proj3, proj3)
    else:
        key_specs = [pl.BlockSpec((1, tq, 128), functools.partial(lambda b, t, c: (b, t, c), c=c))
                     for c in key_cols]
        key_specs += [pl.BlockSpec((1, pos0, a.shape[2]), lambda b, t: (b, 0, 0)) for a in cache]
        key_args = (proj3, proj3, proj3) + tuple(cache)
    kpos = lax.broadcasted_iota(I32, (s_pad, 128), 0)
    feat = lax.broadcasted_iota(I32, (s_pad, 128), 1)
    feat = feat % B_HEAD_DIM
    posf = jnp.where(feat == 0, kpos // CHUNK, jnp.where(feat == 1, kpos % CHUNK,
                     jnp.where(feat < 4, 1, 0))).astype(BF16)
    return pl.pallas_call(
        functools.partial(_dsa_kernel, pos0=pos0, kb_w=kb_w, tq=tq, th=th, n_cache=n_cache,
                          topk=min(MAX_TOPK, (pos0 + t_len) // 4)),
        grid=(bsz, nq),
        in_specs=[
            pl.BlockSpec((tq, 512), lambda b, t: (b * nq + t, C_Q // 512)),
            pl.BlockSpec((tq, 256), lambda b, t: (b * nq + t, C_QI // 256)),
            pl.BlockSpec((tq, 128), lambda b, t: (b * nq + t, C_KIWI // 128)),
            *key_specs,
            pl.BlockSpec((s_pad, 128), lambda b, t: (0, 0)),
        ],
        out_specs=pl.BlockSpec((tq, B_WIDTH), lambda b, t: (b * nq + t, 0)),
        out_shape=jax.ShapeDtypeStruct((bsz * t_len, B_WIDTH), F32),
        scratch_shapes=[
            pltpu.VMEM((tq, s_pad), F32),
            pltpu.VMEM((B_KV_HEADS, rows, s_pad), F32),
            pltpu.VMEM((B_KV_HEADS, rows, 128), F32),
            pltpu.VMEM((B_KV_HEADS, rows, 256), F32),
        ],
        compiler_params=pltpu.CompilerParams(
            dimension_semantics=("arbitrary", "arbitrary"), vmem_limit_bytes=VMEM_LIMIT),
        name="dsa",
    )(proj, proj, proj, *key_args, posf)


def _merge_kernel(x_ref, ya_ref, yb_ref, gd_ref, gab_ref, wpa_ref, wpb_ref, wo_ref, fnw_ref, o_ref):
    gd = gd_ref[...]
    yb = yb_ref[...] * (gd * _sigmoid(gd))
    pa = jnp.dot(ya_ref[...].astype(BF16), wpa_ref[...], preferred_element_type=F32)
    pb = jnp.dot(yb.astype(BF16), wpb_ref[...], preferred_element_type=F32)
    merged = _sigmoid(gab_ref[:, 0:D_MODEL]) * pa + _sigmoid(gab_ref[:, D_MODEL:2 * D_MODEL]) * pb
    out = x_ref[...] + jnp.dot(merged.astype(BF16), wo_ref[...], preferred_element_type=F32)
    ms = jnp.mean(out * out, axis=-1, keepdims=True)
    o_ref[...] = (out * lax.rsqrt(ms + NORM_EPS)) * fnw_ref[...]


def _merge(x2d, ya, yb, proj, w_pa, w_pb, w_o, final_w):
    m = x2d.shape[0]
    tm = min(512, m)
    full = lambda shape: pl.BlockSpec(shape, lambda i: (0, 0))
    return pl.pallas_call(
        _merge_kernel,
        grid=(m // tm,),
        in_specs=[
            pl.BlockSpec((tm, D_MODEL), lambda i: (i, 0)),
            pl.BlockSpec((tm, A_WIDTH), lambda i: (i, 0)),
            pl.BlockSpec((tm, B_WIDTH), lambda i: (i, 0)),
            pl.BlockSpec((tm, 512), lambda i: (i, C_GD // 512)),
            pl.BlockSpec((tm, 2048), lambda i: (i, C_GATES // 2048)),
            full((A_WIDTH, D_MODEL)), full((B_WIDTH, D_MODEL)), full((D_MODEL, D_MODEL)),
            full((1, D_MODEL)),
        ],
        out_specs=pl.BlockSpec((tm, D_MODEL), lambda i: (i, 0)),
        out_shape=jax.ShapeDtypeStruct((m, D_MODEL), F32),
        compiler_params=pltpu.CompilerParams(
            dimension_semantics=("arbitrary",), vmem_limit_bytes=VMEM_LIMIT),
        name="merge",
    )(x2d, ya, yb, proj, proj, w_pa.astype(BF16), w_pb.astype(BF16), w_o.astype(BF16),
      final_w.reshape(1, D_MODEL))


def _rwkv_order(row):
    return row[..., 0:2048], row[..., 2048:2176]


def _mixer(x, shift_prev, wkv_prev, past_k, past_v, past_ki, w_perm, norm_w, prm, w_pa, w_pb, w_o,
           final_w):
    bsz, t_len, _ = x.shape
    x2d = x.reshape(bsz * t_len, D_MODEL)
    proj = _proj(x2d, norm_w, w_perm)
    proj3 = proj.reshape(bsz, t_len, N_PAD)

    shift4, shiftw = _rwkv_order(shift_prev)
    ya, wkv_new = _rwkv(proj, bsz, t_len, shift4, shiftw, wkv_prev, prm)

    k_new = proj3[:, :, C_KD:C_KD + 128]
    v_new = proj3[:, :, C_VD:C_VD + 128]
    ki_new = proj3[:, :, C_KIWI:C_KIWI + IDX_DIM]
    past_len = 0 if past_k is None else past_k.shape[1]
    if past_len == 0:
        yb = _dsa(proj, bsz, t_len)
    else:
        yb = _dsa(proj, bsz, t_len, cache=(past_k.reshape(bsz, past_len, 128),
                                           past_v.reshape(bsz, past_len, 128), past_ki))

    y = _merge(x2d, ya, yb, proj, w_pa, w_pb, w_o, final_w).reshape(bsz, t_len, D_MODEL)
    last = proj3[:, t_len - 1:t_len, :]
    shift_new = jnp.concatenate([last[..., C_RKVG:C_RKVG + 2048], last[..., C_WDAD:C_WDAD + 128]], axis=-1)
    kv_shape = (bsz, t_len, B_KV_HEADS, B_HEAD_DIM)
    return y, k_new.reshape(kv_shape), v_new.reshape(kv_shape), ki_new, wkv_new, shift_new


def kernel(x_prompt, x_sample, cache_k, cache_v, cache_kidx, state_wkv, state_shift, norm_w, w_in,
           shift_mu, decay_w0, decay_up, iclr_a0, iclr_up, k_k, k_a, r_k, gn_w, gn_b, w_pa, w_pb,
           w_o, final_norm_w):
    assert w_in.shape[0] == 1, "the final norm is fused into the (single) layer's merge kernel"
    bp = x_prompt.shape[0]
    w_perm = _permute_w_in(w_in[0])
    mu4, muw = _rwkv_order(shift_mu[0].reshape(1, RWKV_COLS))
    row = lambda a: a.reshape(1, A_WIDTH)
    prm = dict(mu4=mu4, muw=muw, w0=row(decay_w0[0]), dup=decay_up[0], a0=row(iclr_a0[0]),
               aup=iclr_up[0], kk=row(k_k[0]), ka=row(k_a[0]), rk=row(r_k[0]), gnw=row(gn_w[0]),
               gnb=row(gn_b[0]))
    common = (w_perm, norm_w[0], prm, w_pa[0], w_pb[0], w_o[0], final_norm_w)
    yp, kp, vp, kip, wkvp, shp = _mixer(
        x_prompt, jnp.zeros((bp, 1, RWKV_COLS), F32),
        jnp.zeros((bp, A_HEADS, A_HEAD_DIM, A_HEAD_DIM), F32), None, None, None, *common)
    ys, ks, vs, kis, wkvs, shs = _mixer(
        x_sample, state_shift[0], state_wkv[0], cache_k[0], cache_v[0], cache_kidx[0], *common)
    st = lambda a: a[None]
    return (yp, ys, st(kp), st(vp), st(kip), st(wkvp), st(shp),
            st(ks), st(vs), st(kis), st(wkvs), st(shs))
```

```python
import functools

import jax
import jax.numpy as jnp
from jax import lax
from jax.experimental import pallas as pl
from jax.experimental.pallas import tpu as pltpu

F32 = jnp.float32
BF16 = jnp.bfloat16
I32 = jnp.int32
HIGHEST = lax.Precision.HIGHEST

D_MODEL = 1024
CHUNK = 64
A_HEADS = 8
A_HEAD_DIM = 64
A_WIDTH = 512
LORA = 64
RWKV_COLS = 4 * A_WIDTH + 2 * LORA
B_HEADS = 8
B_KV_HEADS = 2
B_GROUP = 4
B_HEAD_DIM = 64
B_WIDTH = 512
B_KV_WIDTH = 128
IDX_HEADS = 8
IDX_DIM = 32
MAX_TOPK = 256
DSA_COLS = 1576
N_IN = 5800
NORM_EPS = 1e-6
GN_EPS = 64e-5

C_GATES = 0
C_RKVG = 2048
C_Q = 4096
C_GD = 4608
C_QI = 5120
C_KD = 5376
C_VD = 5504
C_KIWI = 5632
C_WDAD = 5760
N_PAD = 5888

KEY_BLOCK = 512
INT_MIN = -(2 ** 31)
LOWEST_FINITE_KEY = INT_MIN + 0x00800000
NEG_BIG = -1e30
VMEM_LIMIT = 56 * 1024 * 1024


def _sigmoid(x):
    return 1.0 / (1.0 + jnp.exp(-x))


def _dot_nt(a, b, **kw):
    return lax.dot_general(a, b, (((1,), (1,)), ((), ())), **kw)


def _dot_tn(a, b, **kw):
    return lax.dot_general(a, b, (((0,), (0,)), ((), ())), **kw)


def _permute_w_in(w):
    d0 = RWKV_COLS
    g0 = RWKV_COLS + DSA_COLS
    cols = [
        w[:, g0:g0 + 2048],
        w[:, 0:2048],
        w[:, d0:d0 + 512],
        w[:, d0 + 1064:d0 + 1576],
        w[:, d0 + 768:d0 + 1024],
        w[:, d0 + 512:d0 + 640],
        w[:, d0 + 640:d0 + 768],
        w[:, d0 + 1024:d0 + 1064],
        jnp.zeros((w.shape[0], 128 - IDX_DIM - IDX_HEADS), w.dtype),
        w[:, 2048:2176],
    ]
    return jnp.concatenate(cols, axis=1).astype(BF16)


def _proj_kernel(x_ref, nw_ref, w_ref, o_ref):
    x = x_ref[...]
    ms = jnp.mean(x * x, axis=-1, keepdims=True)
    h = (x * lax.rsqrt(ms + NORM_EPS)) * nw_ref[...]
    o_ref[...] = jnp.dot(h.astype(BF16), w_ref[...], preferred_element_type=F32)


def _proj(x2d, norm_w, w_perm):
    m = x2d.shape[0]
    tm = min(512, m)
    tn = N_PAD // 2
    return pl.pallas_call(
        _proj_kernel,
        grid=(N_PAD // tn, m // tm),
        in_specs=[
            pl.BlockSpec((tm, D_MODEL), lambda j, i: (i, 0)),
            pl.BlockSpec((1, D_MODEL), lambda j, i: (0, 0)),
            pl.BlockSpec((D_MODEL, tn), lambda j, i: (0, j)),
        ],
        out_specs=pl.BlockSpec((tm, tn), lambda j, i: (i, j)),
        out_shape=jax.ShapeDtypeStruct((m, N_PAD), F32),
        compiler_params=pltpu.CompilerParams(
            dimension_semantics=("arbitrary", "arbitrary"), vmem_limit_bytes=VMEM_LIMIT),
        name="proj",
    )(x2d, norm_w.reshape(1, D_MODEL), w_perm)


def _split_bf16(x, terms):
    pieces = []
    for _ in range(terms):
        piece = x.astype(BF16)
        pieces.append(piece)
        x = x - piece.astype(F32)
    return pieces


def _dot_exact_rhs(a, b_bf16, terms):
    acc = None
    for piece in _split_bf16(a, terms):
        d = jnp.dot(piece, b_bf16, preferred_element_type=F32)
        acc = d if acc is None else acc + d
    return acc


def _dot_3pass(a, b):
    ah, al = _split_bf16(a, 2)
    bh, bl = _split_bf16(b, 2)
    dot = functools.partial(jnp.dot, preferred_element_type=F32)
    return dot(ah, bh) + dot(ah, bl) + dot(al, bh)


def _rwkv_kernel(p4_ref, pw_ref, sp4_ref, spw_ref, s0_ref, mu4_ref, muw_ref, w0_ref, dup_ref,
                 a0_ref, aup_ref, kk_ref, ka_ref, rk_ref, gnw_ref, gnb_ref, bd_ref, tri_ref,
                 ya_ref, sout_ref, c4_ref, cw_ref, sbd_ref, *, nch):
    c = CHUNK
    n = A_HEAD_DIM
    rows = nch * c
    n_pairs = A_HEADS // 2
    t_idx = pl.program_id(1)
    lane = lax.broadcasted_iota(I32, (c, 2 * n), 1)
    lo_half = lane < n
    row_c = lax.broadcasted_iota(I32, (c, 2 * n), 0)
    pos_in_head = jnp.where(lo_half, lane, lane - n)
    tri_strict = (pos_in_head < row_c).astype(F32)
    tri_incl = (pos_in_head <= row_c).astype(F32)
    eye2 = (pos_in_head == row_c).astype(F32)
    lane_sq = lax.broadcasted_iota(I32, (2 * n, 2 * n), 1)
    row_sq = lax.broadcasted_iota(I32, (2 * n, 2 * n), 0)
    same_head = (lane_sq < n) == (row_sq < n)
    dot = functools.partial(jnp.dot, preferred_element_type=F32)

    def bdiag(x):
        zero = jnp.zeros_like(x)
        return jnp.concatenate([jnp.where(lo_half, x, zero), jnp.where(lo_half, zero, x)], axis=0)

    @pl.when(t_idx == 0)
    def _():
        c4_ref[...] = sp4_ref[0]
        cw_ref[...] = spw_ref[0]
        zeros = jnp.zeros((n, n), F32)
        for j in range(n_pairs):
            sbd_ref[j] = jnp.concatenate(
                [jnp.concatenate([s0_ref[0, 2 * j], zeros], axis=1),
                 jnp.concatenate([zeros, s0_ref[0, 2 * j + 1]], axis=1)], axis=0)

    p4 = p4_ref[...]
    pw = pw_ref[...]
    row = lax.broadcasted_iota(I32, (rows, 1), 0)
    prev4 = jnp.where(row == 0, c4_ref[...], pltpu.roll(p4, 1, 0))
    prevw = jnp.where(row == 0, cw_ref[...], pltpu.roll(pw, 1, 0))
    c4_ref[...] = p4[rows - 1:rows, :]
    cw_ref[...] = pw[rows - 1:rows, :]
    ps4 = p4 + (prev4 - p4) * mu4_ref[...]
    psw = pw + (prevw - pw) * muw_ref[...]
    r = ps4[:, 0:512]
    k = ps4[:, 512:1024]
    v = ps4[:, 1024:1536]
    g = ps4[:, 1536:2048]
    wd = psw[:, 0:LORA]
    ad = psw[:, LORA:2 * LORA]
    bd = bd_ref[...]

    xw = w0_ref[...] + _dot_3pass(jnp.tanh(wd), dup_ref[...])
    z = -xw
    softplus = jnp.maximum(z, 0.0) + jnp.log(1.0 + jnp.exp(-jnp.abs(z)))
    lw = -jnp.exp(-softplus - 0.5)
    a = _sigmoid(a0_ref[...] + _dot_3pass(ad, aup_ref[...]))
    kkr = k * kk_ref[...]
    kkn = kkr / jnp.maximum(jnp.sqrt(_dot_exact_rhs(kkr * kkr, bd, 2)), 1e-12)
    kmod = k * (1.0 + (a - 1.0) * ka_ref[...])

    cum = None
    for piece in _split_bf16(lw, 3):
        d = dot(tri_ref[...], piece)
        cum = d if cum is None else cum + d
    pdec = jnp.exp(cum)
    pinv = jnp.exp(-cum)
    rt = (r * pdec).astype(BF16)
    at = (-kkn * jnp.exp(cum - lw)).astype(BF16)
    bt = (kkn * a * pinv).astype(BF16)
    kt = (kmod * pinv).astype(BF16)
    vb = v.astype(BF16)

    tiles = [(ci, j) for ci in range(nch) for j in range(n_pairs)]
    rsl = lambda ci: slice(ci * c, (ci + 1) * c)
    lsl = lambda j: slice(j * 2 * n, (j + 1) * 2 * n)
    lhs, bk, a_ak_rk, a_rb, xs, tinv = {}, {}, {}, {}, {}, {}
    for ci, j in tiles:
        rs, ls = rsl(ci), lsl(j)
        lhs[ci, j] = jnp.concatenate([at[rs, ls], rt[rs, ls]], axis=0)
        bk[ci, j] = jnp.concatenate([bt[rs, ls], kt[rs, ls]], axis=0)
        amat = _dot_nt(lhs[ci, j], jnp.concatenate([bdiag(bt[rs, ls]), bdiag(kt[rs, ls])], axis=0),
                       preferred_element_type=F32)
        xs[ci, j] = amat[0:c, 0:2 * n] * tri_strict
        a_ak_rk[ci, j] = jnp.concatenate([amat[0:c, 2 * n:4 * n] * tri_strict,
                                          amat[c:2 * c, 2 * n:4 * n] * tri_incl], axis=0).astype(BF16)
        a_rb[ci, j] = (amat[c:2 * c, 0:2 * n] * tri_incl).astype(BF16)
        tinv[ci, j] = eye2 + xs[ci, j]
    for t in tiles:
        xb = xs[t].astype(BF16)
        xs[t] = dot(xb, bdiag(xb))
    for _ in range(4):
        for t in tiles:
            xb = xs[t].astype(BF16)
            both = dot(xb, jnp.concatenate([bdiag(tinv[t].astype(BF16)), bdiag(xb)], axis=1))
            tinv[t] = tinv[t] + both[:, 0:2 * n]
            xs[t] = both[:, 2 * n:4 * n]
    for t in tiles:
        tinv[t] = tinv[t] + dot(xs[t].astype(BF16), bdiag(tinv[t].astype(BF16)))
    akv = {}
    for ci, j in tiles:
        tinv[ci, j] = tinv[ci, j].astype(BF16)
        akv[ci, j] = dot(a_ak_rk[ci, j], bdiag(vb[rsl(ci), lsl(j)]))

    pairs = range(n_pairs)
    s_pair = [sbd_ref[j] for j in pairs]
    y_chunks = []
    for ci in range(nch):
        from_state = [_dot_nt(lhs[ci, j], s_pair[j].astype(BF16), preferred_element_type=F32)
                      for j in pairs]
        u = [dot(tinv[ci, j], bdiag((from_state[j][0:c] + akv[ci, j][0:c]).astype(BF16)))
             for j in pairs]
        uv_t = [jnp.transpose(jnp.concatenate([u[j], v[rsl(ci), lsl(j)]], axis=0)).astype(BF16)
                for j in pairs]
        upd = [dot(uv_t[j], bk[ci, j]) for j in pairs]
        s_pair = [(s_pair[j] + jnp.where(same_head, upd[j], 0.0))
                  * pdec[(ci + 1) * c - 1:(ci + 1) * c, lsl(j)] for j in pairs]
        y_chunks.append(jnp.concatenate(
            [from_state[j][c:2 * c] + akv[ci, j][c:2 * c] + dot(a_rb[ci, j], bdiag(u[j].astype(BF16)))
             for j in pairs], axis=1))
    for j in pairs:
        sbd_ref[j] = s_pair[j]
    y = y_chunks[0] if nch == 1 else jnp.concatenate(y_chunks, axis=0)

    inv_n = 1.0 / n
    mean = _dot_exact_rhs(y, bd, 2) * inv_n
    dlt = y - mean
    var = _dot_exact_rhs(dlt * dlt, bd, 2) * inv_n
    yn = dlt * lax.rsqrt(var + GN_EPS) * gnw_ref[...] + gnb_ref[...]
    yn = yn + _dot_exact_rhs(r * kmod * rk_ref[...], bd, 2) * v
    ya_ref[...] = yn * (g * _sigmoid(g))

    @pl.when(t_idx == pl.num_programs(1) - 1)
    def _():
        for j in range(n_pairs):
            s_pair = sbd_ref[j]
            sout_ref[0, 2 * j] = s_pair[0:n, 0:n]
            sout_ref[0, 2 * j + 1] = s_pair[n:2 * n, n:2 * n]


def _rwkv(proj, bsz, t_len, shift4, shiftw, wkv_prev, prm):
    nch = min(4, t_len // CHUNK)
    rows = nch * CHUNK
    nt = t_len // rows
    row1 = lambda width: pl.BlockSpec((1, width), lambda b, t: (0, 0))
    head = lax.broadcasted_iota(I32, (A_WIDTH, A_WIDTH), 0) // A_HEAD_DIM
    bd = (head == head.T).astype(BF16)
    ti = lax.broadcasted_iota(I32, (rows, rows), 0)
    si = lax.broadcasted_iota(I32, (rows, rows), 1)
    tri = ((ti // CHUNK == si // CHUNK) & (si <= ti)).astype(BF16)
    return pl.pallas_call(
        functools.partial(_rwkv_kernel, nch=nch),
        grid=(bsz, nt),
        in_specs=[
            pl.BlockSpec((rows, 2048), lambda b, t: (b * nt + t, C_RKVG // 2048)),
            pl.BlockSpec((rows, 128), lambda b, t: (b * nt + t, C_WDAD // 128)),
            pl.BlockSpec((1, 1, 2048), lambda b, t: (b, 0, 0)),
            pl.BlockSpec((1, 1, 128), lambda b, t: (b, 0, 0)),
            pl.BlockSpec((1, A_HEADS, A_HEAD_DIM, A_HEAD_DIM), lambda b, t: (b, 0, 0, 0)),
            row1(2048), row1(128), row1(A_WIDTH),
            pl.BlockSpec((LORA, A_WIDTH), lambda b, t: (0, 0)),
            row1(A_WIDTH),
            pl.BlockSpec((LORA, A_WIDTH), lambda b, t: (0, 0)),
            row1(A_WIDTH), row1(A_WIDTH), row1(A_WIDTH), row1(A_WIDTH), row1(A_WIDTH),
            pl.BlockSpec((A_WIDTH, A_WIDTH), lambda b, t: (0, 0)),
            pl.BlockSpec((rows, rows), lambda b, t: (0, 0)),
        ],
        out_specs=[
            pl.BlockSpec((rows, A_WIDTH), lambda b, t: (b * nt + t, 0)),
            pl.BlockSpec((1, A_HEADS, A_HEAD_DIM, A_HEAD_DIM), lambda b, t: (b, 0, 0, 0)),
        ],
        out_shape=[
            jax.ShapeDtypeStruct((bsz * t_len, A_WIDTH), F32),
            jax.ShapeDtypeStruct((bsz, A_HEADS, A_HEAD_DIM, A_HEAD_DIM), F32),
        ],
        scratch_shapes=[pltpu.VMEM((1, 2048), F32), pltpu.VMEM((1, 128), F32),
                        pltpu.VMEM((A_HEADS // 2, 2 * A_HEAD_DIM, 2 * A_HEAD_DIM), F32)],
        compiler_params=pltpu.CompilerParams(
            dimension_semantics=("arbitrary", "arbitrary"), vmem_limit_bytes=VMEM_LIMIT),
        name="rwkv",
    )(proj, proj, shift4, shiftw, wkv_prev, prm["mu4"], prm["muw"], prm["w0"], prm["dup"],
      prm["a0"], prm["aup"], prm["kk"], prm["ka"], prm["rk"], prm["gnw"], prm["gnb"], bd, tri)


def _dsa_kernel(q_ref, qi_ref, kiwi_ref, k_ref, v_ref, ki_ref, *rest, pos0, kb_w, topk, tq, th, n_cache):
    if n_cache:
        kc_ref, vc_ref, kic_ref, posf_ref, o_ref, keys_ref, s_ref, macc_ref, oacc_ref = rest
    else:
        posf_ref, o_ref, keys_ref, s_ref, macc_ref, oacc_ref = rest
    qt = pl.program_id(1)
    tile_pos = pos0 + qt * tq
    row_chunk = jnp.right_shift(lax.broadcasted_iota(I32, (tq, 1), 0), 6)
    n_adm = tile_pos + (row_chunk + 1) * CHUNK
    nkb = n_cache + 1 if n_cache else (tile_pos + tq + kb_w - 1) // kb_w
    idx_scale = (IDX_HEADS ** -0.5) * (IDX_DIM ** -0.5)
    att_scale = B_HEAD_DIM ** -0.5
    topk = float(topk)
    lane_q128 = lax.broadcasted_iota(I32, (1, 128), 1)
    lane_k = lax.broadcasted_iota(I32, (1, kb_w), 1)

    def loaders(all_ref, cache_ref):
        if n_cache:
            def new_rows():
                new = all_ref[0].astype(BF16)
                return jnp.concatenate([new, jnp.zeros((kb_w - tq, new.shape[1]), BF16)], axis=0)
            return (lambda off: cache_ref[0, pl.ds(off, kb_w), :].astype(BF16)), new_rows
        at = lambda off: all_ref[0, pl.ds(off, kb_w), :].astype(BF16)
        return at, (lambda: at(last_off))

    last_off = n_cache * kb_w if n_cache else pl.multiple_of((nkb - 1) * kb_w, kb_w)

    def for_blocks(body, block_at, last_block):
        def step(kb, carry):
            off = pl.multiple_of(kb * kb_w, kb_w)
            body(off, block_at(off), False)
            return carry

        def pair_step(j, carry):
            offs = [pl.multiple_of((2 * j + i) * kb_w, kb_w) for i in range(2)]
            blocks = [block_at(off) for off in offs]
            for off, block in zip(offs, blocks):
                body(off, block, False)
            return carry

        n_pairs = (nkb - 1) // 2
        lax.fori_loop(0, n_pairs, pair_step, 0)
        lax.fori_loop(2 * n_pairs, nkb - 1, step, 0)
        body(last_off, last_block(), True)

    k_blocks = loaders(k_ref, kc_ref if n_cache else None)
    v_blocks = loaders(v_ref, vc_ref if n_cache else None)
    ki_blocks = loaders(ki_ref, kic_ref if n_cache else None)

    def index_part(r0):
        qi = qi_ref[r0:r0 + th, :]
        wi = kiwi_ref[r0:r0 + th, IDX_DIM:IDX_DIM + IDX_HEADS]
        heads = []
        for h in range(IDX_HEADS):
            slab = qi[:, 128 * (h // 4):128 * (h // 4) + 128]
            if h % 4:
                slab = pltpu.roll(slab, 128 - IDX_DIM * (h % 4), 1)
            heads.append(jnp.where(lane_q128 < IDX_DIM, slab, 0.0))
        qis = jnp.concatenate(heads, axis=0).astype(BF16)
        wis = jnp.concatenate([wi[:, h:h + 1] for h in range(IDX_HEADS)], axis=0)
        n_adm_part = n_adm[r0:r0 + th]

        def score_block(off, kir, _):
            s = _dot_nt(qis[:, 0:kir.shape[1]], kir, preferred_element_type=F32)
            s = jnp.maximum(s, 0.0) * wis
            isc = s[0:th]
            for h in range(1, IDX_HEADS):
                isc = isc + s[h * th:(h + 1) * th]
            isc = isc * idx_scale
            keys_ref[r0:r0 + th, pl.ds(off, kb_w)] = jnp.where(off + lane_k < n_adm_part, isc, -jnp.inf)

        for_blocks(score_block, *ki_blocks)

    for r0 in range(0, tq, th):
        index_part(r0)

    def key_to_score(key):
        return pltpu.bitcast(jnp.where(key < 0, key ^ 0x7FFFFFFF, key), F32)

    def count_ge(cand_key):
        accs = []
        for r0 in range(0, tq, th):
            cand_part = key_to_score(cand_key[r0:r0 + th])

            def body(kb, acc, r0=r0, cand_part=cand_part):
                off = pl.multiple_of(kb * kb_w, kb_w)
                kblk = keys_ref[r0:r0 + th, pl.ds(off, kb_w)]
                for j in range(kb_w // 128):
                    acc = acc + jnp.where(kblk[:, j * 128:(j + 1) * 128] >= cand_part, 1.0, 0.0)
                return acc

            accs.append(lax.fori_loop(0, nkb, body, jnp.zeros((th, 128), F32)))
        acc = accs[0] if len(accs) == 1 else jnp.concatenate(accs, axis=0)
        return jnp.sum(acc, axis=1, keepdims=True)

    c0 = count_ge(jnp.zeros((tq, 128), I32))
    t0 = jnp.where(c0 >= topk, jnp.zeros((tq, 128), I32), jnp.full((tq, 128), INT_MIN, I32))
    n0 = jnp.where(c0 >= topk, c0, n_adm.astype(F32))

    def bit_step(i, carry):
        t, n_t = carry
        cand = t | jnp.left_shift(jnp.int32(1), 30 - i)
        n_cand = count_ge(cand)
        take = n_cand >= topk
        return jnp.where(take, cand, t), jnp.where(take, n_cand, n_t)

    thr_key, n_ge = lax.fori_loop(0, 31, bit_step, (t0, n0))
    thr_key = jnp.maximum(thr_key, LOWEST_FINITE_KEY)
    thr = key_to_score(thr_key)[:, 0:1]

    @pl.when(jnp.max(n_ge) > topk)
    def _():
        n_tie_take = topk - count_ge(thr_key + 1)
        ri = lax.broadcasted_iota(I32, (kb_w, kb_w), 0)
        ci = lax.broadcasted_iota(I32, (kb_w, kb_w), 1)
        upper = (ri <= ci).astype(BF16)

        def body(kb, seen):
            off = pl.multiple_of(kb * kb_w, kb_w)
            kblk = keys_ref[:, pl.ds(off, kb_w)]
            tie = kblk == thr
            rank = seen + jnp.dot(jnp.where(tie, 1.0, 0.0).astype(BF16), upper,
                                  preferred_element_type=F32)
            keys_ref[:, pl.ds(off, kb_w)] = jnp.where(tie & (rank > n_tie_take), -jnp.inf, kblk)
            return rank[:, kb_w - 1:kb_w]

        lax.fori_loop(0, nkb, body, jnp.zeros((tq, 1), F32))

    rows = B_GROUP * th
    lane = lax.broadcasted_iota(I32, (th, 128), 1)

    def fold_lanes(x, op):
        part = x[:, 0:128]
        for j in range(1, kb_w // 128):
            part = op(part, x[:, j * 128:(j + 1) * 128])
        return part

    def attend_part(r0):
        q = q_ref[r0:r0 + th, :]
        thr_part = thr[r0:r0 + th]
        q_chunk = (tile_pos // CHUNK + row_chunk[r0:r0 + th]).astype(F32)
        q_row = (lax.broadcasted_iota(I32, (th, 128), 0) & (CHUNK - 1)).astype(F32)
        qpos = tile_pos + r0 + lax.broadcasted_iota(I32, (th, 1), 0)
        qaug, slope2, own_half = [], [], []
        for n in range(B_KV_HEADS):
            q_parts, slope_parts = [], []
            keep = (lane >= n * B_HEAD_DIM) & (lane < (n + 1) * B_HEAD_DIM)
            own_half.append((lane_q128 >= n * B_HEAD_DIM) & (lane_q128 < (n + 1) * B_HEAD_DIM))
            pos_lane = lane - (1 - n) * B_HEAD_DIM
            for g in range(B_GROUP):
                h = n * B_GROUP + g
                slope = 2.0 ** (-(8.0 / B_HEADS) * (h + 1))
                slab = q[:, 128 * (h // 2):128 * (h // 2) + 128]
                if h % 2 != n:
                    slab = pltpu.roll(slab, B_HEAD_DIM, 1)
                pos_feat = jnp.where(pos_lane == 0, CHUNK * slope,
                           jnp.where(pos_lane == 1, slope,
                           jnp.where(pos_lane == 2, -CHUNK * slope * q_chunk,
                           jnp.where(pos_lane == 3, -slope * q_row, 0.0))))
                q_parts.append(jnp.where(keep, slab * att_scale, pos_feat))
                slope_parts.append(jnp.full((th, 1), 2.0 * slope, F32))
            qaug.append(jnp.concatenate(q_parts, axis=0).astype(BF16))
            slope2.append(jnp.concatenate(slope_parts, axis=0))

        macc_ref[...] = jnp.full(macc_ref.shape, NEG_BIG, F32)
        oacc_ref[...] = jnp.zeros(oacc_ref.shape, F32)

        def score_pass(off, kblk, own_chunk):
            sel = keys_ref[r0:r0 + th, pl.ds(off, kb_w)] >= thr_part
            sel4 = jnp.concatenate([sel] * B_GROUP, axis=0)
            pblk = posf_ref[pl.ds(off, kb_w), :]
            if own_chunk:
                ahead = jnp.maximum((off + lane_k) - qpos, 0).astype(F32)
                ahead4 = jnp.concatenate([ahead] * B_GROUP, axis=0)
            for n in range(B_KV_HEADS):
                kaug = jnp.where(own_half[n], kblk, pblk)
                s = _dot_nt(qaug[n], kaug, preferred_element_type=F32)
                if own_chunk:
                    s = s - slope2[n] * ahead4
                s = jnp.where(sel4, s, NEG_BIG)
                s_ref[n, :, pl.ds(off, kb_w)] = s
                macc_ref[n] = jnp.maximum(macc_ref[n], fold_lanes(s, jnp.maximum))

        for_blocks(score_pass, *k_blocks)

        for n in range(B_KV_HEADS):
            m = jnp.max(macc_ref[n], axis=1, keepdims=True)
            macc_ref[n] = jnp.broadcast_to(m, (rows, 128))

        ones_blk = jnp.ones((kb_w, 128), BF16)

        def value_pass(off, vblk, _):
            vaug = jnp.concatenate([vblk, ones_blk], axis=1)
            for n in range(B_KV_HEADS):
                m_b = macc_ref[n]
                p = jnp.exp(s_ref[n, :, pl.ds(off, kb_w)]
                            - jnp.concatenate([m_b] * (kb_w // 128), axis=1))
                oacc_ref[n] = oacc_ref[n] + jnp.dot(p.astype(BF16), vaug, preferred_element_type=F32)

        for_blocks(value_pass, *v_blocks)
        pieces = []
        for n in range(B_KV_HEADS):
            acc = oacc_ref[n]
            o_n = acc[:, n * B_HEAD_DIM:(n + 1) * B_HEAD_DIM] / acc[:, 128:129]
            pieces += [o_n[g * th:(g + 1) * th] for g in range(B_GROUP)]
        o_ref[r0:r0 + th, :] = jnp.concatenate(pieces, axis=1)

    for r0 in range(0, tq, th):
        attend_part(r0)


def _dsa(proj, bsz, t_len, cache=None):
    kb_w = KEY_BLOCK
    pos0 = 0 if cache is None else cache[0].shape[1]
    tq = next(c for c in (8 * CHUNK, 4 * CHUNK, 2 * CHUNK, CHUNK) if t_len % c == 0 and pos0 % c == 0)
    th = min(tq, 2 * CHUNK)
    nq = t_len // tq
    n_cache = pos0 // kb_w
    s_pad = pos0 + kb_w if cache is not None else t_len
    assert s_pad % kb_w == 0 and pos0 % kb_w == 0 and kb_w % tq == 0
    assert cache is None or t_len == tq
    rows = B_GROUP * th
    proj3 = proj.reshape(bsz, t_len, N_PAD)
    key_cols = (C_KD // 128, C_VD // 128, C_KIWI // 128)
    if cache is None:
        key_specs = [pl.BlockSpec((1, s_pad, 128), functools.partial(lambda b, t, c: (b, 0, c), c=c))
                     for c in key_cols]
        key_args = (proj3, proj3, proj3)
    else:
        key_specs = [pl.BlockSpec((1, tq, 128), functools.partial(lambda b, t, c: (b, t, c), c=c))
                     for c in key_cols]
        key_specs += [pl.BlockSpec((1, pos0, a.shape[2]), lambda b, t: (b, 0, 0)) for a in cache]
        key_args = (proj3, proj3, proj3) + tuple(cache)
    kpos = lax.broadcasted_iota(I32, (s_pad, 128), 0)
    feat = lax.broadcasted_iota(I32, (s_pad, 128), 1)
    feat = feat % B_HEAD_DIM
    posf = jnp.where(feat == 0, kpos // CHUNK, jnp.where(feat == 1, kpos % CHUNK,
                     jnp.where(feat < 4, 1, 0))).astype(BF16)
    return pl.pallas_call(
        functools.partial(_dsa_kernel, pos0=pos0, kb_w=kb_w, tq=tq, th=th, n_cache=n_cache,
                          topk=min(MAX_TOPK, (pos0 + t_len) // 4)),
        grid=(bsz, nq),
        in_specs=[
            pl.BlockSpec((tq, 512), lambda b, t: (b * nq + t, C_Q // 512)),
            pl.BlockSpec((tq, 256), lambda b, t: (b * nq + t, C_QI // 256)),
            pl.BlockSpec((tq, 128), lambda b, t: (b * nq + t, C_KIWI // 128)),
            *key_specs,
            pl.BlockSpec((s_pad, 128), lambda b, t: (0, 0)),
        ],
        out_specs=pl.BlockSpec((tq, B_WIDTH), lambda b, t: (b * nq + t, 0)),
        out_shape=jax.ShapeDtypeStruct((bsz * t_len, B_WIDTH), F32),
        scratch_shapes=[
            pltpu.VMEM((tq, s_pad), F32),
            pltpu.VMEM((B_KV_HEADS, rows, s_pad), F32),
            pltpu.VMEM((B_KV_HEADS, rows, 128), F32),
            pltpu.VMEM((B_KV_HEADS, rows, 256), F32),
        ],
        compiler_params=pltpu.CompilerParams(
            dimension_semantics=("arbitrary", "arbitrary"), vmem_limit_bytes=VMEM_LIMIT),
        name="dsa",
    )(proj, proj, proj, *key_args, posf)


def _merge_kernel(x_ref, ya_ref, yb_ref, gd_ref, gab_ref, wpa_ref, wpb_ref, wo_ref, fnw_ref, o_ref):
    gd = gd_ref[...]
    yb = yb_ref[...] * (gd * _sigmoid(gd))
    pa = jnp.dot(ya_ref[...].astype(BF16), wpa_ref[...], preferred_element_type=F32)
    pb = jnp.dot(yb.astype(BF16), wpb_ref[...], preferred_element_type=F32)
    merged = _sigmoid(gab_ref[:, 0:D_MODEL]) * pa + _sigmoid(gab_ref[:, D_MODEL:2 * D_MODEL]) * pb
    out = x_ref[...] + jnp.dot(merged.astype(BF16), wo_ref[...], preferred_element_type=F32)
    ms = jnp.mean(out * out, axis=-1, keepdims=True)
    o_ref[...] = (out * lax.rsqrt(ms + NORM_EPS)) * fnw_ref[...]


def _merge(x2d, ya, yb, proj, w_pa, w_pb, w_o, final_w):
    m = x2d.shape[0]
    tm = min(512, m)
    full = lambda shape: pl.BlockSpec(shape, lambda i: (0, 0))
    return pl.pallas_call(
        _merge_kernel,
        grid=(m // tm,),
        in_specs=[
            pl.BlockSpec((tm, D_MODEL), lambda i: (i, 0)),
            pl.BlockSpec((tm, A_WIDTH), lambda i: (i, 0)),
            pl.BlockSpec((tm, B_WIDTH), lambda i: (i, 0)),
            pl.BlockSpec((tm, 512), lambda i: (i, C_GD // 512)),
            pl.BlockSpec((tm, 2048), lambda i: (i, C_GATES // 2048)),
            full((A_WIDTH, D_MODEL)), full((B_WIDTH, D_MODEL)), full((D_MODEL, D_MODEL)),
            full((1, D_MODEL)),
        ],
        out_specs=pl.BlockSpec((tm, D_MODEL), lambda i: (i, 0)),
        out_shape=jax.ShapeDtypeStruct((m, D_MODEL), F32),
        compiler_params=pltpu.CompilerParams(
            dimension_semantics=("arbitrary",), vmem_limit_bytes=VMEM_LIMIT),
        name="merge",
    )(x2d, ya, yb, proj, proj, w_pa.astype(BF16), w_pb.astype(BF16), w_o.astype(BF16),
      final_w.reshape(1, D_MODEL))


def _rwkv_order(row):
    return row[..., 0:2048], row[..., 2048:2176]


def _mixer(x, shift_prev, wkv_prev, past_k, past_v, past_ki, w_perm, norm_w, prm, w_pa, w_pb, w_o,
           final_w):
    bsz, t_len, _ = x.shape
    x2d = x.reshape(bsz * t_len, D_MODEL)
    proj = _proj(x2d, norm_w, w_perm)
    proj3 = proj.reshape(bsz, t_len, N_PAD)

    shift4, shiftw = _rwkv_order(shift_prev)
    ya, wkv_new = _rwkv(proj, bsz, t_len, shift4, shiftw, wkv_prev, prm)

    k_new = proj3[:, :, C_KD:C_KD + 128]
    v_new = proj3[:, :, C_VD:C_VD + 128]
    ki_new = proj3[:, :, C_KIWI:C_KIWI + IDX_DIM]
    past_len = 0 if past_k is None else past_k.shape[1]
    if past_len == 0:
        yb = _dsa(proj, bsz, t_len)
    else:
        yb = _dsa(proj, bsz, t_len, cache=(past_k.reshape(bsz, past_len, 128),
                                           past_v.reshape(bsz, past_len, 128), past_ki))

    y = _merge(x2d, ya, yb, proj, w_pa, w_pb, w_o, final_w).reshape(bsz, t_len, D_MODEL)
    last = proj3[:, t_len - 1:t_len, :]
    shift_new = jnp.concatenate([last[..., C_RKVG:C_RKVG + 2048], last[..., C_WDAD:C_WDAD + 128]], axis=-1)
    kv_shape = (bsz, t_len, B_KV_HEADS, B_HEAD_DIM)
    return y, k_new.reshape(kv_shape), v_new.reshape(kv_shape), ki_new, wkv_new, shift_new


def kernel(x_prompt, x_sample, cache_k, cache_v, cache_kidx, state_wkv, state_shift, norm_w, w_in,
           shift_mu, decay_w0, decay_up, iclr_a0, iclr_up, k_k, k_a, r_k, gn_w, gn_b, w_pa, w_pb,
           w_o, final_norm_w):
    assert w_in.shape[0] == 1, "the final norm is fused into the (single) layer's merge kernel"
    bp = x_prompt.shape[0]
    w_perm = _permute_w_in(w_in[0])
    mu4, muw = _rwkv_order(shift_mu[0].reshape(1, RWKV_COLS))
    row = lambda a: a.reshape(1, A_WIDTH)
    prm = dict(mu4=mu4, muw=muw, w0=row(decay_w0[0]), dup=decay_up[0], a0=row(iclr_a0[0]),
               aup=iclr_up[0], kk=row(k_k[0]), ka=row(k_a[0]), rk=row(r_k[0]), gnw=row(gn_w[0]),
               gnb=row(gn_b[0]))
    common = (w_perm, norm_w[0], prm, w_pa[0], w_pb[0], w_o[0], final_norm_w)
    yp, kp, vp, kip, wkvp, shp = _mixer(
        x_prompt, jnp.zeros((bp, 1, RWKV_COLS), F32),
        jnp.zeros((bp, A_HEADS, A_HEAD_DIM, A_HEAD_DIM), F32), None, None, None, *common)
    ys, ks, vs, kis, wkvs, shs = _mixer(
        x_sample, state_shift[0], state_wkv[0], cache_k[0], cache_v[0], cache_kidx[0], *common)
    st = lambda a: a[None]
    return (yp, ys, st(kp), st(vp), st(kip), st(wkvp), st(shp),
            st(ks), st(vs), st(kis), st(wkvs), st(shs))
```

```python
import functools

import jax
import jax.numpy as jnp
from jax import lax
from jax.experimental import pallas as pl
from jax.experimental.pallas import tpu as pltpu

F32 = jnp.float32
BF16 = jnp.bfloat16
I32 = jnp.int32
HIGHEST = lax.Precision.HIGHEST

D_MODEL = 1024
CHUNK = 64
A_HEADS = 8
A_HEAD_DIM = 64
A_WIDTH = 512
LORA = 64
RWKV_COLS = 4 * A_WIDTH + 2 * LORA
B_HEADS = 8
B_KV_HEADS = 2
B_GROUP = 4
B_HEAD_DIM = 64
B_WIDTH = 512
B_KV_WIDTH = 128
IDX_HEADS = 8
IDX_DIM = 32
MAX_TOPK = 256
DSA_COLS = 1576
N_IN = 5800
NORM_EPS = 1e-6
GN_EPS = 64e-5

C_GATES = 0
C_RKVG = 2048
C_Q = 4096
C_GD = 4608
C_QI = 5120
C_KD = 5376
C_VD = 5504
C_KIWI = 5632
C_WDAD = 5760
N_PAD = 5888

KEY_BLOCK = 512
BLOCK_GROUPS = (4, 2, 1)
INT_MIN = -(2 ** 31)
LOWEST_FINITE_KEY = INT_MIN + 0x00800000
NEG_BIG = -1e30
VMEM_LIMIT = 56 * 1024 * 1024


def _sigmoid(x):
    return 1.0 / (1.0 + jnp.exp(-x))


def _dot_nt(a, b, **kw):
    return lax.dot_general(a, b, (((1,), (1,)), ((), ())), **kw)


def _dot_tn(a, b, **kw):
    return lax.dot_general(a, b, (((0,), (0,)), ((), ())), **kw)


def _permute_w_in(w):
    d0 = RWKV_COLS
    g0 = RWKV_COLS + DSA_COLS
    cols = [
        w[:, g0:g0 + 2048],
        w[:, 0:2048],
        w[:, d0:d0 + 512],
        w[:, d0 + 1064:d0 + 1576],
        w[:, d0 + 768:d0 + 1024],
        w[:, d0 + 512:d0 + 640],
        w[:, d0 + 640:d0 + 768],
        w[:, d0 + 1024:d0 + 1064],
        jnp.zeros((w.shape[0], 128 - IDX_DIM - IDX_HEADS), w.dtype),
        w[:, 2048:2176],
    ]
    return jnp.concatenate(cols, axis=1).astype(BF16)


def _proj_kernel(x_ref, nw_ref, w_ref, o_ref):
    x = x_ref[...]
    ms = jnp.mean(x * x, axis=-1, keepdims=True)
    h = (x * lax.rsqrt(ms + NORM_EPS)) * nw_ref[...]
    o_ref[...] = jnp.dot(h.astype(BF16), w_ref[...], preferred_element_type=F32)


def _proj(x2d, norm_w, w_perm):
    m = x2d.shape[0]
    tm = min(512, m)
    tn = N_PAD // 2
    return pl.pallas_call(
        _proj_kernel,
        grid=(N_PAD // tn, m // tm),
        in_specs=[
            pl.BlockSpec((tm, D_MODEL), lambda j, i: (i, 0)),
            pl.BlockSpec((1, D_MODEL), lambda j, i: (0, 0)),
            pl.BlockSpec((D_MODEL, tn), lambda j, i: (0, j)),
        ],
        out_specs=pl.BlockSpec((tm, tn), lambda j, i: (i, j)),
        out_shape=jax.ShapeDtypeStruct((m, N_PAD), F32),
        compiler_params=pltpu.CompilerParams(
            dimension_semantics=("arbitrary", "arbitrary"), vmem_limit_bytes=VMEM_LIMIT),
        name="proj",
    )(x2d, norm_w.reshape(1, D_MODEL), w_perm)


def _split_bf16(x, terms):
    pieces = []
    for _ in range(terms):
        piece = x.astype(BF16)
        pieces.append(piece)
        x = x - piece.astype(F32)
    return pieces


def _dot_exact_rhs(a, b_bf16, terms):
    acc = None
    for piece in _split_bf16(a, terms):
        d = jnp.dot(piece, b_bf16, preferred_element_type=F32)
        acc = d if acc is None else acc + d
    return acc


def _dot_3pass(a, b):
    ah, al = _split_bf16(a, 2)
    bh, bl = _split_bf16(b, 2)
    dot = functools.partial(jnp.dot, preferred_element_type=F32)
    return dot(ah, bh) + dot(ah, bl) + dot(al, bh)


def _rwkv_kernel(p4_ref, pw_ref, sp4_ref, spw_ref, s0_ref, mu4_ref, muw_ref, w0_ref, dup_ref,
                 a0_ref, aup_ref, kk_ref, ka_ref, rk_ref, gnw_ref, gnb_ref, bd_ref, tri_ref,
                 ya_ref, sout_ref, c4_ref, cw_ref, sbd_ref, *, nch):
    c = CHUNK
    n = A_HEAD_DIM
    rows = nch * c
    n_pairs = A_HEADS // 2
    t_idx = pl.program_id(1)
    lane = lax.broadcasted_iota(I32, (c, 2 * n), 1)
    lo_half = lane < n
    row_c = lax.broadcasted_iota(I32, (c, 2 * n), 0)
    pos_in_head = jnp.where(lo_half, lane, lane - n)
    tri_strict = (pos_in_head < row_c).astype(F32)
    tri_incl = (pos_in_head <= row_c).astype(F32)
    eye2 = (pos_in_head == row_c).astype(F32)
    lane_sq = lax.broadcasted_iota(I32, (2 * n, 2 * n), 1)
    row_sq = lax.broadcasted_iota(I32, (2 * n, 2 * n), 0)
    same_head = (lane_sq < n) == (row_sq < n)
    dot = functools.partial(jnp.dot, preferred_element_type=F32)

    def bdiag(x):
        zero = jnp.zeros_like(x)
        return jnp.concatenate([jnp.where(lo_half, x, zero), jnp.where(lo_half, zero, x)], axis=0)

    @pl.when(t_idx == 0)
    def _():
        c4_ref[...] = sp4_ref[0]
        cw_ref[...] = spw_ref[0]
        zeros = jnp.zeros((n, n), F32)
        for j in range(n_pairs):
            sbd_ref[j] = jnp.concatenate(
                [jnp.concatenate([s0_ref[0, 2 * j], zeros], axis=1),
                 jnp.concatenate([zeros, s0_ref[0, 2 * j + 1]], axis=1)], axis=0)

    p4 = p4_ref[...]
    pw = pw_ref[...]
    row = lax.broadcasted_iota(I32, (rows, 1), 0)
    prev4 = jnp.where(row == 0, c4_ref[...], pltpu.roll(p4, 1, 0))
    prevw = jnp.where(row == 0, cw_ref[...], pltpu.roll(pw, 1, 0))
    c4_ref[...] = p4[rows - 1:rows, :]
    cw_ref[...] = pw[rows - 1:rows, :]
    ps4 = p4 + (prev4 - p4) * mu4_ref[...]
    psw = pw + (prevw - pw) * muw_ref[...]
    r = ps4[:, 0:512]
    k = ps4[:, 512:1024]
    v = ps4[:, 1024:1536]
    g = ps4[:, 1536:2048]
    wd = psw[:, 0:LORA]
    ad = psw[:, LORA:2 * LORA]
    bd = bd_ref[...]

    xw = w0_ref[...] + _dot_3pass(jnp.tanh(wd), dup_ref[...])
    z = -xw
    softplus = jnp.maximum(z, 0.0) + jnp.log(1.0 + jnp.exp(-jnp.abs(z)))
    lw = -jnp.exp(-softplus - 0.5)
    a = _sigmoid(a0_ref[...] + _dot_3pass(ad, aup_ref[...]))
    kkr = k * kk_ref[...]
    kkn = kkr / jnp.maximum(jnp.sqrt(_dot_exact_rhs(kkr * kkr, bd, 2)), 1e-12)
    kmod = k * (1.0 + (a - 1.0) * ka_ref[...])

    cum = None
    for piece in _split_bf16(lw, 3):
        d = dot(tri_ref[...], piece)
        cum = d if cum is None else cum + d
    pdec = jnp.exp(cum)
    pinv = jnp.exp(-cum)
    rt = (r * pdec).astype(BF16)
    at = (-kkn * jnp.exp(cum - lw)).astype(BF16)
    bt = (kkn * a * pinv).astype(BF16)
    kt = (kmod * pinv).astype(BF16)
    vb = v.astype(BF16)

    tiles = [(ci, j) for ci in range(nch) for j in range(n_pairs)]
    rsl = lambda ci: slice(ci * c, (ci + 1) * c)
    lsl = lambda j: slice(j * 2 * n, (j + 1) * 2 * n)
    lhs, bk, a_ak_rk, a_rb, xs, tinv = {}, {}, {}, {}, {}, {}
    for ci, j in tiles:
        rs, ls = rsl(ci), lsl(j)
        lhs[ci, j] = jnp.concatenate([at[rs, ls], rt[rs, ls]], axis=0)
        bk[ci, j] = jnp.concatenate([bt[rs, ls], kt[rs, ls]], axis=0)
        amat = _dot_nt(lhs[ci, j], jnp.concatenate([bdiag(bt[rs, ls]), bdiag(kt[rs, ls])], axis=0),
                       preferred_element_type=F32)
        xs[ci, j] = amat[0:c, 0:2 * n] * tri_strict
        a_ak_rk[ci, j] = jnp.concatenate([amat[0:c, 2 * n:4 * n] * tri_strict,
                                          amat[c:2 * c, 2 * n:4 * n] * tri_incl], axis=0).astype(BF16)
        a_rb[ci, j] = (amat[c:2 * c, 0:2 * n] * tri_incl).astype(BF16)
        tinv[ci, j] = eye2 + xs[ci, j]
    for t in tiles:
        xb = xs[t].astype(BF16)
        xs[t] = dot(xb, bdiag(xb))
    for _ in range(4):
        for t in tiles:
            xb = xs[t].astype(BF16)
            both = dot(xb, jnp.concatenate([bdiag(tinv[t].astype(BF16)), bdiag(xb)], axis=1))
            tinv[t] = tinv[t] + both[:, 0:2 * n]
            xs[t] = both[:, 2 * n:4 * n]
    for t in tiles:
        tinv[t] = tinv[t] + dot(xs[t].astype(BF16), bdiag(tinv[t].astype(BF16)))
    akv = {}
    for ci, j in tiles:
        tinv[ci, j] = tinv[ci, j].astype(BF16)
        akv[ci, j] = dot(a_ak_rk[ci, j], bdiag(vb[rsl(ci), lsl(j)]))

    pairs = range(n_pairs)
    s_pair = [sbd_ref[j] for j in pairs]
    y_chunks = []
    for ci in range(nch):
        from_state = [_dot_nt(lhs[ci, j], s_pair[j].astype(BF16), preferred_element_type=F32)
                      for j in pairs]
        u = [dot(tinv[ci, j], bdiag((from_state[j][0:c] + akv[ci, j][0:c]).astype(BF16)))
             for j in pairs]
        uv_t = [jnp.transpose(jnp.concatenate([u[j], v[rsl(ci), lsl(j)]], axis=0)).astype(BF16)
                for j in pairs]
        upd = [dot(uv_t[j], bk[ci, j]) for j in pairs]
        s_pair = [(s_pair[j] + jnp.where(same_head, upd[j], 0.0))
                  * pdec[(ci + 1) * c - 1:(ci + 1) * c, lsl(j)] for j in pairs]
        y_chunks.append(jnp.concatenate(
            [from_state[j][c:2 * c] + akv[ci, j][c:2 * c] + dot(a_rb[ci, j], bdiag(u[j].astype(BF16)))
             for j in pairs], axis=1))
    for j in pairs:
        sbd_ref[j] = s_pair[j]
    y = y_chunks[0] if nch == 1 else jnp.concatenate(y_chunks, axis=0)

    inv_n = 1.0 / n
    mean = _dot_exact_rhs(y, bd, 2) * inv_n
    dlt = y - mean
    var = _dot_exact_rhs(dlt * dlt, bd, 2) * inv_n
    yn = dlt * lax.rsqrt(var + GN_EPS) * gnw_ref[...] + gnb_ref[...]
    yn = yn + _dot_exact_rhs(r * kmod * rk_ref[...], bd, 2) * v
    ya_ref[...] = yn * (g * _sigmoid(g))

    @pl.when(t_idx == pl.num_programs(1) - 1)
    def _():
        for j in range(n_pairs):
            s_pair = sbd_ref[j]
            sout_ref[0, 2 * j] = s_pair[0:n, 0:n]
            sout_ref[0, 2 * j + 1] = s_pair[n:2 * n, n:2 * n]


def _rwkv(proj, bsz, t_len, shift4, shiftw, wkv_prev, prm):
    nch = min(4, t_len // CHUNK)
    rows = nch * CHUNK
    nt = t_len // rows
    row1 = lambda width: pl.BlockSpec((1, width), lambda b, t: (0, 0))
    head = lax.broadcasted_iota(I32, (A_WIDTH, A_WIDTH), 0) // A_HEAD_DIM
    bd = (head == head.T).astype(BF16)
    ti = lax.broadcasted_iota(I32, (rows, rows), 0)
    si = lax.broadcasted_iota(I32, (rows, rows), 1)
    tri = ((ti // CHUNK == si // CHUNK) & (si <= ti)).astype(BF16)
    return pl.pallas_call(
        functools.partial(_rwkv_kernel, nch=nch),
        grid=(bsz, nt),
        in_specs=[
            pl.BlockSpec((rows, 2048), lambda b, t: (b * nt + t, C_RKVG // 2048)),
            pl.BlockSpec((rows, 128), lambda b, t: (b * nt + t, C_WDAD // 128)),
            pl.BlockSpec((1, 1, 2048), lambda b, t: (b, 0, 0)),
            pl.BlockSpec((1, 1, 128), lambda b, t: (b, 0, 0)),
            pl.BlockSpec((1, A_HEADS, A_HEAD_DIM, A_HEAD_DIM), lambda b, t: (b, 0, 0, 0)),
            row1(2048), row1(128), row1(A_WIDTH),
            pl.BlockSpec((LORA, A_WIDTH), lambda b, t: (0, 0)),
            row1(A_WIDTH),
            pl.BlockSpec((LORA, A_WIDTH), lambda b, t: (0, 0)),
            row1(A_WIDTH), row1(A_WIDTH), row1(A_WIDTH), row1(A_WIDTH), row1(A_WIDTH),
            pl.BlockSpec((A_WIDTH, A_WIDTH), lambda b, t: (0, 0)),
            pl.BlockSpec((rows, rows), lambda b, t: (0, 0)),
        ],
        out_specs=[
            pl.BlockSpec((rows, A_WIDTH), lambda b, t: (b * nt + t, 0)),
            pl.BlockSpec((1, A_HEADS, A_HEAD_DIM, A_HEAD_DIM), lambda b, t: (b, 0, 0, 0)),
        ],
        out_shape=[
            jax.ShapeDtypeStruct((bsz * t_len, A_WIDTH), F32),
            jax.ShapeDtypeStruct((bsz, A_HEADS, A_HEAD_DIM, A_HEAD_DIM), F32),
        ],
        scratch_shapes=[pltpu.VMEM((1, 2048), F32), pltpu.VMEM((1, 128), F32),
                        pltpu.VMEM((A_HEADS // 2, 2 * A_HEAD_DIM, 2 * A_HEAD_DIM), F32)],
        compiler_params=pltpu.CompilerParams(
            dimension_semantics=("arbitrary", "arbitrary"), vmem_limit_bytes=VMEM_LIMIT),
        name="rwkv",
    )(proj, proj, shift4, shiftw, wkv_prev, prm["mu4"], prm["muw"], prm["w0"], prm["dup"],
      prm["a0"], prm["aup"], prm["kk"], prm["ka"], prm["rk"], prm["gnw"], prm["gnb"], bd, tri)


def _dsa_kernel(q_ref, qi_ref, kiwi_ref, k_ref, v_ref, ki_ref, *rest, pos0, kb_w, topk, tq, th, n_cache):
    if n_cache:
        kc_ref, vc_ref, kic_ref, posf_ref, o_ref, keys_ref, s_ref, macc_ref, oacc_ref = rest
    else:
        posf_ref, o_ref, keys_ref, s_ref, macc_ref, oacc_ref = rest
    qt = pl.program_id(1)
    tile_pos = pos0 + qt * tq
    row_chunk = jnp.right_shift(lax.broadcasted_iota(I32, (tq, 1), 0), 6)
    n_adm = tile_pos + (row_chunk + 1) * CHUNK
    nkb = n_cache + 1 if n_cache else (tile_pos + tq + kb_w - 1) // kb_w
    idx_scale = (IDX_HEADS ** -0.5) * (IDX_DIM ** -0.5)
    att_scale = B_HEAD_DIM ** -0.5
    topk = float(topk)
    lane_q128 = lax.broadcasted_iota(I32, (1, 128), 1)
    lane_k = lax.broadcasted_iota(I32, (1, kb_w), 1)

    def loaders(all_ref, cache_ref):
        if n_cache:
            def new_rows():
                new = all_ref[0].astype(BF16)
                return jnp.concatenate([new, jnp.zeros((kb_w - tq, new.shape[1]), BF16)], axis=0)
            return (lambda off: cache_ref[0, pl.ds(off, kb_w), :].astype(BF16)), new_rows
        at = lambda off: all_ref[0, pl.ds(off, kb_w), :].astype(BF16)
        return at, (lambda: at(last_off))

    last_off = n_cache * kb_w if n_cache else pl.multiple_of((nkb - 1) * kb_w, kb_w)

    def for_blocks(body, block_at, last_block):
        def group_step(width, first):
            def step(j, carry):
                offs = [pl.multiple_of((first + width * j + i) * kb_w, kb_w) for i in range(width)]
                blocks = [block_at(off) for off in offs]
                for off, block in zip(offs, blocks):
                    body(off, block, False)
                return carry
            return step

        done = 0
        for width in BLOCK_GROUPS:
            trips = (nkb - 1 - done) // width
            lax.fori_loop(0, trips, group_step(width, done), 0)
            done = done + trips * width
        body(last_off, last_block(), True)

    k_blocks = loaders(k_ref, kc_ref if n_cache else None)
    v_blocks = loaders(v_ref, vc_ref if n_cache else None)
    ki_blocks = loaders(ki_ref, kic_ref if n_cache else None)

    def index_part(r0):
        qi = qi_ref[r0:r0 + th, :]
        wi = kiwi_ref[r0:r0 + th, IDX_DIM:IDX_DIM + IDX_HEADS]
        heads = []
        for h in range(IDX_HEADS):
            slab = qi[:, 128 * (h // 4):128 * (h // 4) + 128]
            if h % 4:
                slab = pltpu.roll(slab, 128 - IDX_DIM * (h % 4), 1)
            heads.append(jnp.where(lane_q128 < IDX_DIM, slab, 0.0))
        qis = jnp.concatenate(heads, axis=0).astype(BF16)
        wis = jnp.concatenate([wi[:, h:h + 1] for h in range(IDX_HEADS)], axis=0)
        n_adm_part = n_adm[r0:r0 + th]

        def score_block(off, kir, _):
            s = _dot_nt(qis[:, 0:kir.shape[1]], kir, preferred_element_type=F32)
            s = jnp.maximum(s, 0.0) * wis
            isc = s[0:th]
            for h in range(1, IDX_HEADS):
                isc = isc + s[h * th:(h + 1) * th]
            isc = isc * idx_scale
            keys_ref[r0:r0 + th, pl.ds(off, kb_w)] = jnp.where(off + lane_k < n_adm_part, isc, -jnp.inf)

        for_blocks(score_block, *ki_blocks)

    for r0 in range(0, tq, th):
        index_part(r0)

    def key_to_score(key):
        return pltpu.bitcast(jnp.where(key < 0, key ^ 0x7FFFFFFF, key), F32)

    def count_ge(cand_key):
        accs = []
        for r0 in range(0, tq, th):
            cand_part = key_to_score(cand_key[r0:r0 + th])

            def body(kb, acc, r0=r0, cand_part=cand_part):
                off = pl.multiple_of(kb * kb_w, kb_w)
                kblk = keys_ref[r0:r0 + th, pl.ds(off, kb_w)]
                for j in range(kb_w // 128):
                    acc = acc + jnp.where(kblk[:, j * 128:(j + 1) * 128] >= cand_part, 1.0, 0.0)
                return acc

            accs.append(lax.fori_loop(0, nkb, body, jnp.zeros((th, 128), F32)))
        acc = accs[0] if len(accs) == 1 else jnp.concatenate(accs, axis=0)
        return jnp.sum(acc, axis=1, keepdims=True)

    c0 = count_ge(jnp.zeros((tq, 128), I32))
    t0 = jnp.where(c0 >= topk, jnp.zeros((tq, 128), I32), jnp.full((tq, 128), INT_MIN, I32))
    n0 = jnp.where(c0 >= topk, c0, n_adm.astype(F32))

    def bit_step(i, carry):
        t, n_t = carry
        cand = t | jnp.left_shift(jnp.int32(1), 30 - i)
        n_cand = count_ge(cand)
        take = n_cand >= topk
        return jnp.where(take, cand, t), jnp.where(take, n_cand, n_t)

    thr_key, n_ge = lax.fori_loop(0, 31, bit_step, (t0, n0))
    thr_key = jnp.maximum(thr_key, LOWEST_FINITE_KEY)
    thr = key_to_score(thr_key)[:, 0:1]

    @pl.when(jnp.max(n_ge) > topk)
    def _():
        n_tie_take = topk - count_ge(thr_key + 1)
        ri = lax.broadcasted_iota(I32, (kb_w, kb_w), 0)
        ci = lax.broadcasted_iota(I32, (kb_w, kb_w), 1)
        upper = (ri <= ci).astype(BF16)

        def body(kb, seen):
            off = pl.multiple_of(kb * kb_w, kb_w)
            kblk = keys_ref[:, pl.ds(off, kb_w)]
            tie = kblk == thr
            rank = seen + jnp.dot(jnp.where(tie, 1.0, 0.0).astype(BF16), upper,
                                  preferred_element_type=F32)
            keys_ref[:, pl.ds(off, kb_w)] = jnp.where(tie & (rank > n_tie_take), -jnp.inf, kblk)
            return rank[:, kb_w - 1:kb_w]

        lax.fori_loop(0, nkb, body, jnp.zeros((tq, 1), F32))

    rows = B_GROUP * th
    lane = lax.broadcasted_iota(I32, (th, 128), 1)

    def fold_lanes(x, op):
        part = x[:, 0:128]
        for j in range(1, kb_w // 128):
            part = op(part, x[:, j * 128:(j + 1) * 128])
        return part

    def attend_part(r0):
        q = q_ref[r0:r0 + th, :]
        thr_part = thr[r0:r0 + th]
        q_chunk = (tile_pos // CHUNK + row_chunk[r0:r0 + th]).astype(F32)
        q_row = (lax.broadcasted_iota(I32, (th, 128), 0) & (CHUNK - 1)).astype(F32)
        qpos = tile_pos + r0 + lax.broadcasted_iota(I32, (th, 1), 0)
        qaug, slope2, own_half = [], [], []
        for n in range(B_KV_HEADS):
            q_parts, slope_parts = [], []
            keep = (lane >= n * B_HEAD_DIM) & (lane < (n + 1) * B_HEAD_DIM)
            own_half.append((lane_q128 >= n * B_HEAD_DIM) & (lane_q128 < (n + 1) * B_HEAD_DIM))
            pos_lane = lane - (1 - n) * B_HEAD_DIM
            for g in range(B_GROUP):
                h = n * B_GROUP + g
                slope = 2.0 ** (-(8.0 / B_HEADS) * (h + 1))
                slab = q[:, 128 * (h // 2):128 * (h // 2) + 128]
                if h % 2 != n:
                    slab = pltpu.roll(slab, B_HEAD_DIM, 1)
                pos_feat = jnp.where(pos_lane == 0, CHUNK * slope,
                           jnp.where(pos_lane == 1, slope,
                           jnp.where(pos_lane == 2, -CHUNK * slope * q_chunk,
                           jnp.where(pos_lane == 3, -slope * q_row, 0.0))))
                q_parts.append(jnp.where(keep, slab * att_scale, pos_feat))
                slope_parts.append(jnp.full((th, 1), 2.0 * slope, F32))
            qaug.append(jnp.concatenate(q_parts, axis=0).astype(BF16))
            slope2.append(jnp.concatenate(slope_parts, axis=0))

        macc_ref[...] = jnp.full(macc_ref.shape, NEG_BIG, F32)
        oacc_ref[...] = jnp.zeros(oacc_ref.shape, F32)

        def score_pass(off, kblk, own_chunk):
            sel = keys_ref[r0:r0 + th, pl.ds(off, kb_w)] >= thr_part
            sel4 = jnp.concatenate([sel] * B_GROUP, axis=0)
            pblk = posf_ref[pl.ds(off, kb_w), :]
            if own_chunk:
                ahead = jnp.maximum((off + lane_k) - qpos, 0).astype(F32)
                ahead4 = jnp.concatenate([ahead] * B_GROUP, axis=0)
            for n in range(B_KV_HEADS):
                kaug = jnp.where(own_half[n], kblk, pblk)
                s = _dot_nt(qaug[n], kaug, preferred_element_type=F32)
                if own_chunk:
                    s = s - slope2[n] * ahead4
                s = jnp.where(sel4, s, NEG_BIG)
                s_ref[n, :, pl.ds(off, kb_w)] = s
                macc_ref[n] = jnp.maximum(macc_ref[n], fold_lanes(s, jnp.maximum))

        for_blocks(score_pass, *k_blocks)

        for n in range(B_KV_HEADS):
            m = jnp.max(macc_ref[n], axis=1, keepdims=True)
            macc_ref[n] = jnp.broadcast_to(m, (rows, 128))

        ones_blk = jnp.ones((kb_w, 128), BF16)

        def value_pass(off, vblk, _):
            vaug = jnp.concatenate([vblk, ones_blk], axis=1)
            for n in range(B_KV_HEADS):
                m_b = macc_ref[n]
                p = jnp.exp(s_ref[n, :, pl.ds(off, kb_w)]
                            - jnp.concatenate([m_b] * (kb_w // 128), axis=1))
                oacc_ref[n] = oacc_ref[n] + jnp.dot(p.astype(BF16), vaug, preferred_element_type=F32)

        for_blocks(value_pass, *v_blocks)
        pieces = []
        for n in range(B_KV_HEADS):
            acc = oacc_ref[n]
            o_n = acc[:, n * B_HEAD_DIM:(n + 1) * B_HEAD_DIM] / acc[:, 128:129]
            pieces += [o_n[g * th:(g + 1) * th] for g in range(B_GROUP)]
        o_ref[r0:r0 + th, :] = jnp.concatenate(pieces, axis=1)

    for r0 in range(0, tq, th):
        attend_part(r0)


def _dsa(proj, bsz, t_len, cache=None):
    kb_w = KEY_BLOCK
    pos0 = 0 if cache is None else cache[0].shape[1]
    tq = next(c for c in (8 * CHUNK, 4 * CHUNK, 2 * CHUNK, CHUNK) if t_len % c == 0 and pos0 % c == 0)
    th = min(tq, 2 * CHUNK)
    nq = t_len // tq
    n_cache = pos0 // kb_w
    s_pad = pos0 + kb_w if cache is not None else t_len
    assert s_pad % kb_w == 0 and pos0 % kb_w == 0 and kb_w % tq == 0
    assert cache is None or t_len == tq
    rows = B_GROUP * th
    proj3 = proj.reshape(bsz, t_len, N_PAD)
    key_cols = (C_KD // 128, C_VD // 128, C_KIWI // 128)
    if cache is None:
        key_specs = [pl.BlockSpec((1, s_pad, 128), functools.partial(lambda b, t, c: (b, 0, c), c=c))
                     for c in key_cols]
        key_args = (proj3, proj3, proj3)
    else:
        key_specs = [pl.BlockSpec((1, tq, 128), functools.partial(lambda b, t, c: (b, t, c), c=c))
                     for c in key_cols]
        key_specs += [pl.BlockSpec((1, pos0, a.shape[2]), lambda b, t: (b, 0, 0)) for a in cache]
        key_args = (proj3, proj3, proj3) + tuple(cache)
    kpos = lax.broadcasted_iota(I32, (s_pad, 128), 0)
    feat = lax.broadcasted_iota(I32, (s_pad, 128), 1)
    feat = feat % B_HEAD_DIM
    posf = jnp.where(feat == 0, kpos // CHUNK, jnp.where(feat == 1, kpos % CHUNK,
                     jnp.where(feat < 4, 1, 0))).astype(BF16)
    return pl.pallas_call(
        functools.partial(_dsa_kernel, pos0=pos0, kb_w=kb_w, tq=tq, th=th, n_cache=n_cache,
                          topk=min(MAX_TOPK, (pos0 + t_len) // 4)),
        grid=(bsz, nq),
        in_specs=[
            pl.BlockSpec((tq, 512), lambda b, t: (b * nq + t, C_Q // 512)),
            pl.BlockSpec((tq, 256), lambda b, t: (b * nq + t, C_QI // 256)),
            pl.BlockSpec((tq, 128), lambda b, t: (b * nq + t, C_KIWI // 128)),
            *key_specs,
            pl.BlockSpec((s_pad, 128), lambda b, t: (0, 0)),
        ],
        out_specs=pl.BlockSpec((tq, B_WIDTH), lambda b, t: (b * nq + t, 0)),
        out_shape=jax.ShapeDtypeStruct((bsz * t_len, B_WIDTH), F32),
        scratch_shapes=[
            pltpu.VMEM((tq, s_pad), F32),
            pltpu.VMEM((B_KV_HEADS, rows, s_pad), F32),
            pltpu.VMEM((B_KV_HEADS, rows, 128), F32),
            pltpu.VMEM((B_KV_HEADS, rows, 256), F32),
        ],
        compiler_params=pltpu.CompilerParams(
            dimension_semantics=("arbitrary", "arbitrary"), vmem_limit_bytes=VMEM_LIMIT),
        name="dsa",
    )(proj, proj, proj, *key_args, posf)


def _merge_kernel(x_ref, ya_ref, yb_ref, gd_ref, gab_ref, wpa_ref, wpb_ref, wo_ref, fnw_ref, o_ref):
    gd = gd_ref[...]
    yb = yb_ref[...] * (gd * _sigmoid(gd))
    pa = jnp.dot(ya_ref[...].astype(BF16), wpa_ref[...], preferred_element_type=F32)
    pb = jnp.dot(yb.astype(BF16), wpb_ref[...], preferred_element_type=F32)
    merged = _sigmoid(gab_ref[:, 0:D_MODEL]) * pa + _sigmoid(gab_ref[:, D_MODEL:2 * D_MODEL]) * pb
    out = x_ref[...] + jnp.dot(merged.astype(BF16), wo_ref[...], preferred_element_type=F32)
    ms = jnp.mean(out * out, axis=-1, keepdims=True)
    o_ref[...] = (out * lax.rsqrt(ms + NORM_EPS)) * fnw_ref[...]


def _merge(x2d, ya, yb, proj, w_pa, w_pb, w_o, final_w):
    m = x2d.shape[0]
    tm = min(512, m)
    full = lambda shape: pl.BlockSpec(shape, lambda i: (0, 0))
    return pl.pallas_call(
        _merge_kernel,
        grid=(m // tm,),
        in_specs=[
            pl.BlockSpec((tm, D_MODEL), lambda i: (i, 0)),
            pl.BlockSpec((tm, A_WIDTH), lambda i: (i, 0)),
            pl.BlockSpec((tm, B_WIDTH), lambda i: (i, 0)),
            pl.BlockSpec((tm, 512), lambda i: (i, C_GD // 512)),
            pl.BlockSpec((tm, 2048), lambda i: (i, C_GATES // 2048)),
            full((A_WIDTH, D_MODEL)), full((B_WIDTH, D_MODEL)), full((D_MODEL, D_MODEL)),
            full((1, D_MODEL)),
        ],
        out_specs=pl.BlockSpec((tm, D_MODEL), lambda i: (i, 0)),
        out_shape=jax.ShapeDtypeStruct((m, D_MODEL), F32),
        compiler_params=pltpu.CompilerParams(
            dimension_semantics=("arbitrary",), vmem_limit_bytes=VMEM_LIMIT),
        name="merge",
    )(x2d, ya, yb, proj, proj, w_pa.astype(BF16), w_pb.astype(BF16), w_o.astype(BF16),
      final_w.reshape(1, D_MODEL))


def _rwkv_order(row):
    return row[..., 0:2048], row[..., 2048:2176]


def _mixer(x, shift_prev, wkv_prev, past_k, past_v, past_ki, w_perm, norm_w, prm, w_pa, w_pb, w_o,
           final_w):
    bsz, t_len, _ = x.shape
    x2d = x.reshape(bsz * t_len, D_MODEL)
    proj = _proj(x2d, norm_w, w_perm)
    proj3 = proj.reshape(bsz, t_len, N_PAD)

    shift4, shiftw = _rwkv_order(shift_prev)
    ya, wkv_new = _rwkv(proj, bsz, t_len, shift4, shiftw, wkv_prev, prm)

    k_new = proj3[:, :, C_KD:C_KD + 128]
    v_new = proj3[:, :, C_VD:C_VD + 128]
    ki_new = proj3[:, :, C_KIWI:C_KIWI + IDX_DIM]
    past_len = 0 if past_k is None else past_k.shape[1]
    if past_len == 0:
        yb = _dsa(proj, bsz, t_len)
    else:
        yb = _dsa(proj, bsz, t_len, cache=(past_k.reshape(bsz, past_len, 128),
                                           past_v.reshape(bsz, past_len, 128), past_ki))

    y = _merge(x2d, ya, yb, proj, w_pa, w_pb, w_o, final_w).reshape(bsz, t_len, D_MODEL)
    last = proj3[:, t_len - 1:t_len, :]
    shift_new = jnp.concatenate([last[..., C_RKVG:C_RKVG + 2048], last[..., C_WDAD:C_WDAD + 128]], axis=-1)
    kv_shape = (bsz, t_len, B_KV_HEADS, B_HEAD_DIM)
    return y, k_new.reshape(kv_shape), v_new.reshape(kv_shape), ki_new, wkv_new, shift_new


def kernel(x_prompt, x_sample, cache_k, cache_v, cache_kidx, state_wkv, state_shift, norm_w, w_in,
           shift_mu, decay_w0, decay_up, iclr_a0, iclr_up, k_k, k_a, r_k, gn_w, gn_b, w_pa, w_pb,
           w_o, final_norm_w):
    assert w_in.shape[0] == 1, "the final norm is fused into the (single) layer's merge kernel"
    bp = x_prompt.shape[0]
    w_perm = _permute_w_in(w_in[0])
    mu4, muw = _rwkv_order(shift_mu[0].reshape(1, RWKV_COLS))
    row = lambda a: a.reshape(1, A_WIDTH)
    prm = dict(mu4=mu4, muw=muw, w0=row(decay_w0[0]), dup=decay_up[0], a0=row(iclr_a0[0]),
               aup=iclr_up[0], kk=row(k_k[0]), ka=row(k_a[0]), rk=row(r_k[0]), gnw=row(gn_w[0]),
               gnb=row(gn_b[0]))
    common = (w_perm, norm_w[0], prm, w_pa[0], w_pb[0], w_o[0], final_norm_w)
    yp, kp, vp, kip, wkvp, shp = _mixer(
        x_prompt, jnp.zeros((bp, 1, RWKV_COLS), F32),
        jnp.zeros((bp, A_HEADS, A_HEAD_DIM, A_HEAD_DIM), F32), None, None, None, *common)
    ys, ks, vs, kis, wkvs, shs = _mixer(
        x_sample, state_shift[0], state_wkv[0], cache_k[0], cache_v[0], cache_kidx[0], *common)
    st = lambda a: a[None]
    return (yp, ys, st(kp), st(vp), st(kip), st(wkvp), st(shp),
            st(ks), st(vs), st(kis), st(wkvs), st(shs))
```

```python
import functools

import jax
import jax.numpy as jnp
from jax import lax
from jax.experimental import pallas as pl
from jax.experimental.pallas import tpu as pltpu

F32 = jnp.float32
BF16 = jnp.bfloat16
I32 = jnp.int32
HIGHEST = lax.Precision.HIGHEST

D_MODEL = 1024
CHUNK = 64
A_HEADS = 8
A_HEAD_DIM = 64
A_WIDTH = 512
LORA = 64
RWKV_COLS = 4 * A_WIDTH + 2 * LORA
B_HEADS = 8
B_KV_HEADS = 2
B_GROUP = 4
B_HEAD_DIM = 64
B_WIDTH = 512
B_KV_WIDTH = 128
IDX_HEADS = 8
IDX_DIM = 32
MAX_TOPK = 256
DSA_COLS = 1576
N_IN = 5800
NORM_EPS = 1e-6
GN_EPS = 64e-5

C_GATES = 0
C_RKVG = 2048
C_Q = 4096
C_GD = 4608
C_QI = 5120
C_KD = 5376
C_VD = 5504
C_KIWI = 5632
C_WDAD = 5760
N_PAD = 5888

KEY_BLOCK = 512
BLOCK_GROUPS = (4, 2, 1)
INT_MIN = -(2 ** 31)
LOWEST_FINITE_KEY = INT_MIN + 0x00800000
NEG_BIG = -1e30
VMEM_LIMIT = 56 * 1024 * 1024


def _sigmoid(x):
    return 1.0 / (1.0 + jnp.exp(-x))


def _dot_nt(a, b, **kw):
    return lax.dot_general(a, b, (((1,), (1,)), ((), ())), **kw)


def _dot_tn(a, b, **kw):
    return lax.dot_general(a, b, (((0,), (0,)), ((), ())), **kw)


def _permute_w_in(w):
    d0 = RWKV_COLS
    g0 = RWKV_COLS + DSA_COLS
    cols = [
        w[:, g0:g0 + 2048],
        w[:, 0:2048],
        w[:, d0:d0 + 512],
        w[:, d0 + 1064:d0 + 1576],
        w[:, d0 + 768:d0 + 1024],
        w[:, d0 + 512:d0 + 640],
        w[:, d0 + 640:d0 + 768],
        w[:, d0 + 1024:d0 + 1064],
        jnp.zeros((w.shape[0], 128 - IDX_DIM - IDX_HEADS), w.dtype),
        w[:, 2048:2176],
    ]
    return jnp.concatenate(cols, axis=1).astype(BF16)


def _proj_kernel(x_ref, nw_ref, w_ref, o_ref):
    x = x_ref[...]
    ms = jnp.mean(x * x, axis=-1, keepdims=True)
    h = (x * lax.rsqrt(ms + NORM_EPS)) * nw_ref[...]
    o_ref[...] = jnp.dot(h.astype(BF16), w_ref[...], preferred_element_type=F32)


def _proj(x2d, norm_w, w_perm):
    m = x2d.shape[0]
    tm = min(512, m)
    tn = N_PAD // 2
    return pl.pallas_call(
        _proj_kernel,
        grid=(N_PAD // tn, m // tm),
        in_specs=[
            pl.BlockSpec((tm, D_MODEL), lambda j, i: (i, 0)),
            pl.BlockSpec((1, D_MODEL), lambda j, i: (0, 0)),
            pl.BlockSpec((D_MODEL, tn), lambda j, i: (0, j)),
        ],
        out_specs=pl.BlockSpec((tm, tn), lambda j, i: (i, j)),
        out_shape=jax.ShapeDtypeStruct((m, N_PAD), F32),
        compiler_params=pltpu.CompilerParams(
            dimension_semantics=("arbitrary", "arbitrary"), vmem_limit_bytes=VMEM_LIMIT),
        name="proj",
    )(x2d, norm_w.reshape(1, D_MODEL), w_perm)


def _split_bf16(x, terms):
    pieces = []
    for _ in range(terms):
        piece = x.astype(BF16)
        pieces.append(piece)
        x = x - piece.astype(F32)
    return pieces


def _dot_exact_rhs(a, b_bf16, terms):
    acc = None
    for piece in _split_bf16(a, terms):
        d = jnp.dot(piece, b_bf16, preferred_element_type=F32)
        acc = d if acc is None else acc + d
    return acc


def _dot_3pass(a, b):
    ah, al = _split_bf16(a, 2)
    bh, bl = _split_bf16(b, 2)
    dot = functools.partial(jnp.dot, preferred_element_type=F32)
    return dot(ah, bh) + dot(ah, bl) + dot(al, bh)


def _rwkv_kernel(p4_ref, pw_ref, sp4_ref, spw_ref, s0_ref, mu4_ref, muw_ref, w0_ref, dup_ref,
                 a0_ref, aup_ref, kk_ref, ka_ref, rk_ref, gnw_ref, gnb_ref, bd_ref, tri_ref,
                 ya_ref, sout_ref, c4_ref, cw_ref, sbd_ref, *, nch):
    c = CHUNK
    n = A_HEAD_DIM
    rows = nch * c
    n_pairs = A_HEADS // 2
    t_idx = pl.program_id(1)
    lane = lax.broadcasted_iota(I32, (c, 2 * n), 1)
    lo_half = lane < n
    row_c = lax.broadcasted_iota(I32, (c, 2 * n), 0)
    pos_in_head = jnp.where(lo_half, lane, lane - n)
    tri_strict = (pos_in_head < row_c).astype(F32)
    tri_incl = (pos_in_head <= row_c).astype(F32)
    eye2 = (pos_in_head == row_c).astype(F32)
    lane_sq = lax.broadcasted_iota(I32, (2 * n, 2 * n), 1)
    row_sq = lax.broadcasted_iota(I32, (2 * n, 2 * n), 0)
    same_head = (lane_sq < n) == (row_sq < n)
    dot = functools.partial(jnp.dot, preferred_element_type=F32)

    def bdiag(x):
        zero = jnp.zeros_like(x)
        return jnp.concatenate([jnp.where(lo_half, x, zero), jnp.where(lo_half, zero, x)], axis=0)

    @pl.when(t_idx == 0)
    def _():
        c4_ref[...] = sp4_ref[0]
        cw_ref[...] = spw_ref[0]
        zeros = jnp.zeros((n, n), F32)
        for j in range(n_pairs):
            sbd_ref[j] = jnp.concatenate(
                [jnp.concatenate([s0_ref[0, 2 * j], zeros], axis=1),
                 jnp.concatenate([zeros, s0_ref[0, 2 * j + 1]], axis=1)], axis=0)

    p4 = p4_ref[...]
    pw = pw_ref[...]
    row = lax.broadcasted_iota(I32, (rows, 1), 0)
    prev4 = jnp.where(row == 0, c4_ref[...], pltpu.roll(p4, 1, 0))
    prevw = jnp.where(row == 0, cw_ref[...], pltpu.roll(pw, 1, 0))
    c4_ref[...] = p4[rows - 1:rows, :]
    cw_ref[...] = pw[rows - 1:rows, :]
    ps4 = p4 + (prev4 - p4) * mu4_ref[...]
    psw = pw + (prevw - pw) * muw_ref[...]
    r = ps4[:, 0:512]
    k = ps4[:, 512:1024]
    v = ps4[:, 1024:1536]
    g = ps4[:, 1536:2048]
    wd = psw[:, 0:LORA]
    ad = psw[:, LORA:2 * LORA]
    bd = bd_ref[...]

    xw = w0_ref[...] + _dot_3pass(jnp.tanh(wd), dup_ref[...])
    z = -xw
    softplus = jnp.maximum(z, 0.0) + jnp.log(1.0 + jnp.exp(-jnp.abs(z)))
    lw = -jnp.exp(-softplus - 0.5)
    a = _sigmoid(a0_ref[...] + _dot_3pass(ad, aup_ref[...]))
    kkr = k * kk_ref[...]
    kkn = kkr / jnp.maximum(jnp.sqrt(_dot_exact_rhs(kkr * kkr, bd, 2)), 1e-12)
    kmod = k * (1.0 + (a - 1.0) * ka_ref[...])

    cum = None
    for piece in _split_bf16(lw, 3):
        d = dot(tri_ref[...], piece)
        cum = d if cum is None else cum + d
    pdec = jnp.exp(cum)
    pinv = jnp.exp(-cum)
    rt = (r * pdec).astype(BF16)
    at = (-kkn * jnp.exp(cum - lw)).astype(BF16)
    bt = (kkn * a * pinv).astype(BF16)
    kt = (kmod * pinv).astype(BF16)
    vb = v.astype(BF16)

    tiles = [(ci, j) for ci in range(nch) for j in range(n_pairs)]
    rsl = lambda ci: slice(ci * c, (ci + 1) * c)
    lsl = lambda j: slice(j * 2 * n, (j + 1) * 2 * n)
    lhs, bk, a_ak_rk, a_rb, xs, tinv = {}, {}, {}, {}, {}, {}
    for ci, j in tiles:
        rs, ls = rsl(ci), lsl(j)
        lhs[ci, j] = jnp.concatenate([at[rs, ls], rt[rs, ls]], axis=0)
        bk[ci, j] = jnp.concatenate([bt[rs, ls], kt[rs, ls]], axis=0)
        amat = _dot_nt(lhs[ci, j], jnp.concatenate([bdiag(bt[rs, ls]), bdiag(kt[rs, ls])], axis=0),
                       preferred_element_type=F32)
        xs[ci, j] = amat[0:c, 0:2 * n] * tri_strict
        a_ak_rk[ci, j] = jnp.concatenate([amat[0:c, 2 * n:4 * n] * tri_strict,
                                          amat[c:2 * c, 2 * n:4 * n] * tri_incl], axis=0).astype(BF16)
        a_rb[ci, j] = (amat[c:2 * c, 0:2 * n] * tri_incl).astype(BF16)
        tinv[ci, j] = eye2 + xs[ci, j]
    for t in tiles:
        xb = xs[t].astype(BF16)
        xs[t] = dot(xb, bdiag(xb))
    for _ in range(4):
        for t in tiles:
            xb = xs[t].astype(BF16)
            both = dot(xb, jnp.concatenate([bdiag(tinv[t].astype(BF16)), bdiag(xb)], axis=1))
            tinv[t] = tinv[t] + both[:, 0:2 * n]
            xs[t] = both[:, 2 * n:4 * n]
    for t in tiles:
        tinv[t] = tinv[t] + dot(xs[t].astype(BF16), bdiag(tinv[t].astype(BF16)))
    akv = {}
    for ci, j in tiles:
        tinv[ci, j] = tinv[ci, j].astype(BF16)
        akv[ci, j] = dot(a_ak_rk[ci, j], bdiag(vb[rsl(ci), lsl(j)]))

    pairs = range(n_pairs)
    s_pair = [sbd_ref[j] for j in pairs]
    y_chunks = []
    for ci in range(nch):
        from_state = [_dot_nt(lhs[ci, j], s_pair[j].astype(BF16), preferred_element_type=F32)
                      for j in pairs]
        u = [dot(tinv[ci, j], bdiag((from_state[j][0:c] + akv[ci, j][0:c]).astype(BF16)))
             for j in pairs]
        uv_t = [jnp.transpose(jnp.concatenate([u[j], v[rsl(ci), lsl(j)]], axis=0)).astype(BF16)
                for j in pairs]
        upd = [dot(uv_t[j], bk[ci, j]) for j in pairs]
        s_pair = [(s_pair[j] + jnp.where(same_head, upd[j], 0.0))
                  * pdec[(ci + 1) * c - 1:(ci + 1) * c, lsl(j)] for j in pairs]
        y_chunks.append(jnp.concatenate(
            [from_state[j][c:2 * c] + akv[ci, j][c:2 * c] + dot(a_rb[ci, j], bdiag(u[j].astype(BF16)))
             for j in pairs], axis=1))
    for j in pairs:
        sbd_ref[j] = s_pair[j]
    y = y_chunks[0] if nch == 1 else jnp.concatenate(y_chunks, axis=0)

    inv_n = 1.0 / n
    mean = _dot_exact_rhs(y, bd, 2) * inv_n
    dlt = y - mean
    var = _dot_exact_rhs(dlt * dlt, bd, 2) * inv_n
    yn = dlt * lax.rsqrt(var + GN_EPS) * gnw_ref[...] + gnb_ref[...]
    yn = yn + _dot_exact_rhs(r * kmod * rk_ref[...], bd, 2) * v
    ya_ref[...] = yn * (g * _sigmoid(g))

    @pl.when(t_idx == pl.num_programs(1) - 1)
    def _():
        for j in range(n_pairs):
            s_pair = sbd_ref[j]
            sout_ref[0, 2 * j] = s_pair[0:n, 0:n]
            sout_ref[0, 2 * j + 1] = s_pair[n:2 * n, n:2 * n]


def _rwkv(proj, bsz, t_len, shift4, shiftw, wkv_prev, prm):
    nch = min(4, t_len // CHUNK)
    rows = nch * CHUNK
    nt = t_len // rows
    row1 = lambda width: pl.BlockSpec((1, width), lambda b, t: (0, 0))
    head = lax.broadcasted_iota(I32, (A_WIDTH, A_WIDTH), 0) // A_HEAD_DIM
    bd = (head == head.T).astype(BF16)
    ti = lax.broadcasted_iota(I32, (rows, rows), 0)
    si = lax.broadcasted_iota(I32, (rows, rows), 1)
    tri = ((ti // CHUNK == si // CHUNK) & (si <= ti)).astype(BF16)
    return pl.pallas_call(
        functools.partial(_rwkv_kernel, nch=nch),
        grid=(bsz, nt),
        in_specs=[
            pl.BlockSpec((rows, 2048), lambda b, t: (b * nt + t, C_RKVG // 2048)),
            pl.BlockSpec((rows, 128), lambda b, t: (b * nt + t, C_WDAD // 128)),
            pl.BlockSpec((1, 1, 2048), lambda b, t: (b, 0, 0)),
            pl.BlockSpec((1, 1, 128), lambda b, t: (b, 0, 0)),
            pl.BlockSpec((1, A_HEADS, A_HEAD_DIM, A_HEAD_DIM), lambda b, t: (b, 0, 0, 0)),
            row1(2048), row1(128), row1(A_WIDTH),
            pl.BlockSpec((LORA, A_WIDTH), lambda b, t: (0, 0)),
            row1(A_WIDTH),
            pl.BlockSpec((LORA, A_WIDTH), lambda b, t: (0, 0)),
            row1(A_WIDTH), row1(A_WIDTH), row1(A_WIDTH), row1(A_WIDTH), row1(A_WIDTH),
            pl.BlockSpec((A_WIDTH, A_WIDTH), lambda b, t: (0, 0)),
            pl.BlockSpec((rows, rows), lambda b, t: (0, 0)),
        ],
        out_specs=[
            pl.BlockSpec((rows, A_WIDTH), lambda b, t: (b * nt + t, 0)),
            pl.BlockSpec((1, A_HEADS, A_HEAD_DIM, A_HEAD_DIM), lambda b, t: (b, 0, 0, 0)),
        ],
        out_shape=[
            jax.ShapeDtypeStruct((bsz * t_len, A_WIDTH), F32),
            jax.ShapeDtypeStruct((bsz, A_HEADS, A_HEAD_DIM, A_HEAD_DIM), F32),
        ],
        scratch_shapes=[pltpu.VMEM((1, 2048), F32), pltpu.VMEM((1, 128), F32),
                        pltpu.VMEM((A_HEADS // 2, 2 * A_HEAD_DIM, 2 * A_HEAD_DIM), F32)],
        compiler_params=pltpu.CompilerParams(
            dimension_semantics=("arbitrary", "arbitrary"), vmem_limit_bytes=VMEM_LIMIT),
        name="rwkv",
    )(proj, proj, shift4, shiftw, wkv_prev, prm["mu4"], prm["muw"], prm["w0"], prm["dup"],
      prm["a0"], prm["aup"], prm["kk"], prm["ka"], prm["rk"], prm["gnw"], prm["gnb"], bd, tri)


def _dsa_kernel(q_ref, qi_ref, kiwi_ref, k_ref, v_ref, ki_ref, *rest, pos0, kb_w, topk, tq, th, n_cache):
    if n_cache:
        kc_ref, vc_ref, kic_ref, *rest = rest
    posf_ref, o_ref, keys_ref, s_ref, macc_ref, oacc_ref, thr_ref = rest
    qt = pl.program_id(1)
    tile_pos = pos0 + qt * tq
    row_chunk = jnp.right_shift(lax.broadcasted_iota(I32, (tq, 1), 0), 6)
    n_adm = tile_pos + (row_chunk + 1) * CHUNK
    nkb = n_cache + 1 if n_cache else (tile_pos + tq + kb_w - 1) // kb_w
    idx_scale = (IDX_HEADS ** -0.5) * (IDX_DIM ** -0.5)
    att_scale = B_HEAD_DIM ** -0.5
    topk = float(topk)
    lane_q128 = lax.broadcasted_iota(I32, (1, 128), 1)
    lane_k = lax.broadcasted_iota(I32, (1, kb_w), 1)

    def loaders(all_ref, cache_ref):
        if n_cache:
            def new_rows():
                new = all_ref[0].astype(BF16)
                return jnp.concatenate([new, jnp.zeros((kb_w - tq, new.shape[1]), BF16)], axis=0)
            return (lambda off: cache_ref[0, pl.ds(off, kb_w), :].astype(BF16)), new_rows
        at = lambda off: all_ref[0, pl.ds(off, kb_w), :].astype(BF16)
        return at, (lambda: at(last_off))

    last_off = n_cache * kb_w if n_cache else pl.multiple_of((nkb - 1) * kb_w, kb_w)

    def for_blocks(body, block_at, last_block):
        def group_step(width, first):
            def step(j, carry):
                offs = [pl.multiple_of((first + width * j + i) * kb_w, kb_w) for i in range(width)]
                blocks = [block_at(off) for off in offs]
                for off, block in zip(offs, blocks):
                    body(off, block, False)
                return carry
            return step

        done = 0
        for width in BLOCK_GROUPS:
            trips = (nkb - 1 - done) // width
            lax.fori_loop(0, trips, group_step(width, done), 0)
            done = done + trips * width
        body(last_off, last_block(), True)

    def for_parts(fn):
        def step(p, carry):
            fn(pl.multiple_of(p * th, th))
            return carry

        lax.fori_loop(0, tq // th, step, 0)

    def part_chunk(r0):
        return jnp.right_shift(r0 + lax.broadcasted_iota(I32, (th, 1), 0), 6)

    k_blocks = loaders(k_ref, kc_ref if n_cache else None)
    v_blocks = loaders(v_ref, vc_ref if n_cache else None)
    ki_blocks = loaders(ki_ref, kic_ref if n_cache else None)

    def index_part(r0):
        qi = qi_ref[pl.ds(r0, th), :]
        wi = kiwi_ref[pl.ds(r0, th), IDX_DIM:IDX_DIM + IDX_HEADS]
        heads = []
        for h in range(IDX_HEADS):
            slab = qi[:, 128 * (h // 4):128 * (h // 4) + 128]
            if h % 4:
                slab = pltpu.roll(slab, 128 - IDX_DIM * (h % 4), 1)
            heads.append(jnp.where(lane_q128 < IDX_DIM, slab, 0.0))
        qis = jnp.concatenate(heads, axis=0).astype(BF16)
        wis = jnp.concatenate([wi[:, h:h + 1] for h in range(IDX_HEADS)], axis=0)
        n_adm_part = tile_pos + (part_chunk(r0) + 1) * CHUNK

        def score_block(off, kir, _):
            s = _dot_nt(qis[:, 0:kir.shape[1]], kir, preferred_element_type=F32)
            s = jnp.maximum(s, 0.0) * wis
            isc = s[0:th]
            for h in range(1, IDX_HEADS):
                isc = isc + s[h * th:(h + 1) * th]
            isc = isc * idx_scale
            keys_ref[pl.ds(r0, th), pl.ds(off, kb_w)] = jnp.where(off + lane_k < n_adm_part, isc, -jnp.inf)

        for_blocks(score_block, *ki_blocks)

    for_parts(index_part)

    def key_to_score(key):
        return pltpu.bitcast(jnp.where(key < 0, key ^ 0x7FFFFFFF, key), F32)

    def count_ge(cand_key):
        accs = []
        for r0 in range(0, tq, th):
            cand_part = key_to_score(cand_key[r0:r0 + th])

            def body(kb, acc, r0=r0, cand_part=cand_part):
                off = pl.multiple_of(kb * kb_w, kb_w)
                kblk = keys_ref[r0:r0 + th, pl.ds(off, kb_w)]
                for j in range(kb_w // 128):
                    acc = acc + jnp.where(kblk[:, j * 128:(j + 1) * 128] >= cand_part, 1.0, 0.0)
                return acc

            accs.append(lax.fori_loop(0, nkb, body, jnp.zeros((th, 128), F32)))
        acc = accs[0] if len(accs) == 1 else jnp.concatenate(accs, axis=0)
        return jnp.sum(acc, axis=1, keepdims=True)

    c0 = count_ge(jnp.zeros((tq, 128), I32))
    t0 = jnp.where(c0 >= topk, jnp.zeros((tq, 128), I32), jnp.full((tq, 128), INT_MIN, I32))
    n0 = jnp.where(c0 >= topk, c0, n_adm.astype(F32))

    def bit_step(i, carry):
        t, n_t = carry
        cand = t | jnp.left_shift(jnp.int32(1), 30 - i)
        n_cand = count_ge(cand)
        take = n_cand >= topk
        return jnp.where(take, cand, t), jnp.where(take, n_cand, n_t)

    thr_key, n_ge = lax.fori_loop(0, 31, bit_step, (t0, n0))
    thr_key = jnp.maximum(thr_key, LOWEST_FINITE_KEY)
    thr_ref[...] = key_to_score(thr_key)
    thr = thr_ref[:, 0:1]

    @pl.when(jnp.max(n_ge) > topk)
    def _():
        n_tie_take = topk - count_ge(thr_key + 1)
        ri = lax.broadcasted_iota(I32, (kb_w, kb_w), 0)
        ci = lax.broadcasted_iota(I32, (kb_w, kb_w), 1)
        upper = (ri <= ci).astype(BF16)

        def body(kb, seen):
            off = pl.multiple_of(kb * kb_w, kb_w)
            kblk = keys_ref[:, pl.ds(off, kb_w)]
            tie = kblk == thr
            rank = seen + jnp.dot(jnp.where(tie, 1.0, 0.0).astype(BF16), upper,
                                  preferred_element_type=F32)
            keys_ref[:, pl.ds(off, kb_w)] = jnp.where(tie & (rank > n_tie_take), -jnp.inf, kblk)
            return rank[:, kb_w - 1:kb_w]

        lax.fori_loop(0, nkb, body, jnp.zeros((tq, 1), F32))

    rows = B_GROUP * th
    lane = lax.broadcasted_iota(I32, (th, 128), 1)

    def fold_lanes(x, op):
        part = x[:, 0:128]
        for j in range(1, kb_w // 128):
            part = op(part, x[:, j * 128:(j + 1) * 128])
        return part

    def attend_part(r0):
        q = q_ref[pl.ds(r0, th), :]
        thr_part = thr_ref[pl.ds(r0, th), 0:1]
        q_chunk = (tile_pos // CHUNK + part_chunk(r0)).astype(F32)
        q_row = (lax.broadcasted_iota(I32, (th, 128), 0) & (CHUNK - 1)).astype(F32)
        qpos = tile_pos + r0 + lax.broadcasted_iota(I32, (th, 1), 0)
        qaug, slope2, own_half = [], [], []
        for n in range(B_KV_HEADS):
            q_parts, slope_parts = [], []
            keep = (lane >= n * B_HEAD_DIM) & (lane < (n + 1) * B_HEAD_DIM)
            own_half.append((lane_q128 >= n * B_HEAD_DIM) & (lane_q128 < (n + 1) * B_HEAD_DIM))
            pos_lane = lane - (1 - n) * B_HEAD_DIM
            for g in range(B_GROUP):
                h = n * B_GROUP + g
                slope = 2.0 ** (-(8.0 / B_HEADS) * (h + 1))
                slab = q[:, 128 * (h // 2):128 * (h // 2) + 128]
                if h % 2 != n:
                    slab = pltpu.roll(slab, B_HEAD_DIM, 1)
                pos_feat = jnp.where(pos_lane == 0, CHUNK * slope,
                           jnp.where(pos_lane == 1, slope,
                           jnp.where(pos_lane == 2, -CHUNK * slope * q_chunk,
                           jnp.where(pos_lane == 3, -slope * q_row, 0.0))))
                q_parts.append(jnp.where(keep, slab * att_scale, pos_feat))
                slope_parts.append(jnp.full((th, 1), 2.0 * slope, F32))
            qaug.append(jnp.concatenate(q_parts, axis=0).astype(BF16))
            slope2.append(jnp.concatenate(slope_parts, axis=0))

        macc_ref[...] = jnp.full(macc_ref.shape, NEG_BIG, F32)
        oacc_ref[...] = jnp.zeros(oacc_ref.shape, F32)

        def score_pass(off, kblk, own_chunk):
            sel = keys_ref[pl.ds(r0, th), pl.ds(off, kb_w)] >= thr_part
            sel4 = jnp.concatenate([sel] * B_GROUP, axis=0)
            pblk = posf_ref[pl.ds(off, kb_w), :]
            if own_chunk:
                ahead = jnp.maximum((off + lane_k) - qpos, 0).astype(F32)
                ahead4 = jnp.concatenate([ahead] * B_GROUP, axis=0)
            for n in range(B_KV_HEADS):
                kaug = jnp.where(own_half[n], kblk, pblk)
                s = _dot_nt(qaug[n], kaug, preferred_element_type=F32)
                if own_chunk:
                    s = s - slope2[n] * ahead4
                s = jnp.where(sel4, s, NEG_BIG)
                s_ref[n, :, pl.ds(off, kb_w)] = s
                macc_ref[n] = jnp.maximum(macc_ref[n], fold_lanes(s, jnp.maximum))

        for_blocks(score_pass, *k_blocks)

        for n in range(B_KV_HEADS):
            m = jnp.max(macc_ref[n], axis=1, keepdims=True)
            macc_ref[n] = jnp.broadcast_to(m, (rows, 128))

        ones_blk = jnp.ones((kb_w, 128), BF16)

        def value_pass(off, vblk, _):
            vaug = jnp.concatenate([vblk, ones_blk], axis=1)
            for n in range(B_KV_HEADS):
                m_b = macc_ref[n]
                p = jnp.exp(s_ref[n, :, pl.ds(off, kb_w)]
                            - jnp.concatenate([m_b] * (kb_w // 128), axis=1))
                oacc_ref[n] = oacc_ref[n] + jnp.dot(p.astype(BF16), vaug, preferred_element_type=F32)

        for_blocks(value_pass, *v_blocks)
        pieces = []
        for n in range(B_KV_HEADS):
            acc = oacc_ref[n]
            o_n = acc[:, n * B_HEAD_DIM:(n + 1) * B_HEAD_DIM] / acc[:, 128:129]
            pieces += [o_n[g * th:(g + 1) * th] for g in range(B_GROUP)]
        o_ref[pl.ds(r0, th), :] = jnp.concatenate(pieces, axis=1)

    for_parts(attend_part)


def _dsa(proj, bsz, t_len, cache=None):
    kb_w = KEY_BLOCK
    pos0 = 0 if cache is None else cache[0].shape[1]
    tq = next(c for c in (8 * CHUNK, 4 * CHUNK, 2 * CHUNK, CHUNK) if t_len % c == 0 and pos0 % c == 0)
    th = min(tq, 2 * CHUNK)
    nq = t_len // tq
    n_cache = pos0 // kb_w
    s_pad = pos0 + kb_w if cache is not None else t_len
    assert s_pad % kb_w == 0 and pos0 % kb_w == 0 and kb_w % tq == 0
    assert cache is None or t_len == tq
    rows = B_GROUP * th
    proj3 = proj.reshape(bsz, t_len, N_PAD)
    key_cols = (C_KD // 128, C_VD // 128, C_KIWI // 128)
    if cache is None:
        key_specs = [pl.BlockSpec((1, s_pad, 128), functools.partial(lambda b, t, c: (b, 0, c), c=c))
                     for c in key_cols]
        key_args = (proj3, proj3, proj3)
    else:
        key_specs = [pl.BlockSpec((1, tq, 128), functools.partial(lambda b, t, c: (b, t, c), c=c))
                     for c in key_cols]
        key_specs += [pl.BlockSpec((1, pos0, a.shape[2]), lambda b, t: (b, 0, 0)) for a in cache]
        key_args = (proj3, proj3, proj3) + tuple(cache)
    kpos = lax.broadcasted_iota(I32, (s_pad, 128), 0)
    feat = lax.broadcasted_iota(I32, (s_pad, 128), 1)
    feat = feat % B_HEAD_DIM
    posf = jnp.where(feat == 0, kpos // CHUNK, jnp.where(feat == 1, kpos % CHUNK,
                     jnp.where(feat < 4, 1, 0))).astype(BF16)
    return pl.pallas_call(
        functools.partial(_dsa_kernel, pos0=pos0, kb_w=kb_w, tq=tq, th=th, n_cache=n_cache,
                          topk=min(MAX_TOPK, (pos0 + t_len) // 4)),
        grid=(bsz, nq),
        in_specs=[
            pl.BlockSpec((tq, 512), lambda b, t: (b * nq + t, C_Q // 512)),
            pl.BlockSpec((tq, 256), lambda b, t: (b * nq + t, C_QI // 256)),
            pl.BlockSpec((tq, 128), lambda b, t: (b * nq + t, C_KIWI // 128)),
            *key_specs,
            pl.BlockSpec((s_pad, 128), lambda b, t: (0, 0)),
        ],
        out_specs=pl.BlockSpec((tq, B_WIDTH), lambda b, t: (b * nq + t, 0)),
        out_shape=jax.ShapeDtypeStruct((bsz * t_len, B_WIDTH), F32),
        scratch_shapes=[
            pltpu.VMEM((tq, s_pad), F32),
            pltpu.VMEM((B_KV_HEADS, rows, s_pad), F32),
            pltpu.VMEM((B_KV_HEADS, rows, 128), F32),
            pltpu.VMEM((B_KV_HEADS, rows, 256), F32),
            pltpu.VMEM((tq, 128), F32),
        ],
        compiler_params=pltpu.CompilerParams(
            dimension_semantics=("arbitrary", "arbitrary"), vmem_limit_bytes=VMEM_LIMIT),
        name="dsa",
    )(proj, proj, proj, *key_args, posf)


def _merge_kernel(x_ref, ya_ref, yb_ref, gd_ref, gab_ref, wpa_ref, wpb_ref, wo_ref, fnw_ref, o_ref):
    gd = gd_ref[...]
    yb = yb_ref[...] * (gd * _sigmoid(gd))
    pa = jnp.dot(ya_ref[...].astype(BF16), wpa_ref[...], preferred_element_type=F32)
    pb = jnp.dot(yb.astype(BF16), wpb_ref[...], preferred_element_type=F32)
    merged = _sigmoid(gab_ref[:, 0:D_MODEL]) * pa + _sigmoid(gab_ref[:, D_MODEL:2 * D_MODEL]) * pb
    out = x_ref[...] + jnp.dot(merged.astype(BF16), wo_ref[...], preferred_element_type=F32)
    ms = jnp.mean(out * out, axis=-1, keepdims=True)
    o_ref[...] = (out * lax.rsqrt(ms + NORM_EPS)) * fnw_ref[...]


def _merge(x2d, ya, yb, proj, w_pa, w_pb, w_o, final_w):
    m = x2d.shape[0]
    tm = min(512, m)
    full = lambda shape: pl.BlockSpec(shape, lambda i: (0, 0))
    return pl.pallas_call(
        _merge_kernel,
        grid=(m // tm,),
        in_specs=[
            pl.BlockSpec((tm, D_MODEL), lambda i: (i, 0)),
            pl.BlockSpec((tm, A_WIDTH), lambda i: (i, 0)),
            pl.BlockSpec((tm, B_WIDTH), lambda i: (i, 0)),
            pl.BlockSpec((tm, 512), lambda i: (i, C_GD // 512)),
            pl.BlockSpec((tm, 2048), lambda i: (i, C_GATES // 2048)),
            full((A_WIDTH, D_MODEL)), full((B_WIDTH, D_MODEL)), full((D_MODEL, D_MODEL)),
            full((1, D_MODEL)),
        ],
        out_specs=pl.BlockSpec((tm, D_MODEL), lambda i: (i, 0)),
        out_shape=jax.ShapeDtypeStruct((m, D_MODEL), F32),
        compiler_params=pltpu.CompilerParams(
            dimension_semantics=("arbitrary",), vmem_limit_bytes=VMEM_LIMIT),
        name="merge",
    )(x2d, ya, yb, proj, proj, w_pa.astype(BF16), w_pb.astype(BF16), w_o.astype(BF16),
      final_w.reshape(1, D_MODEL))


def _rwkv_order(row):
    return row[..., 0:2048], row[..., 2048:2176]


def _mixer(x, shift_prev, wkv_prev, past_k, past_v, past_ki, w_perm, norm_w, prm, w_pa, w_pb, w_o,
           final_w):
    bsz, t_len, _ = x.shape
    x2d = x.reshape(bsz * t_len, D_MODEL)
    proj = _proj(x2d, norm_w, w_perm)
    proj3 = proj.reshape(bsz, t_len, N_PAD)

    shift4, shiftw = _rwkv_order(shift_prev)
    ya, wkv_new = _rwkv(proj, bsz, t_len, shift4, shiftw, wkv_prev, prm)

    k_new = proj3[:, :, C_KD:C_KD + 128]
    v_new = proj3[:, :, C_VD:C_VD + 128]
    ki_new = proj3[:, :, C_KIWI:C_KIWI + IDX_DIM]
    past_len = 0 if past_k is None else past_k.shape[1]
    if past_len == 0:
        yb = _dsa(proj, bsz, t_len)
    else:
        yb = _dsa(proj, bsz, t_len, cache=(past_k.reshape(bsz, past_len, 128),
                                           past_v.reshape(bsz, past_len, 128), past_ki))

    y = _merge(x2d, ya, yb, proj, w_pa, w_pb, w_o, final_w).reshape(bsz, t_len, D_MODEL)
    last = proj3[:, t_len - 1:t_len, :]
    shift_new = jnp.concatenate([last[..., C_RKVG:C_RKVG + 2048], last[..., C_WDAD:C_WDAD + 128]], axis=-1)
    kv_shape = (bsz, t_len, B_KV_HEADS, B_HEAD_DIM)
    return y, k_new.reshape(kv_shape), v_new.reshape(kv_shape), ki_new, wkv_new, shift_new


def kernel(x_prompt, x_sample, cache_k, cache_v, cache_kidx, state_wkv, state_shift, norm_w, w_in,
           shift_mu, decay_w0, decay_up, iclr_a0, iclr_up, k_k, k_a, r_k, gn_w, gn_b, w_pa, w_pb,
           w_o, final_norm_w):
    assert w_in.shape[0] == 1, "the final norm is fused into the (single) layer's merge kernel"
    bp = x_prompt.shape[0]
    w_perm = _permute_w_in(w_in[0])
    mu4, muw = _rwkv_order(shift_mu[0].reshape(1, RWKV_COLS))
    row = lambda a: a.reshape(1, A_WIDTH)
    prm = dict(mu4=mu4, muw=muw, w0=row(decay_w0[0]), dup=decay_up[0], a0=row(iclr_a0[0]),
               aup=iclr_up[0], kk=row(k_k[0]), ka=row(k_a[0]), rk=row(r_k[0]), gnw=row(gn_w[0]),
               gnb=row(gn_b[0]))
    common = (w_perm, norm_w[0], prm, w_pa[0], w_pb[0], w_o[0], final_norm_w)
    yp, kp, vp, kip, wkvp, shp = _mixer(
        x_prompt, jnp.zeros((bp, 1, RWKV_COLS), F32),
        jnp.zeros((bp, A_HEADS, A_HEAD_DIM, A_HEAD_DIM), F32), None, None, None, *common)
    ys, ks, vs, kis, wkvs, shs = _mixer(
        x_sample, state_shift[0], state_wkv[0], cache_k[0], cache_v[0], cache_kidx[0], *common)
    st = lambda a: a[None]
    return (yp, ys, st(kp), st(vp), st(kip), st(wkvp), st(shp),
            st(ks), st(vs), st(kis), st(wkvs), st(shs))
```

```python
import functools

import jax
import jax.numpy as jnp
from jax import lax
from jax.experimental import pallas as pl
from jax.experimental.pallas import tpu as pltpu

F32 = jnp.float32
BF16 = jnp.bfloat16
I32 = jnp.int32
HIGHEST = lax.Precision.HIGHEST

D_MODEL = 1024
CHUNK = 64
A_HEADS = 8
A_HEAD_DIM = 64
A_WIDTH = 512
LORA = 64
RWKV_COLS = 4 * A_WIDTH + 2 * LORA
B_HEADS = 8
B_KV_HEADS = 2
B_GROUP = 4
B_HEAD_DIM = 64
B_WIDTH = 512
B_KV_WIDTH = 128
IDX_HEADS = 8
IDX_DIM = 32
MAX_TOPK = 256
DSA_COLS = 1576
N_IN = 5800
NORM_EPS = 1e-6
GN_EPS = 64e-5

C_GATES = 0
C_RKVG = 2048
C_Q = 4096
C_GD = 4608
C_QI = 5120
C_KD = 5376
C_VD = 5504
C_KIWI = 5632
C_WDAD = 5760
N_PAD = 5888

KEY_BLOCK = 512
BLOCK_GROUPS = (4, 2, 1)
INT_MIN = -(2 ** 31)
LOWEST_FINITE_KEY = INT_MIN + 0x00800000
NEG_BIG = -1e30
VMEM_LIMIT = 56 * 1024 * 1024


def _sigmoid(x):
    return 1.0 / (1.0 + jnp.exp(-x))


def _dot_nt(a, b, **kw):
    return lax.dot_general(a, b, (((1,), (1,)), ((), ())), **kw)


def _dot_tn(a, b, **kw):
    return lax.dot_general(a, b, (((0,), (0,)), ((), ())), **kw)


def _permute_w_in(w):
    d0 = RWKV_COLS
    g0 = RWKV_COLS + DSA_COLS
    cols = [
        w[:, g0:g0 + 2048],
        w[:, 0:2048],
        w[:, d0:d0 + 512],
        w[:, d0 + 1064:d0 + 1576],
        w[:, d0 + 768:d0 + 1024],
        w[:, d0 + 512:d0 + 640],
        w[:, d0 + 640:d0 + 768],
        w[:, d0 + 1024:d0 + 1064],
        jnp.zeros((w.shape[0], 128 - IDX_DIM - IDX_HEADS), w.dtype),
        w[:, 2048:2176],
    ]
    return jnp.concatenate(cols, axis=1).astype(BF16)


def _proj_kernel(x_ref, nw_ref, w_ref, o_ref):
    x = x_ref[...]
    ms = jnp.mean(x * x, axis=-1, keepdims=True)
    h = (x * lax.rsqrt(ms + NORM_EPS)) * nw_ref[...]
    o_ref[...] = jnp.dot(h.astype(BF16), w_ref[...], preferred_element_type=F32)


def _proj(x2d, norm_w, w_perm):
    m = x2d.shape[0]
    tm = min(1024, m)
    tn = N_PAD // 2
    return pl.pallas_call(
        _proj_kernel,
        grid=(N_PAD // tn, m // tm),
        in_specs=[
            pl.BlockSpec((tm, D_MODEL), lambda j, i: (i, 0)),
            pl.BlockSpec((1, D_MODEL), lambda j, i: (0, 0)),
            pl.BlockSpec((D_MODEL, tn), lambda j, i: (0, j)),
        ],
        out_specs=pl.BlockSpec((tm, tn), lambda j, i: (i, j)),
        out_shape=jax.ShapeDtypeStruct((m, N_PAD), F32),
        compiler_params=pltpu.CompilerParams(
            dimension_semantics=("arbitrary", "arbitrary"), vmem_limit_bytes=VMEM_LIMIT),
        name="proj",
    )(x2d, norm_w.reshape(1, D_MODEL), w_perm)


def _split_bf16(x, terms):
    pieces = []
    for _ in range(terms):
        piece = x.astype(BF16)
        pieces.append(piece)
        x = x - piece.astype(F32)
    return pieces


def _dot_exact_rhs(a, b_bf16, terms):
    acc = None
    for piece in _split_bf16(a, terms):
        d = jnp.dot(piece, b_bf16, preferred_element_type=F32)
        acc = d if acc is None else acc + d
    return acc


def _dot_3pass(a, b):
    ah, al = _split_bf16(a, 2)
    bh, bl = _split_bf16(b, 2)
    dot = functools.partial(jnp.dot, preferred_element_type=F32)
    return dot(ah, bh) + dot(ah, bl) + dot(al, bh)


def _rwkv_kernel(p4_ref, pw_ref, sp4_ref, spw_ref, s0_ref, mu4_ref, muw_ref, w0_ref, dup_ref,
                 a0_ref, aup_ref, kk_ref, ka_ref, rk_ref, gnw_ref, gnb_ref, bd_ref, tri_ref,
                 ya_ref, sout_ref, c4_ref, cw_ref, sbd_ref, *, nch):
    c = CHUNK
    n = A_HEAD_DIM
    rows = nch * c
    n_pairs = A_HEADS // 2
    t_idx = pl.program_id(1)
    lane = lax.broadcasted_iota(I32, (c, 2 * n), 1)
    lo_half = lane < n
    row_c = lax.broadcasted_iota(I32, (c, 2 * n), 0)
    pos_in_head = jnp.where(lo_half, lane, lane - n)
    tri_strict = (pos_in_head < row_c).astype(F32)
    tri_incl = (pos_in_head <= row_c).astype(F32)
    eye2 = (pos_in_head == row_c).astype(F32)
    lane_sq = lax.broadcasted_iota(I32, (2 * n, 2 * n), 1)
    row_sq = lax.broadcasted_iota(I32, (2 * n, 2 * n), 0)
    same_head = (lane_sq < n) == (row_sq < n)
    dot = functools.partial(jnp.dot, preferred_element_type=F32)

    def bdiag(x):
        zero = jnp.zeros_like(x)
        return jnp.concatenate([jnp.where(lo_half, x, zero), jnp.where(lo_half, zero, x)], axis=0)

    @pl.when(t_idx == 0)
    def _():
        c4_ref[...] = sp4_ref[0]
        cw_ref[...] = spw_ref[0]
        zeros = jnp.zeros((n, n), F32)
        for j in range(n_pairs):
            sbd_ref[j] = jnp.concatenate(
                [jnp.concatenate([s0_ref[0, 2 * j], zeros], axis=1),
                 jnp.concatenate([zeros, s0_ref[0, 2 * j + 1]], axis=1)], axis=0)

    p4 = p4_ref[...]
    pw = pw_ref[...]
    row = lax.broadcasted_iota(I32, (rows, 1), 0)
    prev4 = jnp.where(row == 0, c4_ref[...], pltpu.roll(p4, 1, 0))
    prevw = jnp.where(row == 0, cw_ref[...], pltpu.roll(pw, 1, 0))
    c4_ref[...] = p4[rows - 1:rows, :]
    cw_ref[...] = pw[rows - 1:rows, :]
    ps4 = p4 + (prev4 - p4) * mu4_ref[...]
    psw = pw + (prevw - pw) * muw_ref[...]
    r = ps4[:, 0:512]
    k = ps4[:, 512:1024]
    v = ps4[:, 1024:1536]
    g = ps4[:, 1536:2048]
    wd = psw[:, 0:LORA]
    ad = psw[:, LORA:2 * LORA]
    bd = bd_ref[...]

    xw = w0_ref[...] + _dot_3pass(jnp.tanh(wd), dup_ref[...])
    z = -xw
    softplus = jnp.maximum(z, 0.0) + jnp.log(1.0 + jnp.exp(-jnp.abs(z)))
    lw = -jnp.exp(-softplus - 0.5)
    a = _sigmoid(a0_ref[...] + _dot_3pass(ad, aup_ref[...]))
    kkr = k * kk_ref[...]
    kkn = kkr / jnp.maximum(jnp.sqrt(_dot_exact_rhs(kkr * kkr, bd, 2)), 1e-12)
    kmod = k * (1.0 + (a - 1.0) * ka_ref[...])

    cum = None
    for piece in _split_bf16(lw, 3):
        d = dot(tri_ref[...], piece)
        cum = d if cum is None else cum + d
    pdec = jnp.exp(cum)
    pinv = jnp.exp(-cum)
    rt = (r * pdec).astype(BF16)
    at = (-kkn * jnp.exp(cum - lw)).astype(BF16)
    bt = (kkn * a * pinv).astype(BF16)
    kt = (kmod * pinv).astype(BF16)
    vb = v.astype(BF16)

    tiles = [(ci, j) for ci in range(nch) for j in range(n_pairs)]
    rsl = lambda ci: slice(ci * c, (ci + 1) * c)
    lsl = lambda j: slice(j * 2 * n, (j + 1) * 2 * n)
    lhs, bk, a_ak_rk, a_rb, xs, tinv = {}, {}, {}, {}, {}, {}
    for ci, j in tiles:
        rs, ls = rsl(ci), lsl(j)
        lhs[ci, j] = jnp.concatenate([at[rs, ls], rt[rs, ls]], axis=0)
        bk[ci, j] = jnp.concatenate([bt[rs, ls], kt[rs, ls]], axis=0)
        amat = _dot_nt(lhs[ci, j], jnp.concatenate([bdiag(bt[rs, ls]), bdiag(kt[rs, ls])], axis=0),
                       preferred_element_type=F32)
        xs[ci, j] = amat[0:c, 0:2 * n] * tri_strict
        a_ak_rk[ci, j] = jnp.concatenate([amat[0:c, 2 * n:4 * n] * tri_strict,
                                          amat[c:2 * c, 2 * n:4 * n] * tri_incl], axis=0).astype(BF16)
        a_rb[ci, j] = (amat[c:2 * c, 0:2 * n] * tri_incl).astype(BF16)
        tinv[ci, j] = eye2 + xs[ci, j]
    for t in tiles:
        xb = xs[t].astype(BF16)
        xs[t] = dot(xb, bdiag(xb))
    for _ in range(4):
        for t in tiles:
            xb = xs[t].astype(BF16)
            both = dot(xb, jnp.concatenate([bdiag(tinv[t].astype(BF16)), bdiag(xb)], axis=1))
            tinv[t] = tinv[t] + both[:, 0:2 * n]
            xs[t] = both[:, 2 * n:4 * n]
    for t in tiles:
        tinv[t] = tinv[t] + dot(xs[t].astype(BF16), bdiag(tinv[t].astype(BF16)))
    akv = {}
    for ci, j in tiles:
        tinv[ci, j] = tinv[ci, j].astype(BF16)
        akv[ci, j] = dot(a_ak_rk[ci, j], bdiag(vb[rsl(ci), lsl(j)]))

    pairs = range(n_pairs)
    s_pair = [sbd_ref[j] for j in pairs]
    y_chunks = []
    for ci in range(nch):
        from_state = [_dot_nt(lhs[ci, j], s_pair[j].astype(BF16), preferred_element_type=F32)
                      for j in pairs]
        u = [dot(tinv[ci, j], bdiag((from_state[j][0:c] + akv[ci, j][0:c]).astype(BF16)))
             for j in pairs]
        uv_t = [jnp.transpose(jnp.concatenate([u[j], v[rsl(ci), lsl(j)]], axis=0)).astype(BF16)
                for j in pairs]
        upd = [dot(uv_t[j], bk[ci, j]) for j in pairs]
        s_pair = [(s_pair[j] + jnp.where(same_head, upd[j], 0.0))
                  * pdec[(ci + 1) * c - 1:(ci + 1) * c, lsl(j)] for j in pairs]
        y_chunks.append(jnp.concatenate(
            [from_state[j][c:2 * c] + akv[ci, j][c:2 * c] + dot(a_rb[ci, j], bdiag(u[j].astype(BF16)))
             for j in pairs], axis=1))
    for j in pairs:
        sbd_ref[j] = s_pair[j]
    y = y_chunks[0] if nch == 1 else jnp.concatenate(y_chunks, axis=0)

    inv_n = 1.0 / n
    mean = _dot_exact_rhs(y, bd, 2) * inv_n
    dlt = y - mean
    var = _dot_exact_rhs(dlt * dlt, bd, 2) * inv_n
    yn = dlt * lax.rsqrt(var + GN_EPS) * gnw_ref[...] + gnb_ref[...]
    yn = yn + _dot_exact_rhs(r * kmod * rk_ref[...], bd, 2) * v
    ya_ref[...] = (yn * (g * _sigmoid(g))).astype(BF16)

    @pl.when(t_idx == pl.num_programs(1) - 1)
    def _():
        for j in range(n_pairs):
            s_pair = sbd_ref[j]
            sout_ref[0, 2 * j] = s_pair[0:n, 0:n]
            sout_ref[0, 2 * j + 1] = s_pair[n:2 * n, n:2 * n]


def _rwkv(proj, bsz, t_len, shift4, shiftw, wkv_prev, prm):
    nch = min(4, t_len // CHUNK)
    rows = nch * CHUNK
    nt = t_len // rows
    row1 = lambda width: pl.BlockSpec((1, width), lambda b, t: (0, 0))
    head = lax.broadcasted_iota(I32, (A_WIDTH, A_WIDTH), 0) // A_HEAD_DIM
    bd = (head == head.T).astype(BF16)
    ti = lax.broadcasted_iota(I32, (rows, rows), 0)
    si = lax.broadcasted_iota(I32, (rows, rows), 1)
    tri = ((ti // CHUNK == si // CHUNK) & (si <= ti)).astype(BF16)
    return pl.pallas_call(
        functools.partial(_rwkv_kernel, nch=nch),
        grid=(bsz, nt),
        in_specs=[
            pl.BlockSpec((rows, 2048), lambda b, t: (b * nt + t, C_RKVG // 2048)),
            pl.BlockSpec((rows, 128), lambda b, t: (b * nt + t, C_WDAD // 128)),
            pl.BlockSpec((1, 1, 2048), lambda b, t: (b, 0, 0)),
            pl.BlockSpec((1, 1, 128), lambda b, t: (b, 0, 0)),
            pl.BlockSpec((1, A_HEADS, A_HEAD_DIM, A_HEAD_DIM), lambda b, t: (b, 0, 0, 0)),
            row1(2048), row1(128), row1(A_WIDTH),
            pl.BlockSpec((LORA, A_WIDTH), lambda b, t: (0, 0)),
            row1(A_WIDTH),
            pl.BlockSpec((LORA, A_WIDTH), lambda b, t: (0, 0)),
            row1(A_WIDTH), row1(A_WIDTH), row1(A_WIDTH), row1(A_WIDTH), row1(A_WIDTH),
            pl.BlockSpec((A_WIDTH, A_WIDTH), lambda b, t: (0, 0)),
            pl.BlockSpec((rows, rows), lambda b, t: (0, 0)),
        ],
        out_specs=[
            pl.BlockSpec((rows, A_WIDTH), lambda b, t: (b * nt + t, 0)),
            pl.BlockSpec((1, A_HEADS, A_HEAD_DIM, A_HEAD_DIM), lambda b, t: (b, 0, 0, 0)),
        ],
        out_shape=[
            jax.ShapeDtypeStruct((bsz * t_len, A_WIDTH), BF16),
            jax.ShapeDtypeStruct((bsz, A_HEADS, A_HEAD_DIM, A_HEAD_DIM), F32),
        ],
        scratch_shapes=[pltpu.VMEM((1, 2048), F32), pltpu.VMEM((1, 128), F32),
                        pltpu.VMEM((A_HEADS // 2, 2 * A_HEAD_DIM, 2 * A_HEAD_DIM), F32)],
        compiler_params=pltpu.CompilerParams(
            dimension_semantics=("arbitrary", "arbitrary"), vmem_limit_bytes=VMEM_LIMIT),
        name="rwkv",
    )(proj, proj, shift4, shiftw, wkv_prev, prm["mu4"], prm["muw"], prm["w0"], prm["dup"],
      prm["a0"], prm["aup"], prm["kk"], prm["ka"], prm["rk"], prm["gnw"], prm["gnb"], bd, tri)


def _dsa_kernel(q_ref, gd_ref, qi_ref, kiwi_ref, k_ref, v_ref, ki_ref, *rest, pos0, kb_w, topk, tq, th,
                n_cache):
    if n_cache:
        kc_ref, vc_ref, kic_ref, *rest = rest
    posf_ref, o_ref, keys_ref, s_ref, macc_ref, oacc_ref, thr_ref = rest
    qt = pl.program_id(1)
    tile_pos = pos0 + qt * tq
    row_chunk = jnp.right_shift(lax.broadcasted_iota(I32, (tq, 1), 0), 6)
    n_adm = tile_pos + (row_chunk + 1) * CHUNK
    nkb = n_cache + 1 if n_cache else (tile_pos + tq + kb_w - 1) // kb_w
    idx_scale = (IDX_HEADS ** -0.5) * (IDX_DIM ** -0.5)
    att_scale = B_HEAD_DIM ** -0.5
    topk = float(topk)
    lane_q128 = lax.broadcasted_iota(I32, (1, 128), 1)
    lane_k = lax.broadcasted_iota(I32, (1, kb_w), 1)

    def loaders(all_ref, cache_ref):
        if n_cache:
            def new_rows():
                new = all_ref[0].astype(BF16)
                return jnp.concatenate([new, jnp.zeros((kb_w - tq, new.shape[1]), BF16)], axis=0)
            return (lambda off: cache_ref[0, pl.ds(off, kb_w), :].astype(BF16)), new_rows
        at = lambda off: all_ref[0, pl.ds(off, kb_w), :].astype(BF16)
        return at, (lambda: at(last_off))

    last_off = n_cache * kb_w if n_cache else pl.multiple_of((nkb - 1) * kb_w, kb_w)

    def for_blocks(body, block_at, last_block):
        def group_step(width, first):
            def step(j, carry):
                offs = [pl.multiple_of((first + width * j + i) * kb_w, kb_w) for i in range(width)]
                blocks = [block_at(off) for off in offs]
                for off, block in zip(offs, blocks):
                    body(off, block, False)
                return carry
            return step

        done = 0
        for width in BLOCK_GROUPS:
            trips = (nkb - 1 - done) // width
            lax.fori_loop(0, trips, group_step(width, done), 0)
            done = done + trips * width
        body(last_off, last_block(), True)

    def for_parts(fn):
        def step(p, carry):
            fn(pl.multiple_of(p * th, th))
            return carry

        lax.fori_loop(0, tq // th, step, 0)

    def part_chunk(r0):
        return jnp.right_shift(r0 + lax.broadcasted_iota(I32, (th, 1), 0), 6)

    k_blocks = loaders(k_ref, kc_ref if n_cache else None)
    v_blocks = loaders(v_ref, vc_ref if n_cache else None)
    ki_blocks = loaders(ki_ref, kic_ref if n_cache else None)

    def index_part(r0):
        qi = qi_ref[pl.ds(r0, th), :]
        wi = kiwi_ref[pl.ds(r0, th), IDX_DIM:IDX_DIM + IDX_HEADS]
        heads = []
        for h in range(IDX_HEADS):
            slab = qi[:, 128 * (h // 4):128 * (h // 4) + 128]
            if h % 4:
                slab = pltpu.roll(slab, 128 - IDX_DIM * (h % 4), 1)
            heads.append(jnp.where(lane_q128 < IDX_DIM, slab, 0.0))
        qis = jnp.concatenate(heads, axis=0).astype(BF16)
        wis = jnp.concatenate([wi[:, h:h + 1] for h in range(IDX_HEADS)], axis=0)
        n_adm_part = tile_pos + (part_chunk(r0) + 1) * CHUNK

        def score_block(off, kir, _):
            s = _dot_nt(qis[:, 0:kir.shape[1]], kir, preferred_element_type=F32)
            s = jnp.maximum(s, 0.0) * wis
            isc = s[0:th]
            for h in range(1, IDX_HEADS):
                isc = isc + s[h * th:(h + 1) * th]
            isc = isc * idx_scale
            keys_ref[pl.ds(r0, th), pl.ds(off, kb_w)] = jnp.where(off + lane_k < n_adm_part, isc, -jnp.inf)

        for_blocks(score_block, *ki_blocks)

    for_parts(index_part)

    def key_to_score(key):
        return pltpu.bitcast(jnp.where(key < 0, key ^ 0x7FFFFFFF, key), F32)

    def count_ge(cand_key):
        accs = []
        for r0 in range(0, tq, th):
            cand_part = key_to_score(cand_key[r0:r0 + th])

            def body(kb, acc, r0=r0, cand_part=cand_part):
                off = pl.multiple_of(kb * kb_w, kb_w)
                kblk = keys_ref[r0:r0 + th, pl.ds(off, kb_w)]
                for j in range(kb_w // 128):
                    acc = acc + jnp.where(kblk[:, j * 128:(j + 1) * 128] >= cand_part, 1.0, 0.0)
                return acc

            accs.append(lax.fori_loop(0, nkb, body, jnp.zeros((th, 128), F32)))
        acc = accs[0] if len(accs) == 1 else jnp.concatenate(accs, axis=0)
        return jnp.sum(acc, axis=1, keepdims=True)

    c0 = count_ge(jnp.zeros((tq, 128), I32))
    t0 = jnp.where(c0 >= topk, jnp.zeros((tq, 128), I32), jnp.full((tq, 128), INT_MIN, I32))
    n0 = jnp.where(c0 >= topk, c0, n_adm.astype(F32))

    def bit_step(i, carry):
        t, n_t = carry
        cand = t | jnp.left_shift(jnp.int32(1), 30 - i)
        n_cand = count_ge(cand)
        take = n_cand >= topk
        return jnp.where(take, cand, t), jnp.where(take, n_cand, n_t)

    thr_key, n_ge = lax.fori_loop(0, 31, bit_step, (t0, n0))
    thr_key = jnp.maximum(thr_key, LOWEST_FINITE_KEY)
    thr_ref[...] = key_to_score(thr_key)
    thr = thr_ref[:, 0:1]

    @pl.when(jnp.max(n_ge) > topk)
    def _():
        n_tie_take = topk - count_ge(thr_key + 1)
        ri = lax.broadcasted_iota(I32, (kb_w, kb_w), 0)
        ci = lax.broadcasted_iota(I32, (kb_w, kb_w), 1)
        upper = (ri <= ci).astype(BF16)

        def body(kb, seen):
            off = pl.multiple_of(kb * kb_w, kb_w)
            kblk = keys_ref[:, pl.ds(off, kb_w)]
            tie = kblk == thr
            rank = seen + jnp.dot(jnp.where(tie, 1.0, 0.0).astype(BF16), upper,
                                  preferred_element_type=F32)
            keys_ref[:, pl.ds(off, kb_w)] = jnp.where(tie & (rank > n_tie_take), -jnp.inf, kblk)
            return rank[:, kb_w - 1:kb_w]

        lax.fori_loop(0, nkb, body, jnp.zeros((tq, 1), F32))

    rows = B_GROUP * th
    lane = lax.broadcasted_iota(I32, (th, 128), 1)

    def fold_lanes(x, op):
        part = x[:, 0:128]
        for j in range(1, kb_w // 128):
            part = op(part, x[:, j * 128:(j + 1) * 128])
        return part

    def attend_part(r0):
        q = q_ref[pl.ds(r0, th), :]
        thr_part = thr_ref[pl.ds(r0, th), 0:1]
        q_chunk = (tile_pos // CHUNK + part_chunk(r0)).astype(F32)
        q_row = (lax.broadcasted_iota(I32, (th, 128), 0) & (CHUNK - 1)).astype(F32)
        qpos = tile_pos + r0 + lax.broadcasted_iota(I32, (th, 1), 0)
        qaug, slope2, own_half = [], [], []
        for n in range(B_KV_HEADS):
            q_parts, slope_parts = [], []
            keep = (lane >= n * B_HEAD_DIM) & (lane < (n + 1) * B_HEAD_DIM)
            own_half.append((lane_q128 >= n * B_HEAD_DIM) & (lane_q128 < (n + 1) * B_HEAD_DIM))
            pos_lane = lane - (1 - n) * B_HEAD_DIM
            for g in range(B_GROUP):
                h = n * B_GROUP + g
                slope = 2.0 ** (-(8.0 / B_HEADS) * (h + 1))
                slab = q[:, 128 * (h // 2):128 * (h // 2) + 128]
                if h % 2 != n:
                    slab = pltpu.roll(slab, B_HEAD_DIM, 1)
                pos_feat = jnp.where(pos_lane == 0, CHUNK * slope,
                           jnp.where(pos_lane == 1, slope,
                           jnp.where(pos_lane == 2, -CHUNK * slope * q_chunk,
                           jnp.where(pos_lane == 3, -slope * q_row, 0.0))))
                q_parts.append(jnp.where(keep, slab * att_scale, pos_feat))
                slope_parts.append(jnp.full((th, 1), 2.0 * slope, F32))
            qaug.append(jnp.concatenate(q_parts, axis=0).astype(BF16))
            slope2.append(jnp.concatenate(slope_parts, axis=0))

        macc_ref[...] = jnp.full(macc_ref.shape, NEG_BIG, F32)
        oacc_ref[...] = jnp.zeros(oacc_ref.shape, F32)

        def score_pass(off, kblk, own_chunk):
            sel = keys_ref[pl.ds(r0, th), pl.ds(off, kb_w)] >= thr_part
            sel4 = jnp.concatenate([sel] * B_GROUP, axis=0)
            pblk = posf_ref[pl.ds(off, kb_w), :]
            if own_chunk:
                ahead = jnp.maximum((off + lane_k) - qpos, 0).astype(F32)
                ahead4 = jnp.concatenate([ahead] * B_GROUP, axis=0)
            for n in range(B_KV_HEADS):
                kaug = jnp.where(own_half[n], kblk, pblk)
                s = _dot_nt(qaug[n], kaug, preferred_element_type=F32)
                if own_chunk:
                    s = s - slope2[n] * ahead4
                s = jnp.where(sel4, s, NEG_BIG)
                s_ref[n, :, pl.ds(off, kb_w)] = s
                macc_ref[n] = jnp.maximum(macc_ref[n], fold_lanes(s, jnp.maximum))

        for_blocks(score_pass, *k_blocks)

        for n in range(B_KV_HEADS):
            m = jnp.max(macc_ref[n], axis=1, keepdims=True)
            macc_ref[n] = jnp.broadcast_to(m, (rows, 128))

        ones_blk = jnp.ones((kb_w, 128), BF16)

        def value_pass(off, vblk, _):
            vaug = jnp.concatenate([vblk, ones_blk], axis=1)
            for n in range(B_KV_HEADS):
                m_b = macc_ref[n]
                p = jnp.exp(s_ref[n, :, pl.ds(off, kb_w)]
                            - jnp.concatenate([m_b] * (kb_w // 128), axis=1))
                oacc_ref[n] = oacc_ref[n] + jnp.dot(p.astype(BF16), vaug, preferred_element_type=F32)

        for_blocks(value_pass, *v_blocks)
        pieces = []
        for n in range(B_KV_HEADS):
            acc = oacc_ref[n]
            o_n = acc[:, n * B_HEAD_DIM:(n + 1) * B_HEAD_DIM] / acc[:, 128:129]
            pieces += [o_n[g * th:(g + 1) * th] for g in range(B_GROUP)]
        gd = gd_ref[pl.ds(r0, th), :]
        o_ref[pl.ds(r0, th), :] = (jnp.concatenate(pieces, axis=1) * (gd * _sigmoid(gd))).astype(BF16)

    for_parts(attend_part)


def _dsa(proj, bsz, t_len, cache=None):
    kb_w = KEY_BLOCK
    pos0 = 0 if cache is None else cache[0].shape[1]
    tq = next(c for c in (8 * CHUNK, 4 * CHUNK, 2 * CHUNK, CHUNK) if t_len % c == 0 and pos0 % c == 0)
    th = min(tq, 2 * CHUNK)
    nq = t_len // tq
    n_cache = pos0 // kb_w
    s_pad = pos0 + kb_w if cache is not None else t_len
    assert s_pad % kb_w == 0 and pos0 % kb_w == 0 and kb_w % tq == 0
    assert cache is None or t_len == tq
    rows = B_GROUP * th
    proj3 = proj.reshape(bsz, t_len, N_PAD)
    key_cols = (C_KD // 128, C_VD // 128, C_KIWI // 128)
    if cache is None:
        key_specs = [pl.BlockSpec((1, s_pad, 128), functools.partial(lambda b, t, c: (b, 0, c), c=c))
                     for c in key_cols]
        key_args = (proj3, proj3, proj3)
    else:
        key_specs = [pl.BlockSpec((1, tq, 128), functools.partial(lambda b, t, c: (b, t, c), c=c))
                     for c in key_cols]
        key_specs += [pl.BlockSpec((1, pos0, a.shape[2]), lambda b, t: (b, 0, 0)) for a in cache]
        key_args = (proj3, proj3, proj3) + tuple(cache)
    kpos = lax.broadcasted_iota(I32, (s_pad, 128), 0)
    feat = lax.broadcasted_iota(I32, (s_pad, 128), 1)
    feat = feat % B_HEAD_DIM
    posf = jnp.where(feat == 0, kpos // CHUNK, jnp.where(feat == 1, kpos % CHUNK,
                     jnp.where(feat < 4, 1, 0))).astype(BF16)
    return pl.pallas_call(
        functools.partial(_dsa_kernel, pos0=pos0, kb_w=kb_w, tq=tq, th=th, n_cache=n_cache,
                          topk=min(MAX_TOPK, (pos0 + t_len) // 4)),
        grid=(bsz, nq),
        in_specs=[
            pl.BlockSpec((tq, 512), lambda b, t: (b * nq + t, C_Q // 512)),
            pl.BlockSpec((tq, 512), lambda b, t: (b * nq + t, C_GD // 512)),
            pl.BlockSpec((tq, 256), lambda b, t: (b * nq + t, C_QI // 256)),
            pl.BlockSpec((tq, 128), lambda b, t: (b * nq + t, C_KIWI // 128)),
            *key_specs,
            pl.BlockSpec((s_pad, 128), lambda b, t: (0, 0)),
        ],
        out_specs=pl.BlockSpec((tq, B_WIDTH), lambda b, t: (b * nq + t, 0)),
        out_shape=jax.ShapeDtypeStruct((bsz * t_len, B_WIDTH), BF16),
        scratch_shapes=[
            pltpu.VMEM((tq, s_pad), F32),
            pltpu.VMEM((B_KV_HEADS, rows, s_pad), F32),
            pltpu.VMEM((B_KV_HEADS, rows, 128), F32),
            pltpu.VMEM((B_KV_HEADS, rows, 256), F32),
            pltpu.VMEM((tq, 128), F32),
        ],
        compiler_params=pltpu.CompilerParams(
            dimension_semantics=("arbitrary", "arbitrary"), vmem_limit_bytes=VMEM_LIMIT),
        name="dsa",
    )(proj, proj, proj, proj, *key_args, posf)


def _merge_kernel(x_ref, ya_ref, yb_ref, gab_ref, wpa_ref, wpb_ref, wo_ref, fnw_ref, o_ref):
    pa = jnp.dot(ya_ref[...], wpa_ref[...], preferred_element_type=F32)
    pb = jnp.dot(yb_ref[...], wpb_ref[...], preferred_element_type=F32)
    merged = _sigmoid(gab_ref[:, 0:D_MODEL]) * pa + _sigmoid(gab_ref[:, D_MODEL:2 * D_MODEL]) * pb
    out = x_ref[...] + jnp.dot(merged.astype(BF16), wo_ref[...], preferred_element_type=F32)
    ms = jnp.mean(out * out, axis=-1, keepdims=True)
    o_ref[...] = (out * lax.rsqrt(ms + NORM_EPS)) * fnw_ref[...]


def _merge(x2d, ya, yb, proj, w_pa, w_pb, w_o, final_w):
    m = x2d.shape[0]
    tm = min(512, m)
    full = lambda shape: pl.BlockSpec(shape, lambda i: (0, 0))
    return pl.pallas_call(
        _merge_kernel,
        grid=(m // tm,),
        in_specs=[
            pl.BlockSpec((tm, D_MODEL), lambda i: (i, 0)),
            pl.BlockSpec((tm, A_WIDTH), lambda i: (i, 0)),
            pl.BlockSpec((tm, B_WIDTH), lambda i: (i, 0)),
            pl.BlockSpec((tm, 2048), lambda i: (i, C_GATES // 2048)),
            full((A_WIDTH, D_MODEL)), full((B_WIDTH, D_MODEL)), full((D_MODEL, D_MODEL)),
            full((1, D_MODEL)),
        ],
        out_specs=pl.BlockSpec((tm, D_MODEL), lambda i: (i, 0)),
        out_shape=jax.ShapeDtypeStruct((m, D_MODEL), F32),
        compiler_params=pltpu.CompilerParams(
            dimension_semantics=("arbitrary",), vmem_limit_bytes=VMEM_LIMIT),
        name="merge",
    )(x2d, ya, yb, proj, w_pa.astype(BF16), w_pb.astype(BF16), w_o.astype(BF16),
      final_w.reshape(1, D_MODEL))


def _rwkv_order(row):
    return row[..., 0:2048], row[..., 2048:2176]


def _mixer(x, shift_prev, wkv_prev, past_k, past_v, past_ki, w_perm, norm_w, prm, w_pa, w_pb, w_o,
           final_w):
    bsz, t_len, _ = x.shape
    x2d = x.reshape(bsz * t_len, D_MODEL)
    proj = _proj(x2d, norm_w, w_perm)
    proj3 = proj.reshape(bsz, t_len, N_PAD)

    shift4, shiftw = _rwkv_order(shift_prev)
    ya, wkv_new = _rwkv(proj, bsz, t_len, shift4, shiftw, wkv_prev, prm)

    k_new = proj3[:, :, C_KD:C_KD + 128]
    v_new = proj3[:, :, C_VD:C_VD + 128]
    ki_new = proj3[:, :, C_KIWI:C_KIWI + IDX_DIM]
    past_len = 0 if past_k is None else past_k.shape[1]
    if past_len == 0:
        yb = _dsa(proj, bsz, t_len)
    else:
        yb = _dsa(proj, bsz, t_len, cache=(past_k.reshape(bsz, past_len, 128),
                                           past_v.reshape(bsz, past_len, 128), past_ki))

    y = _merge(x2d, ya, yb, proj, w_pa, w_pb, w_o, final_w).reshape(bsz, t_len, D_MODEL)
    last = proj3[:, t_len - 1:t_len, :]
    shift_new = jnp.concatenate([last[..., C_RKVG:C_RKVG + 2048], last[..., C_WDAD:C_WDAD + 128]], axis=-1)
    kv_shape = (bsz, t_len, B_KV_HEADS, B_HEAD_DIM)
    return y, k_new.reshape(kv_shape), v_new.reshape(kv_shape), ki_new, wkv_new, shift_new


def kernel(x_prompt, x_sample, cache_k, cache_v, cache_kidx, state_wkv, state_shift, norm_w, w_in,
           shift_mu, decay_w0, decay_up, iclr_a0, iclr_up, k_k, k_a, r_k, gn_w, gn_b, w_pa, w_pb,
           w_o, final_norm_w):
    assert w_in.shape[0] == 1, "the final norm is fused into the (single) layer's merge kernel"
    bp = x_prompt.shape[0]
    w_perm = _permute_w_in(w_in[0])
    mu4, muw = _rwkv_order(shift_mu[0].reshape(1, RWKV_COLS))
    row = lambda a: a.reshape(1, A_WIDTH)
    prm = dict(mu4=mu4, muw=muw, w0=row(decay_w0[0]), dup=decay_up[0], a0=row(iclr_a0[0]),
               aup=iclr_up[0], kk=row(k_k[0]), ka=row(k_a[0]), rk=row(r_k[0]), gnw=row(gn_w[0]),
               gnb=row(gn_b[0]))
    common = (w_perm, norm_w[0], prm, w_pa[0], w_pb[0], w_o[0], final_norm_w)
    yp, kp, vp, kip, wkvp, shp = _mixer(
        x_prompt, jnp.zeros((bp, 1, RWKV_COLS), F32),
        jnp.zeros((bp, A_HEADS, A_HEAD_DIM, A_HEAD_DIM), F32), None, None, None, *common)
    ys, ks, vs, kis, wkvs, shs = _mixer(
        x_sample, state_shift[0], state_wkv[0], cache_k[0], cache_v[0], cache_kidx[0], *common)
    st = lambda a: a[None]
    return (yp, ys, st(kp), st(vp), st(kip), st(wkvp), st(shp),
            st(ks), st(vs), st(kis), st(wkvs), st(shs))
```

```python
import functools

import jax
import jax.numpy as jnp
from jax import lax
from jax.experimental import pallas as pl
from jax.experimental.pallas import tpu as pltpu

F32 = jnp.float32
BF16 = jnp.bfloat16
I32 = jnp.int32
HIGHEST = lax.Precision.HIGHEST

D_MODEL = 1024
CHUNK = 64
A_HEADS = 8
A_HEAD_DIM = 64
A_WIDTH = 512
LORA = 64
RWKV_COLS = 4 * A_WIDTH + 2 * LORA
B_HEADS = 8
B_KV_HEADS = 2
B_GROUP = 4
B_HEAD_DIM = 64
B_WIDTH = 512
B_KV_WIDTH = 128
IDX_HEADS = 8
IDX_DIM = 32
MAX_TOPK = 256
DSA_COLS = 1576
N_IN = 5800
NORM_EPS = 1e-6
GN_EPS = 64e-5

C_GATES = 0
C_RKVG = 2048
C_Q = 4096
C_GD = 4608
C_QI = 5120
C_KD = 5376
C_VD = 5504
C_KIWI = 5632
C_WDAD = 5760
N_PAD = 5888

KEY_BLOCK = 512
BLOCK_GROUPS = (4, 2, 1)
INT_MIN = -(2 ** 31)
LOWEST_FINITE_KEY = INT_MIN + 0x00800000
NEG_BIG = -1e30
VMEM_LIMIT = 56 * 1024 * 1024


def _sigmoid(x):
    return 1.0 / (1.0 + jnp.exp(-x))


def _dot_nt(a, b, **kw):
    return lax.dot_general(a, b, (((1,), (1,)), ((), ())), **kw)


def _dot_tn(a, b, **kw):
    return lax.dot_general(a, b, (((0,), (0,)), ((), ())), **kw)


def _permute_w_in(w):
    d0 = RWKV_COLS
    g0 = RWKV_COLS + DSA_COLS
    cols = [
        w[:, g0:g0 + 2048],
        w[:, 0:2048],
        w[:, d0:d0 + 512],
        w[:, d0 + 1064:d0 + 1576],
        w[:, d0 + 768:d0 + 1024],
        w[:, d0 + 512:d0 + 640],
        w[:, d0 + 640:d0 + 768],
        w[:, d0 + 1024:d0 + 1064],
        jnp.zeros((w.shape[0], 128 - IDX_DIM - IDX_HEADS), w.dtype),
        w[:, 2048:2176],
    ]
    return jnp.concatenate(cols, axis=1).astype(BF16)


def _proj_kernel(x_ref, nw_ref, w_ref, o_ref):
    x = x_ref[...]
    ms = jnp.mean(x * x, axis=-1, keepdims=True)
    h = (x * lax.rsqrt(ms + NORM_EPS)) * nw_ref[...]
    o_ref[...] = jnp.dot(h.astype(BF16), w_ref[...], preferred_element_type=F32)


def _proj(x2d, norm_w, w_perm):
    m = x2d.shape[0]
    tm = min(1024, m)
    tn = N_PAD // 2
    return pl.pallas_call(
        _proj_kernel,
        grid=(N_PAD // tn, m // tm),
        in_specs=[
            pl.BlockSpec((tm, D_MODEL), lambda j, i: (i, 0)),
            pl.BlockSpec((1, D_MODEL), lambda j, i: (0, 0)),
            pl.BlockSpec((D_MODEL, tn), lambda j, i: (0, j)),
        ],
        out_specs=pl.BlockSpec((tm, tn), lambda j, i: (i, j)),
        out_shape=jax.ShapeDtypeStruct((m, N_PAD), F32),
        compiler_params=pltpu.CompilerParams(
            dimension_semantics=("arbitrary", "arbitrary"), vmem_limit_bytes=VMEM_LIMIT),
        name="proj",
    )(x2d, norm_w.reshape(1, D_MODEL), w_perm)


def _split_bf16(x, terms):
    pieces = []
    for _ in range(terms):
        piece = x.astype(BF16)
        pieces.append(piece)
        x = x - piece.astype(F32)
    return pieces


def _dot_exact_rhs(a, b_bf16, terms):
    acc = None
    for piece in _split_bf16(a, terms):
        d = jnp.dot(piece, b_bf16, preferred_element_type=F32)
        acc = d if acc is None else acc + d
    return acc


def _dot_3pass(a, b):
    ah, al = _split_bf16(a, 2)
    bh, bl = _split_bf16(b, 2)
    dot = functools.partial(jnp.dot, preferred_element_type=F32)
    return dot(ah, bh) + dot(ah, bl) + dot(al, bh)


def _rwkv_kernel(p4_ref, pw_ref, sp4_ref, spw_ref, s0_ref, mu4_ref, muw_ref, w0_ref, dup_ref,
                 a0_ref, aup_ref, kk_ref, ka_ref, rk_ref, gnw_ref, gnb_ref, bd_ref, tri_ref,
                 ya_ref, sout_ref, c4_ref, cw_ref, sbd_ref, *, nch):
    c = CHUNK
    n = A_HEAD_DIM
    rows = nch * c
    n_pairs = A_HEADS // 2
    t_idx = pl.program_id(1)
    lane = lax.broadcasted_iota(I32, (c, 2 * n), 1)
    lo_half = lane < n
    row_c = lax.broadcasted_iota(I32, (c, 2 * n), 0)
    pos_in_head = jnp.where(lo_half, lane, lane - n)
    tri_strict = (pos_in_head < row_c).astype(F32)
    tri_incl = (pos_in_head <= row_c).astype(F32)
    eye2 = (pos_in_head == row_c).astype(F32)
    lane_sq = lax.broadcasted_iota(I32, (2 * n, 2 * n), 1)
    row_sq = lax.broadcasted_iota(I32, (2 * n, 2 * n), 0)
    same_head = (lane_sq < n) == (row_sq < n)
    dot = functools.partial(jnp.dot, preferred_element_type=F32)

    def bdiag(x):
        zero = jnp.zeros_like(x)
        return jnp.concatenate([jnp.where(lo_half, x, zero), jnp.where(lo_half, zero, x)], axis=0)

    @pl.when(t_idx == 0)
    def _():
        c4_ref[...] = sp4_ref[0]
        cw_ref[...] = spw_ref[0]
        zeros = jnp.zeros((n, n), F32)
        for j in range(n_pairs):
            sbd_ref[j] = jnp.concatenate(
                [jnp.concatenate([s0_ref[0, 2 * j], zeros], axis=1),
                 jnp.concatenate([zeros, s0_ref[0, 2 * j + 1]], axis=1)], axis=0)

    p4 = p4_ref[...]
    pw = pw_ref[...]
    row = lax.broadcasted_iota(I32, (rows, 1), 0)
    prev4 = jnp.where(row == 0, c4_ref[...], pltpu.roll(p4, 1, 0))
    prevw = jnp.where(row == 0, cw_ref[...], pltpu.roll(pw, 1, 0))
    c4_ref[...] = p4[rows - 1:rows, :]
    cw_ref[...] = pw[rows - 1:rows, :]
    ps4 = p4 + (prev4 - p4) * mu4_ref[...]
    psw = pw + (prevw - pw) * muw_ref[...]
    r = ps4[:, 0:512]
    k = ps4[:, 512:1024]
    v = ps4[:, 1024:1536]
    g = ps4[:, 1536:2048]
    wd = psw[:, 0:LORA]
    ad = psw[:, LORA:2 * LORA]
    bd = bd_ref[...]

    xw = w0_ref[...] + _dot_3pass(jnp.tanh(wd), dup_ref[...])
    z = -xw
    softplus = jnp.maximum(z, 0.0) + jnp.log(1.0 + jnp.exp(-jnp.abs(z)))
    lw = -jnp.exp(-softplus - 0.5)
    a = _sigmoid(a0_ref[...] + _dot_3pass(ad, aup_ref[...]))
    kkr = k * kk_ref[...]
    kkn = kkr / jnp.maximum(jnp.sqrt(_dot_exact_rhs(kkr * kkr, bd, 2)), 1e-12)
    kmod = k * (1.0 + (a - 1.0) * ka_ref[...])

    cum = None
    for piece in _split_bf16(lw, 3):
        d = dot(tri_ref[...], piece)
        cum = d if cum is None else cum + d
    pdec = jnp.exp(cum)
    pinv = jnp.exp(-cum)
    rt = (r * pdec).astype(BF16)
    at = (-kkn * jnp.exp(cum - lw)).astype(BF16)
    bt = (kkn * a * pinv).astype(BF16)
    kt = (kmod * pinv).astype(BF16)
    vb = v.astype(BF16)

    tiles = [(ci, j) for ci in range(nch) for j in range(n_pairs)]
    rsl = lambda ci: slice(ci * c, (ci + 1) * c)
    lsl = lambda j: slice(j * 2 * n, (j + 1) * 2 * n)
    lhs, bk, a_ak_rk, a_rb, xs, tinv = {}, {}, {}, {}, {}, {}
    for ci, j in tiles:
        rs, ls = rsl(ci), lsl(j)
        lhs[ci, j] = jnp.concatenate([at[rs, ls], rt[rs, ls]], axis=0)
        bk[ci, j] = jnp.concatenate([bt[rs, ls], kt[rs, ls]], axis=0)
        amat = _dot_nt(lhs[ci, j], jnp.concatenate([bdiag(bt[rs, ls]), bdiag(kt[rs, ls])], axis=0),
                       preferred_element_type=F32)
        xs[ci, j] = amat[0:c, 0:2 * n] * tri_strict
        a_ak_rk[ci, j] = jnp.concatenate([amat[0:c, 2 * n:4 * n] * tri_strict,
                                          amat[c:2 * c, 2 * n:4 * n] * tri_incl], axis=0).astype(BF16)
        a_rb[ci, j] = (amat[c:2 * c, 0:2 * n] * tri_incl).astype(BF16)
        tinv[ci, j] = eye2 + xs[ci, j]
    for t in tiles:
        xb = xs[t].astype(BF16)
        xs[t] = dot(xb, bdiag(xb))
    for _ in range(4):
        for t in tiles:
            xb = xs[t].astype(BF16)
            both = dot(xb, jnp.concatenate([bdiag(tinv[t].astype(BF16)), bdiag(xb)], axis=1))
            tinv[t] = tinv[t] + both[:, 0:2 * n]
            xs[t] = both[:, 2 * n:4 * n]
    for t in tiles:
        tinv[t] = tinv[t] + dot(xs[t].astype(BF16), bdiag(tinv[t].astype(BF16)))
    akv = {}
    for ci, j in tiles:
        tinv[ci, j] = tinv[ci, j].astype(BF16)
        akv[ci, j] = dot(a_ak_rk[ci, j], bdiag(vb[rsl(ci), lsl(j)]))

    pairs = range(n_pairs)
    s_pair = [sbd_ref[j] for j in pairs]
    y_chunks = []
    for ci in range(nch):
        from_state = [_dot_nt(lhs[ci, j], s_pair[j].astype(BF16), preferred_element_type=F32)
                      for j in pairs]
        u = [dot(tinv[ci, j], bdiag((from_state[j][0:c] + akv[ci, j][0:c]).astype(BF16)))
             for j in pairs]
        uv_t = [jnp.transpose(jnp.concatenate([u[j], v[rsl(ci), lsl(j)]], axis=0)).astype(BF16)
                for j in pairs]
        upd = [dot(uv_t[j], bk[ci, j]) for j in pairs]
        s_pair = [(s_pair[j] + jnp.where(same_head, upd[j], 0.0))
                  * pdec[(ci + 1) * c - 1:(ci + 1) * c, lsl(j)] for j in pairs]
        y_chunks.append(jnp.concatenate(
            [from_state[j][c:2 * c] + akv[ci, j][c:2 * c] + dot(a_rb[ci, j], bdiag(u[j].astype(BF16)))
             for j in pairs], axis=1))
    for j in pairs:
        sbd_ref[j] = s_pair[j]
    y = y_chunks[0] if nch == 1 else jnp.concatenate(y_chunks, axis=0)

    inv_n = 1.0 / n
    mean = _dot_exact_rhs(y, bd, 2) * inv_n
    dlt = y - mean
    var = _dot_exact_rhs(dlt * dlt, bd, 2) * inv_n
    yn = dlt * lax.rsqrt(var + GN_EPS) * gnw_ref[...] + gnb_ref[...]
    yn = yn + _dot_exact_rhs(r * kmod * rk_ref[...], bd, 2) * v
    ya_ref[...] = (yn * (g * _sigmoid(g))).astype(BF16)

    @pl.when(t_idx == pl.num_programs(1) - 1)
    def _():
        for j in range(n_pairs):
            s_pair = sbd_ref[j]
            sout_ref[0, 2 * j] = s_pair[0:n, 0:n]
            sout_ref[0, 2 * j + 1] = s_pair[n:2 * n, n:2 * n]


def _rwkv(proj, bsz, t_len, shift4, shiftw, wkv_prev, prm):
    nch = min(4, t_len // CHUNK)
    rows = nch * CHUNK
    nt = t_len // rows
    row1 = lambda width: pl.BlockSpec((1, width), lambda b, t: (0, 0))
    head = lax.broadcasted_iota(I32, (A_WIDTH, A_WIDTH), 0) // A_HEAD_DIM
    bd = (head == head.T).astype(BF16)
    ti = lax.broadcasted_iota(I32, (rows, rows), 0)
    si = lax.broadcasted_iota(I32, (rows, rows), 1)
    tri = ((ti // CHUNK == si // CHUNK) & (si <= ti)).astype(BF16)
    return pl.pallas_call(
        functools.partial(_rwkv_kernel, nch=nch),
        grid=(bsz, nt),
        in_specs=[
            pl.BlockSpec((rows, 2048), lambda b, t: (b * nt + t, C_RKVG // 2048)),
            pl.BlockSpec((rows, 128), lambda b, t: (b * nt + t, C_WDAD // 128)),
            pl.BlockSpec((1, 1, 2048), lambda b, t: (b, 0, 0)),
            pl.BlockSpec((1, 1, 128), lambda b, t: (b, 0, 0)),
            pl.BlockSpec((1, A_HEADS, A_HEAD_DIM, A_HEAD_DIM), lambda b, t: (b, 0, 0, 0)),
            row1(2048), row1(128), row1(A_WIDTH),
            pl.BlockSpec((LORA, A_WIDTH), lambda b, t: (0, 0)),
            row1(A_WIDTH),
            pl.BlockSpec((LORA, A_WIDTH), lambda b, t: (0, 0)),
            row1(A_WIDTH), row1(A_WIDTH), row1(A_WIDTH), row1(A_WIDTH), row1(A_WIDTH),
            pl.BlockSpec((A_WIDTH, A_WIDTH), lambda b, t: (0, 0)),
            pl.BlockSpec((rows, rows), lambda b, t: (0, 0)),
        ],
        out_specs=[
            pl.BlockSpec((rows, A_WIDTH), lambda b, t: (b * nt + t, 0)),
            pl.BlockSpec((1, A_HEADS, A_HEAD_DIM, A_HEAD_DIM), lambda b, t: (b, 0, 0, 0)),
        ],
        out_shape=[
            jax.ShapeDtypeStruct((bsz * t_len, A_WIDTH), BF16),
            jax.ShapeDtypeStruct((bsz, A_HEADS, A_HEAD_DIM, A_HEAD_DIM), F32),
        ],
        scratch_shapes=[pltpu.VMEM((1, 2048), F32), pltpu.VMEM((1, 128), F32),
                        pltpu.VMEM((A_HEADS // 2, 2 * A_HEAD_DIM, 2 * A_HEAD_DIM), F32)],
        compiler_params=pltpu.CompilerParams(
            dimension_semantics=("arbitrary", "arbitrary"), vmem_limit_bytes=VMEM_LIMIT),
        name="rwkv",
    )(proj, proj, shift4, shiftw, wkv_prev, prm["mu4"], prm["muw"], prm["w0"], prm["dup"],
      prm["a0"], prm["aup"], prm["kk"], prm["ka"], prm["rk"], prm["gnw"], prm["gnb"], bd, tri)


def _dsa_kernel(q_ref, gd_ref, qi_ref, kiwi_ref, k_ref, v_ref, ki_ref, *rest, pos0, kb_w, topk, tq, th,
                n_cache):
    if n_cache:
        kc_ref, vc_ref, kic_ref, *rest = rest
    posf_ref, o_ref, keys_ref, s_ref, macc_ref, oacc_ref, row_ref = rest
    qt = pl.program_id(1)
    tile_pos = pos0 + qt * tq
    row_chunk = jnp.right_shift(lax.broadcasted_iota(I32, (tq, 1), 0), 6)
    n_adm = tile_pos + (row_chunk + 1) * CHUNK
    nkb = n_cache + 1 if n_cache else (tile_pos + tq + kb_w - 1) // kb_w
    idx_scale = (IDX_HEADS ** -0.5) * (IDX_DIM ** -0.5)
    att_scale = B_HEAD_DIM ** -0.5
    topk = float(topk)
    lane_q128 = lax.broadcasted_iota(I32, (1, 128), 1)
    lane_k = lax.broadcasted_iota(I32, (1, kb_w), 1)

    def loaders(all_ref, cache_ref):
        if n_cache:
            def new_rows():
                new = all_ref[0].astype(BF16)
                return jnp.concatenate([new, jnp.zeros((kb_w - tq, new.shape[1]), BF16)], axis=0)
            return (lambda off: cache_ref[0, pl.ds(off, kb_w), :].astype(BF16)), new_rows
        at = lambda off: all_ref[0, pl.ds(off, kb_w), :].astype(BF16)
        return at, (lambda: at(last_off))

    last_off = n_cache * kb_w if n_cache else pl.multiple_of((nkb - 1) * kb_w, kb_w)

    def for_blocks(body, block_at, last_block):
        def group_step(width, first):
            def step(j, carry):
                offs = [pl.multiple_of((first + width * j + i) * kb_w, kb_w) for i in range(width)]
                blocks = [block_at(off) for off in offs]
                for off, block in zip(offs, blocks):
                    body(off, block, False)
                return carry
            return step

        done = 0
        for width in BLOCK_GROUPS:
            trips = (nkb - 1 - done) // width
            lax.fori_loop(0, trips, group_step(width, done), 0)
            done = done + trips * width
        body(last_off, last_block(), True)

    def for_parts(fn):
        def step(p, carry):
            fn(pl.multiple_of(p * th, th))
            return carry

        lax.fori_loop(0, tq // th, step, 0)

    def part_chunk(r0):
        return jnp.right_shift(r0 + lax.broadcasted_iota(I32, (th, 1), 0), 6)

    k_blocks = loaders(k_ref, kc_ref if n_cache else None)
    v_blocks = loaders(v_ref, vc_ref if n_cache else None)
    ki_blocks = loaders(ki_ref, kic_ref if n_cache else None)

    def index_part(r0):
        qi = qi_ref[pl.ds(r0, th), :]
        wi = kiwi_ref[pl.ds(r0, th), IDX_DIM:IDX_DIM + IDX_HEADS]
        heads = []
        for h in range(IDX_HEADS):
            slab = qi[:, 128 * (h // 4):128 * (h // 4) + 128]
            if h % 4:
                slab = pltpu.roll(slab, 128 - IDX_DIM * (h % 4), 1)
            heads.append(jnp.where(lane_q128 < IDX_DIM, slab, 0.0))
        qis = jnp.concatenate(heads, axis=0).astype(BF16)
        wis = jnp.concatenate([wi[:, h:h + 1] for h in range(IDX_HEADS)], axis=0)
        n_adm_part = tile_pos + (part_chunk(r0) + 1) * CHUNK

        def score_block(off, kir, _):
            s = _dot_nt(qis[:, 0:kir.shape[1]], kir, preferred_element_type=F32)
            s = jnp.maximum(s, 0.0) * wis
            isc = s[0:th]
            for h in range(1, IDX_HEADS):
                isc = isc + s[h * th:(h + 1) * th]
            isc = jnp.where(off + lane_k < n_adm_part, isc * idx_scale, -jnp.inf)
            keys_ref[pl.ds(r0, th), pl.ds(off, kb_w)] = isc
            hits = row_ref[pl.ds(r0, th), :]
            for j in range(kb_w // 128):
                hits = hits + jnp.where(isc[:, j * 128:(j + 1) * 128] >= 0.0, 1.0, 0.0)
            row_ref[pl.ds(r0, th), :] = hits

        row_ref[pl.ds(r0, th), :] = jnp.zeros((th, 128), F32)
        for_blocks(score_block, *ki_blocks)

    for_parts(index_part)

    def key_to_score(key):
        return pltpu.bitcast(jnp.where(key < 0, key ^ 0x7FFFFFFF, key), F32)

    def count_ge(cand_key):
        accs = []
        for r0 in range(0, tq, th):
            cand_part = key_to_score(cand_key[r0:r0 + th])

            def body(kb, acc, r0=r0, cand_part=cand_part):
                off = pl.multiple_of(kb * kb_w, kb_w)
                kblk = keys_ref[r0:r0 + th, pl.ds(off, kb_w)]
                for j in range(kb_w // 128):
                    acc = acc + jnp.where(kblk[:, j * 128:(j + 1) * 128] >= cand_part, 1.0, 0.0)
                return acc

            accs.append(lax.fori_loop(0, nkb, body, jnp.zeros((th, 128), F32)))
        acc = accs[0] if len(accs) == 1 else jnp.concatenate(accs, axis=0)
        return jnp.sum(acc, axis=1, keepdims=True)

    c0 = jnp.sum(row_ref[...], axis=1, keepdims=True)
    t0 = jnp.where(c0 >= topk, jnp.zeros((tq, 128), I32), jnp.full((tq, 128), INT_MIN, I32))
    n0 = jnp.where(c0 >= topk, c0, n_adm.astype(F32))

    def bit_step(i, carry):
        t, n_t = carry
        cand = t | jnp.left_shift(jnp.int32(1), 30 - i)
        n_cand = count_ge(cand)
        take = n_cand >= topk
        return jnp.where(take, cand, t), jnp.where(take, n_cand, n_t)

    thr_key, n_ge = lax.fori_loop(0, 31, bit_step, (t0, n0))
    thr_key = jnp.maximum(thr_key, LOWEST_FINITE_KEY)
    row_ref[...] = key_to_score(thr_key)
    thr = row_ref[:, 0:1]

    @pl.when(jnp.max(n_ge) > topk)
    def _():
        n_tie_take = topk - count_ge(thr_key + 1)
        ri = lax.broadcasted_iota(I32, (kb_w, kb_w), 0)
        ci = lax.broadcasted_iota(I32, (kb_w, kb_w), 1)
        upper = (ri <= ci).astype(BF16)

        def body(kb, seen):
            off = pl.multiple_of(kb * kb_w, kb_w)
            kblk = keys_ref[:, pl.ds(off, kb_w)]
            tie = kblk == thr
            rank = seen + jnp.dot(jnp.where(tie, 1.0, 0.0).astype(BF16), upper,
                                  preferred_element_type=F32)
            keys_ref[:, pl.ds(off, kb_w)] = jnp.where(tie & (rank > n_tie_take), -jnp.inf, kblk)
            return rank[:, kb_w - 1:kb_w]

        lax.fori_loop(0, nkb, body, jnp.zeros((tq, 1), F32))

    rows = B_GROUP * th
    lane = lax.broadcasted_iota(I32, (th, 128), 1)

    def fold_lanes(x, op):
        part = x[:, 0:128]
        for j in range(1, kb_w // 128):
            part = op(part, x[:, j * 128:(j + 1) * 128])
        return part

    def attend_part(r0):
        q = q_ref[pl.ds(r0, th), :]
        thr_part = row_ref[pl.ds(r0, th), 0:1]
        q_chunk = (tile_pos // CHUNK + part_chunk(r0)).astype(F32)
        q_row = (lax.broadcasted_iota(I32, (th, 128), 0) & (CHUNK - 1)).astype(F32)
        qpos = tile_pos + r0 + lax.broadcasted_iota(I32, (th, 1), 0)
        qaug, slope2, own_half = [], [], []
        for n in range(B_KV_HEADS):
            q_parts, slope_parts = [], []
            keep = (lane >= n * B_HEAD_DIM) & (lane < (n + 1) * B_HEAD_DIM)
            own_half.append((lane_q128 >= n * B_HEAD_DIM) & (lane_q128 < (n + 1) * B_HEAD_DIM))
            pos_lane = lane - (1 - n) * B_HEAD_DIM
            for g in range(B_GROUP):
                h = n * B_GROUP + g
                slope = 2.0 ** (-(8.0 / B_HEADS) * (h + 1))
                slab = q[:, 128 * (h // 2):128 * (h // 2) + 128]
                if h % 2 != n:
                    slab = pltpu.roll(slab, B_HEAD_DIM, 1)
                pos_feat = jnp.where(pos_lane == 0, CHUNK * slope,
                           jnp.where(pos_lane == 1, slope,
                           jnp.where(pos_lane == 2, -CHUNK * slope * q_chunk,
                           jnp.where(pos_lane == 3, -slope * q_row, 0.0))))
                q_parts.append(jnp.where(keep, slab * att_scale, pos_feat))
                slope_parts.append(jnp.full((th, 1), 2.0 * slope, F32))
            qaug.append(jnp.concatenate(q_parts, axis=0).astype(BF16))
            slope2.append(jnp.concatenate(slope_parts, axis=0))

        macc_ref[...] = jnp.full(macc_ref.shape, NEG_BIG, F32)
        oacc_ref[...] = jnp.zeros(oacc_ref.shape, F32)

        def score_pass(off, kblk, own_chunk):
            sel = keys_ref[pl.ds(r0, th), pl.ds(off, kb_w)] >= thr_part
            sel4 = jnp.concatenate([sel] * B_GROUP, axis=0)
            pblk = posf_ref[pl.ds(off, kb_w), :]
            if own_chunk:
                ahead = jnp.maximum((off + lane_k) - qpos, 0).astype(F32)
                ahead4 = jnp.concatenate([ahead] * B_GROUP, axis=0)
            for n in range(B_KV_HEADS):
                kaug = jnp.where(own_half[n], kblk, pblk)
                s = _dot_nt(qaug[n], kaug, preferred_element_type=F32)
                if own_chunk:
                    s = s - slope2[n] * ahead4
                s = jnp.where(sel4, s, NEG_BIG)
                s_ref[n, :, pl.ds(off, kb_w)] = s
                macc_ref[n] = jnp.maximum(macc_ref[n], fold_lanes(s, jnp.maximum))

        for_blocks(score_pass, *k_blocks)

        for n in range(B_KV_HEADS):
            m = jnp.max(macc_ref[n], axis=1, keepdims=True)
            macc_ref[n] = jnp.broadcast_to(m, (rows, 128))

        ones_blk = jnp.ones((kb_w, 128), BF16)

        def value_pass(off, vblk, _):
            vaug = jnp.concatenate([vblk, ones_blk], axis=1)
            for n in range(B_KV_HEADS):
                m_b = macc_ref[n]
                p = jnp.exp(s_ref[n, :, pl.ds(off, kb_w)]
                            - jnp.concatenate([m_b] * (kb_w // 128), axis=1))
                oacc_ref[n] = oacc_ref[n] + jnp.dot(p.astype(BF16), vaug, preferred_element_type=F32)

        for_blocks(value_pass, *v_blocks)
        pieces = []
        for n in range(B_KV_HEADS):
            acc = oacc_ref[n]
            o_n = acc[:, n * B_HEAD_DIM:(n + 1) * B_HEAD_DIM] / acc[:, 128:129]
            pieces += [o_n[g * th:(g + 1) * th] for g in range(B_GROUP)]
        gd = gd_ref[pl.ds(r0, th), :]
        o_ref[pl.ds(r0, th), :] = (jnp.concatenate(pieces, axis=1) * (gd * _sigmoid(gd))).astype(BF16)

    for_parts(attend_part)


def _dsa(proj, bsz, t_len, cache=None):
    kb_w = KEY_BLOCK
    pos0 = 0 if cache is None else cache[0].shape[1]
    tq = next(c for c in (8 * CHUNK, 4 * CHUNK, 2 * CHUNK, CHUNK) if t_len % c == 0 and pos0 % c == 0)
    th = min(tq, 2 * CHUNK)
    nq = t_len // tq
    n_cache = pos0 // kb_w
    s_pad = pos0 + kb_w if cache is not None else t_len
    assert s_pad % kb_w == 0 and pos0 % kb_w == 0 and kb_w % tq == 0
    assert cache is None or t_len == tq
    rows = B_GROUP * th
    proj3 = proj.reshape(bsz, t_len, N_PAD)
    key_cols = (C_KD // 128, C_VD // 128, C_KIWI // 128)
    if cache is None:
        key_specs = [pl.BlockSpec((1, s_pad, 128), functools.partial(lambda b, t, c: (b, 0, c), c=c))
                     for c in key_cols]
        key_args = (proj3, proj3, proj3)
    else:
        key_specs = [pl.BlockSpec((1, tq, 128), functools.partial(lambda b, t, c: (b, t, c), c=c))
                     for c in key_cols]
        key_specs += [pl.BlockSpec((1, pos0, a.shape[2]), lambda b, t: (b, 0, 0)) for a in cache]
        key_args = (proj3, proj3, proj3) + tuple(cache)
    kpos = lax.broadcasted_iota(I32, (s_pad, 128), 0)
    feat = lax.broadcasted_iota(I32, (s_pad, 128), 1)
    feat = feat % B_HEAD_DIM
    posf = jnp.where(feat == 0, kpos // CHUNK, jnp.where(feat == 1, kpos % CHUNK,
                     jnp.where(feat < 4, 1, 0))).astype(BF16)
    return pl.pallas_call(
        functools.partial(_dsa_kernel, pos0=pos0, kb_w=kb_w, tq=tq, th=th, n_cache=n_cache,
                          topk=min(MAX_TOPK, (pos0 + t_len) // 4)),
        grid=(bsz, nq),
        in_specs=[
            pl.BlockSpec((tq, 512), lambda b, t: (b * nq + t, C_Q // 512)),
            pl.BlockSpec((tq, 512), lambda b, t: (b * nq + t, C_GD // 512)),
            pl.BlockSpec((tq, 256), lambda b, t: (b * nq + t, C_QI // 256)),
            pl.BlockSpec((tq, 128), lambda b, t: (b * nq + t, C_KIWI // 128)),
            *key_specs,
            pl.BlockSpec((s_pad, 128), lambda b, t: (0, 0)),
        ],
        out_specs=pl.BlockSpec((tq, B_WIDTH), lambda b, t: (b * nq + t, 0)),
        out_shape=jax.ShapeDtypeStruct((bsz * t_len, B_WIDTH), BF16),
        scratch_shapes=[
            pltpu.VMEM((tq, s_pad), F32),
            pltpu.VMEM((B_KV_HEADS, rows, s_pad), F32),
            pltpu.VMEM((B_KV_HEADS, rows, 128), F32),
            pltpu.VMEM((B_KV_HEADS, rows, 256), F32),
            pltpu.VMEM((tq, 128), F32),
        ],
        compiler_params=pltpu.CompilerParams(
            dimension_semantics=("arbitrary", "arbitrary"), vmem_limit_bytes=VMEM_LIMIT),
        name="dsa",
    )(proj, proj, proj, proj, *key_args, posf)


def _merge_kernel(x_ref, ya_ref, yb_ref, gab_ref, wpa_ref, wpb_ref, wo_ref, fnw_ref, o_ref):
    pa = jnp.dot(ya_ref[...], wpa_ref[...], preferred_element_type=F32)
    pb = jnp.dot(yb_ref[...], wpb_ref[...], preferred_element_type=F32)
    merged = _sigmoid(gab_ref[:, 0:D_MODEL]) * pa + _sigmoid(gab_ref[:, D_MODEL:2 * D_MODEL]) * pb
    out = x_ref[...] + jnp.dot(merged.astype(BF16), wo_ref[...], preferred_element_type=F32)
    ms = jnp.mean(out * out, axis=-1, keepdims=True)
    o_ref[...] = (out * lax.rsqrt(ms + NORM_EPS)) * fnw_ref[...]


def _merge(x2d, ya, yb, proj, w_pa, w_pb, w_o, final_w):
    m = x2d.shape[0]
    tm = min(1024, m)
    full = lambda shape: pl.BlockSpec(shape, lambda i: (0, 0))
    return pl.pallas_call(
        _merge_kernel,
        grid=(m // tm,),
        in_specs=[
            pl.BlockSpec((tm, D_MODEL), lambda i: (i, 0)),
            pl.BlockSpec((tm, A_WIDTH), lambda i: (i, 0)),
            pl.BlockSpec((tm, B_WIDTH), lambda i: (i, 0)),
            pl.BlockSpec((tm, 2048), lambda i: (i, C_GATES // 2048)),
            full((A_WIDTH, D_MODEL)), full((B_WIDTH, D_MODEL)), full((D_MODEL, D_MODEL)),
            full((1, D_MODEL)),
        ],
        out_specs=pl.BlockSpec((tm, D_MODEL), lambda i: (i, 0)),
        out_shape=jax.ShapeDtypeStruct((m, D_MODEL), F32),
        compiler_params=pltpu.CompilerParams(
            dimension_semantics=("arbitrary",), vmem_limit_bytes=VMEM_LIMIT),
        name="merge",
    )(x2d, ya, yb, proj, w_pa.astype(BF16), w_pb.astype(BF16), w_o.astype(BF16),
      final_w.reshape(1, D_MODEL))


def _rwkv_order(row):
    return row[..., 0:2048], row[..., 2048:2176]


def _mixer(x, shift_prev, wkv_prev, past_k, past_v, past_ki, w_perm, norm_w, prm, w_pa, w_pb, w_o,
           final_w):
    bsz, t_len, _ = x.shape
    x2d = x.reshape(bsz * t_len, D_MODEL)
    proj = _proj(x2d, norm_w, w_perm)
    proj3 = proj.reshape(bsz, t_len, N_PAD)

    shift4, shiftw = _rwkv_order(shift_prev)
    ya, wkv_new = _rwkv(proj, bsz, t_len, shift4, shiftw, wkv_prev, prm)

    k_new = proj3[:, :, C_KD:C_KD + 128]
    v_new = proj3[:, :, C_VD:C_VD + 128]
    ki_new = proj3[:, :, C_KIWI:C_KIWI + IDX_DIM]
    past_len = 0 if past_k is None else past_k.shape[1]
    if past_len == 0:
        yb = _dsa(proj, bsz, t_len)
    else:
        yb = _dsa(proj, bsz, t_len, cache=(past_k.reshape(bsz, past_len, 128),
                                           past_v.reshape(bsz, past_len, 128), past_ki))

    y = _merge(x2d, ya, yb, proj, w_pa, w_pb, w_o, final_w).reshape(bsz, t_len, D_MODEL)
    last = proj3[:, t_len - 1:t_len, :]
    shift_new = jnp.concatenate([last[..., C_RKVG:C_RKVG + 2048], last[..., C_WDAD:C_WDAD + 128]], axis=-1)
    kv_shape = (bsz, t_len, B_KV_HEADS, B_HEAD_DIM)
    return y, k_new.reshape(kv_shape), v_new.reshape(kv_shape), ki_new, wkv_new, shift_new


def kernel(x_prompt, x_sample, cache_k, cache_v, cache_kidx, state_wkv, state_shift, norm_w, w_in,
           shift_mu, decay_w0, decay_up, iclr_a0, iclr_up, k_k, k_a, r_k, gn_w, gn_b, w_pa, w_pb,
           w_o, final_norm_w):
    assert w_in.shape[0] == 1, "the final norm is fused into the (single) layer's merge kernel"
    bp = x_prompt.shape[0]
    w_perm = _permute_w_in(w_in[0])
    mu4, muw = _rwkv_order(shift_mu[0].reshape(1, RWKV_COLS))
    row = lambda a: a.reshape(1, A_WIDTH)
    prm = dict(mu4=mu4, muw=muw, w0=row(decay_w0[0]), dup=decay_up[0], a0=row(iclr_a0[0]),
               aup=iclr_up[0], kk=row(k_k[0]), ka=row(k_a[0]), rk=row(r_k[0]), gnw=row(gn_w[0]),
               gnb=row(gn_b[0]))
    common = (w_perm, norm_w[0], prm, w_pa[0], w_pb[0], w_o[0], final_norm_w)
    yp, kp, vp, kip, wkvp, shp = _mixer(
        x_prompt, jnp.zeros((bp, 1, RWKV_COLS), F32),
        jnp.zeros((bp, A_HEADS, A_HEAD_DIM, A_HEAD_DIM), F32), None, None, None, *common)
    ys, ks, vs, kis, wkvs, shs = _mixer(
        x_sample, state_shift[0], state_wkv[0], cache_k[0], cache_v[0], cache_kidx[0], *common)
    st = lambda a: a[None]
    return (yp, ys, st(kp), st(vp), st(kip), st(wkvp), st(shp),
            st(ks), st(vs), st(kis), st(wkvs), st(shs))
```

```python
import functools

import jax
import jax.numpy as jnp
from jax import lax
from jax.experimental import pallas as pl
from jax.experimental.pallas import tpu as pltpu

F32 = jnp.float32
BF16 = jnp.bfloat16
I32 = jnp.int32
HIGHEST = lax.Precision.HIGHEST

D_MODEL = 1024
CHUNK = 64
A_HEADS = 8
A_HEAD_DIM = 64
A_WIDTH = 512
LORA = 64
RWKV_COLS = 4 * A_WIDTH + 2 * LORA
B_HEADS = 8
B_KV_HEADS = 2
B_GROUP = 4
B_HEAD_DIM = 64
B_WIDTH = 512
B_KV_WIDTH = 128
IDX_HEADS = 8
IDX_DIM = 32
MAX_TOPK = 256
DSA_COLS = 1576
N_IN = 5800
NORM_EPS = 1e-6
GN_EPS = 64e-5

C_GATES = 0
C_RKVG = 2048
C_Q = 4096
C_GD = 4608
C_QI = 5120
C_KD = 5376
C_VD = 5504
C_KIWI = 5632
C_WDAD = 5760
N_PAD = 5888

KEY_BLOCK = 512
BLOCK_GROUPS = (4, 2, 1)
INT_MIN = -(2 ** 31)
LOWEST_FINITE_KEY = INT_MIN + 0x00800000
NEG_BIG = -1e30
VMEM_LIMIT = 56 * 1024 * 1024


def _sigmoid(x):
    return 1.0 / (1.0 + jnp.exp(-x))


def _dot_nt(a, b, **kw):
    return lax.dot_general(a, b, (((1,), (1,)), ((), ())), **kw)


def _dot_tn(a, b, **kw):
    return lax.dot_general(a, b, (((0,), (0,)), ((), ())), **kw)


def _permute_w_in(w):
    d0 = RWKV_COLS
    g0 = RWKV_COLS + DSA_COLS
    cols = [
        w[:, g0:g0 + 2048],
        w[:, 0:2048],
        w[:, d0:d0 + 512],
        w[:, d0 + 1064:d0 + 1576],
        w[:, d0 + 768:d0 + 1024],
        w[:, d0 + 512:d0 + 640],
        w[:, d0 + 640:d0 + 768],
        w[:, d0 + 1024:d0 + 1064],
        jnp.zeros((w.shape[0], 128 - IDX_DIM - IDX_HEADS), w.dtype),
        w[:, 2048:2176],
    ]
    return jnp.concatenate(cols, axis=1).astype(BF16)


def _proj_kernel(x_ref, nw_ref, w_ref, o_ref):
    x = x_ref[...]
    ms = jnp.mean(x * x, axis=-1, keepdims=True)
    h = (x * lax.rsqrt(ms + NORM_EPS)) * nw_ref[...]
    o_ref[...] = jnp.dot(h.astype(BF16), w_ref[...], preferred_element_type=F32)


def _proj(x2d, norm_w, w_perm):
    m = x2d.shape[0]
    tm = min(1024, max(m // 4, 8))
    tn = N_PAD // 2
    return pl.pallas_call(
        _proj_kernel,
        grid=(N_PAD // tn, m // tm),
        in_specs=[
            pl.BlockSpec((tm, D_MODEL), lambda j, i: (i, 0)),
            pl.BlockSpec((1, D_MODEL), lambda j, i: (0, 0)),
            pl.BlockSpec((D_MODEL, tn), lambda j, i: (0, j)),
        ],
        out_specs=pl.BlockSpec((tm, tn), lambda j, i: (i, j)),
        out_shape=jax.ShapeDtypeStruct((m, N_PAD), F32),
        compiler_params=pltpu.CompilerParams(
            dimension_semantics=("arbitrary", "arbitrary"), vmem_limit_bytes=VMEM_LIMIT),
        name="proj",
    )(x2d, norm_w.reshape(1, D_MODEL), w_perm)


def _split_bf16(x, terms):
    pieces = []
    for _ in range(terms):
        piece = x.astype(BF16)
        pieces.append(piece)
        x = x - piece.astype(F32)
    return pieces


def _dot_exact_rhs(a, b_bf16, terms):
    acc = None
    for piece in _split_bf16(a, terms):
        d = jnp.dot(piece, b_bf16, preferred_element_type=F32)
        acc = d if acc is None else acc + d
    return acc


def _dot_3pass(a, b):
    ah, al = _split_bf16(a, 2)
    bh, bl = _split_bf16(b, 2)
    dot = functools.partial(jnp.dot, preferred_element_type=F32)
    return dot(ah, bh) + dot(ah, bl) + dot(al, bh)


def _rwkv_kernel(p4_ref, pw_ref, sp4_ref, spw_ref, s0_ref, mu4_ref, muw_ref, w0_ref, dup_ref,
                 a0_ref, aup_ref, kk_ref, ka_ref, rk_ref, gnw_ref, gnb_ref, bd_ref, tri_ref,
                 ya_ref, sout_ref, c4_ref, cw_ref, sbd_ref, *, nch):
    c = CHUNK
    n = A_HEAD_DIM
    rows = nch * c
    n_pairs = A_HEADS // 2
    t_idx = pl.program_id(1)
    lane = lax.broadcasted_iota(I32, (c, 2 * n), 1)
    lo_half = lane < n
    row_c = lax.broadcasted_iota(I32, (c, 2 * n), 0)
    pos_in_head = jnp.where(lo_half, lane, lane - n)
    tri_strict = (pos_in_head < row_c).astype(F32)
    tri_incl = (pos_in_head <= row_c).astype(F32)
    eye2 = (pos_in_head == row_c).astype(F32)
    lane_sq = lax.broadcasted_iota(I32, (2 * n, 2 * n), 1)
    row_sq = lax.broadcasted_iota(I32, (2 * n, 2 * n), 0)
    same_head = (lane_sq < n) == (row_sq < n)
    dot = functools.partial(jnp.dot, preferred_element_type=F32)

    def bdiag(x):
        zero = jnp.zeros_like(x)
        return jnp.concatenate([jnp.where(lo_half, x, zero), jnp.where(lo_half, zero, x)], axis=0)

    @pl.when(t_idx == 0)
    def _():
        c4_ref[...] = sp4_ref[0]
        cw_ref[...] = spw_ref[0]
        zeros = jnp.zeros((n, n), F32)
        for j in range(n_pairs):
            sbd_ref[j] = jnp.concatenate(
                [jnp.concatenate([s0_ref[0, 2 * j], zeros], axis=1),
                 jnp.concatenate([zeros, s0_ref[0, 2 * j + 1]], axis=1)], axis=0)

    p4 = p4_ref[...]
    pw = pw_ref[...]
    row = lax.broadcasted_iota(I32, (rows, 1), 0)
    prev4 = jnp.where(row == 0, c4_ref[...], pltpu.roll(p4, 1, 0))
    prevw = jnp.where(row == 0, cw_ref[...], pltpu.roll(pw, 1, 0))
    c4_ref[...] = p4[rows - 1:rows, :]
    cw_ref[...] = pw[rows - 1:rows, :]
    ps4 = p4 + (prev4 - p4) * mu4_ref[...]
    psw = pw + (prevw - pw) * muw_ref[...]
    r = ps4[:, 0:512]
    k = ps4[:, 512:1024]
    v = ps4[:, 1024:1536]
    g = ps4[:, 1536:2048]
    wd = psw[:, 0:LORA]
    ad = psw[:, LORA:2 * LORA]
    bd = bd_ref[...]

    xw = w0_ref[...] + _dot_3pass(jnp.tanh(wd), dup_ref[...])
    z = -xw
    softplus = jnp.maximum(z, 0.0) + jnp.log(1.0 + jnp.exp(-jnp.abs(z)))
    lw = -jnp.exp(-softplus - 0.5)
    a = _sigmoid(a0_ref[...] + _dot_3pass(ad, aup_ref[...]))
    kkr = k * kk_ref[...]
    kkn = kkr / jnp.maximum(jnp.sqrt(_dot_exact_rhs(kkr * kkr, bd, 2)), 1e-12)
    kmod = k * (1.0 + (a - 1.0) * ka_ref[...])

    cum = None
    for piece in _split_bf16(lw, 3):
        d = dot(tri_ref[...], piece)
        cum = d if cum is None else cum + d
    pdec = jnp.exp(cum)
    pinv = jnp.exp(-cum)
    rt = (r * pdec).astype(BF16)
    at = (-kkn * jnp.exp(cum - lw)).astype(BF16)
    bt = (kkn * a * pinv).astype(BF16)
    kt = (kmod * pinv).astype(BF16)
    vb = v.astype(BF16)

    tiles = [(ci, j) for ci in range(nch) for j in range(n_pairs)]
    rsl = lambda ci: slice(ci * c, (ci + 1) * c)
    lsl = lambda j: slice(j * 2 * n, (j + 1) * 2 * n)
    lhs, bk, a_ak_rk, a_rb, xs, tinv = {}, {}, {}, {}, {}, {}
    for ci, j in tiles:
        rs, ls = rsl(ci), lsl(j)
        lhs[ci, j] = jnp.concatenate([at[rs, ls], rt[rs, ls]], axis=0)
        bk[ci, j] = jnp.concatenate([bt[rs, ls], kt[rs, ls]], axis=0)
        amat = _dot_nt(lhs[ci, j], jnp.concatenate([bdiag(bt[rs, ls]), bdiag(kt[rs, ls])], axis=0),
                       preferred_element_type=F32)
        xs[ci, j] = amat[0:c, 0:2 * n] * tri_strict
        a_ak_rk[ci, j] = jnp.concatenate([amat[0:c, 2 * n:4 * n] * tri_strict,
                                          amat[c:2 * c, 2 * n:4 * n] * tri_incl], axis=0).astype(BF16)
        a_rb[ci, j] = (amat[c:2 * c, 0:2 * n] * tri_incl).astype(BF16)
        tinv[ci, j] = eye2 + xs[ci, j]
    for t in tiles:
        xb = xs[t].astype(BF16)
        xs[t] = dot(xb, bdiag(xb))
    for _ in range(4):
        for t in tiles:
            xb = xs[t].astype(BF16)
            both = dot(xb, jnp.concatenate([bdiag(tinv[t].astype(BF16)), bdiag(xb)], axis=1))
            tinv[t] = tinv[t] + both[:, 0:2 * n]
            xs[t] = both[:, 2 * n:4 * n]
    for t in tiles:
        tinv[t] = tinv[t] + dot(xs[t].astype(BF16), bdiag(tinv[t].astype(BF16)))
    akv = {}
    for ci, j in tiles:
        tinv[ci, j] = tinv[ci, j].astype(BF16)
        akv[ci, j] = dot(a_ak_rk[ci, j], bdiag(vb[rsl(ci), lsl(j)]))

    pairs = range(n_pairs)
    s_pair = [sbd_ref[j] for j in pairs]
    y_chunks = []
    for ci in range(nch):
        from_state = [_dot_nt(lhs[ci, j], s_pair[j].astype(BF16), preferred_element_type=F32)
                      for j in pairs]
        u = [dot(tinv[ci, j], bdiag((from_state[j][0:c] + akv[ci, j][0:c]).astype(BF16)))
             for j in pairs]
        uv_t = [jnp.transpose(jnp.concatenate([u[j], v[rsl(ci), lsl(j)]], axis=0)).astype(BF16)
                for j in pairs]
        upd = [dot(uv_t[j], bk[ci, j]) for j in pairs]
        s_pair = [(s_pair[j] + jnp.where(same_head, upd[j], 0.0))
                  * pdec[(ci + 1) * c - 1:(ci + 1) * c, lsl(j)] for j in pairs]
        y_chunks.append(jnp.concatenate(
            [from_state[j][c:2 * c] + akv[ci, j][c:2 * c] + dot(a_rb[ci, j], bdiag(u[j].astype(BF16)))
             for j in pairs], axis=1))
    for j in pairs:
        sbd_ref[j] = s_pair[j]
    y = y_chunks[0] if nch == 1 else jnp.concatenate(y_chunks, axis=0)

    inv_n = 1.0 / n
    mean = _dot_exact_rhs(y, bd, 2) * inv_n
    dlt = y - mean
    var = _dot_exact_rhs(dlt * dlt, bd, 2) * inv_n
    yn = dlt * lax.rsqrt(var + GN_EPS) * gnw_ref[...] + gnb_ref[...]
    yn = yn + _dot_exact_rhs(r * kmod * rk_ref[...], bd, 2) * v
    ya_ref[...] = (yn * (g * _sigmoid(g))).astype(BF16)

    @pl.when(t_idx == pl.num_programs(1) - 1)
    def _():
        for j in range(n_pairs):
            s_pair = sbd_ref[j]
            sout_ref[0, 2 * j] = s_pair[0:n, 0:n]
            sout_ref[0, 2 * j + 1] = s_pair[n:2 * n, n:2 * n]


def _rwkv(proj, bsz, t_len, shift4, shiftw, wkv_prev, prm):
    nch = min(4, t_len // CHUNK)
    rows = nch * CHUNK
    nt = t_len // rows
    row1 = lambda width: pl.BlockSpec((1, width), lambda b, t: (0, 0))
    head = lax.broadcasted_iota(I32, (A_WIDTH, A_WIDTH), 0) // A_HEAD_DIM
    bd = (head == head.T).astype(BF16)
    ti = lax.broadcasted_iota(I32, (rows, rows), 0)
    si = lax.broadcasted_iota(I32, (rows, rows), 1)
    tri = ((ti // CHUNK == si // CHUNK) & (si <= ti)).astype(BF16)
    return pl.pallas_call(
        functools.partial(_rwkv_kernel, nch=nch),
        grid=(bsz, nt),
        in_specs=[
            pl.BlockSpec((rows, 2048), lambda b, t: (b * nt + t, C_RKVG // 2048)),
            pl.BlockSpec((rows, 128), lambda b, t: (b * nt + t, C_WDAD // 128)),
            pl.BlockSpec((1, 1, 2048), lambda b, t: (b, 0, 0)),
            pl.BlockSpec((1, 1, 128), lambda b, t: (b, 0, 0)),
            pl.BlockSpec((1, A_HEADS, A_HEAD_DIM, A_HEAD_DIM), lambda b, t: (b, 0, 0, 0)),
            row1(2048), row1(128), row1(A_WIDTH),
            pl.BlockSpec((LORA, A_WIDTH), lambda b, t: (0, 0)),
            row1(A_WIDTH),
            pl.BlockSpec((LORA, A_WIDTH), lambda b, t: (0, 0)),
            row1(A_WIDTH), row1(A_WIDTH), row1(A_WIDTH), row1(A_WIDTH), row1(A_WIDTH),
            pl.BlockSpec((A_WIDTH, A_WIDTH), lambda b, t: (0, 0)),
            pl.BlockSpec((rows, rows), lambda b, t: (0, 0)),
        ],
        out_specs=[
            pl.BlockSpec((rows, A_WIDTH), lambda b, t: (b * nt + t, 0)),
            pl.BlockSpec((1, A_HEADS, A_HEAD_DIM, A_HEAD_DIM), lambda b, t: (b, 0, 0, 0)),
        ],
        out_shape=[
            jax.ShapeDtypeStruct((bsz * t_len, A_WIDTH), BF16),
            jax.ShapeDtypeStruct((bsz, A_HEADS, A_HEAD_DIM, A_HEAD_DIM), F32),
        ],
        scratch_shapes=[pltpu.VMEM((1, 2048), F32), pltpu.VMEM((1, 128), F32),
                        pltpu.VMEM((A_HEADS // 2, 2 * A_HEAD_DIM, 2 * A_HEAD_DIM), F32)],
        compiler_params=pltpu.CompilerParams(
            dimension_semantics=("arbitrary", "arbitrary"), vmem_limit_bytes=VMEM_LIMIT),
        name="rwkv",
    )(proj, proj, shift4, shiftw, wkv_prev, prm["mu4"], prm["muw"], prm["w0"], prm["dup"],
      prm["a0"], prm["aup"], prm["kk"], prm["ka"], prm["rk"], prm["gnw"], prm["gnb"], bd, tri)


def _dsa_kernel(q_ref, gd_ref, qi_ref, kiwi_ref, k_ref, v_ref, ki_ref, *rest, pos0, kb_w, topk, tq, th,
                n_cache):
    if n_cache:
        kc_ref, vc_ref, kic_ref, *rest = rest
    posf_ref, o_ref, keys_ref, s_ref, macc_ref, oacc_ref, row_ref = rest
    qt = pl.program_id(1)
    tile_pos = pos0 + qt * tq
    row_chunk = jnp.right_shift(lax.broadcasted_iota(I32, (tq, 1), 0), 6)
    n_adm = tile_pos + (row_chunk + 1) * CHUNK
    nkb = n_cache + 1 if n_cache else (tile_pos + tq + kb_w - 1) // kb_w
    idx_scale = (IDX_HEADS ** -0.5) * (IDX_DIM ** -0.5)
    att_scale = B_HEAD_DIM ** -0.5
    topk = float(topk)
    lane_q128 = lax.broadcasted_iota(I32, (1, 128), 1)
    lane_k = lax.broadcasted_iota(I32, (1, kb_w), 1)

    def loaders(all_ref, cache_ref):
        if n_cache:
            def new_rows():
                new = all_ref[0].astype(BF16)
                return jnp.concatenate([new, jnp.zeros((kb_w - tq, new.shape[1]), BF16)], axis=0)
            return (lambda off: cache_ref[0, pl.ds(off, kb_w), :].astype(BF16)), new_rows
        at = lambda off: all_ref[0, pl.ds(off, kb_w), :].astype(BF16)
        return at, (lambda: at(last_off))

    last_off = n_cache * kb_w if n_cache else pl.multiple_of((nkb - 1) * kb_w, kb_w)

    def for_blocks(body, block_at, last_block):
        def group_step(width, first):
            def step(j, carry):
                offs = [pl.multiple_of((first + width * j + i) * kb_w, kb_w) for i in range(width)]
                blocks = [block_at(off) for off in offs]
                for off, block in zip(offs, blocks):
                    body(off, block, False)
                return carry
            return step

        done = 0
        for width in BLOCK_GROUPS:
            trips = (nkb - 1 - done) // width
            lax.fori_loop(0, trips, group_step(width, done), 0)
            done = done + trips * width
        body(last_off, last_block(), True)

    def for_parts(fn):
        def step(p, carry):
            fn(pl.multiple_of(p * th, th))
            return carry

        lax.fori_loop(0, tq // th, step, 0)

    def part_chunk(r0):
        return jnp.right_shift(r0 + lax.broadcasted_iota(I32, (th, 1), 0), 6)

    k_blocks = loaders(k_ref, kc_ref if n_cache else None)
    v_blocks = loaders(v_ref, vc_ref if n_cache else None)
    ki_blocks = loaders(ki_ref, kic_ref if n_cache else None)

    def index_part(r0):
        qi = qi_ref[pl.ds(r0, th), :]
        wi = kiwi_ref[pl.ds(r0, th), IDX_DIM:IDX_DIM + IDX_HEADS]
        heads = []
        for h in range(IDX_HEADS):
            slab = qi[:, 128 * (h // 4):128 * (h // 4) + 128]
            if h % 4:
                slab = pltpu.roll(slab, 128 - IDX_DIM * (h % 4), 1)
            heads.append(jnp.where(lane_q128 < IDX_DIM, slab, 0.0))
        qis = jnp.concatenate(heads, axis=0).astype(BF16)
        wis = jnp.concatenate([wi[:, h:h + 1] for h in range(IDX_HEADS)], axis=0)
        n_adm_part = tile_pos + (part_chunk(r0) + 1) * CHUNK

        def score_block(off, kir, _):
            s = _dot_nt(qis[:, 0:kir.shape[1]], kir, preferred_element_type=F32)
            s = jnp.maximum(s, 0.0) * wis
            isc = s[0:th]
            for h in range(1, IDX_HEADS):
                isc = isc + s[h * th:(h + 1) * th]
            isc = jnp.where(off + lane_k < n_adm_part, isc * idx_scale, -jnp.inf)
            keys_ref[pl.ds(r0, th), pl.ds(off, kb_w)] = isc
            hits = row_ref[pl.ds(r0, th), :]
            for j in range(kb_w // 128):
                hits = hits + jnp.where(isc[:, j * 128:(j + 1) * 128] >= 0.0, 1.0, 0.0)
            row_ref[pl.ds(r0, th), :] = hits

        row_ref[pl.ds(r0, th), :] = jnp.zeros((th, 128), F32)
        for_blocks(score_block, *ki_blocks)

    for_parts(index_part)

    def key_to_score(key):
        return pltpu.bitcast(jnp.where(key < 0, key ^ 0x7FFFFFFF, key), F32)

    def count_ge(cand_key):
        accs = []
        for r0 in range(0, tq, th):
            cand_part = key_to_score(cand_key[r0:r0 + th])

            def body(kb, acc, r0=r0, cand_part=cand_part):
                off = pl.multiple_of(kb * kb_w, kb_w)
                kblk = keys_ref[r0:r0 + th, pl.ds(off, kb_w)]
                for j in range(kb_w // 128):
                    acc = acc + jnp.where(kblk[:, j * 128:(j + 1) * 128] >= cand_part, 1.0, 0.0)
                return acc

            accs.append(lax.fori_loop(0, nkb, body, jnp.zeros((th, 128), F32)))
        acc = accs[0] if len(accs) == 1 else jnp.concatenate(accs, axis=0)
        return jnp.sum(acc, axis=1, keepdims=True)

    c0 = jnp.sum(row_ref[...], axis=1, keepdims=True)
    t0 = jnp.where(c0 >= topk, jnp.zeros((tq, 128), I32), jnp.full((tq, 128), INT_MIN, I32))
    n0 = jnp.where(c0 >= topk, c0, n_adm.astype(F32))

    def bit_step(i, carry):
        t, n_t = carry
        cand = t | jnp.left_shift(jnp.int32(1), 30 - i)
        n_cand = count_ge(cand)
        take = n_cand >= topk
        return jnp.where(take, cand, t), jnp.where(take, n_cand, n_t)

    thr_key, n_ge = lax.fori_loop(0, 31, bit_step, (t0, n0))
    thr_key = jnp.maximum(thr_key, LOWEST_FINITE_KEY)
    row_ref[...] = key_to_score(thr_key)
    thr = row_ref[:, 0:1]

    @pl.when(jnp.max(n_ge) > topk)
    def _():
        n_tie_take = topk - count_ge(thr_key + 1)
        ri = lax.broadcasted_iota(I32, (kb_w, kb_w), 0)
        ci = lax.broadcasted_iota(I32, (kb_w, kb_w), 1)
        upper = (ri <= ci).astype(BF16)

        def body(kb, seen):
            off = pl.multiple_of(kb * kb_w, kb_w)
            kblk = keys_ref[:, pl.ds(off, kb_w)]
            tie = kblk == thr
            rank = seen + jnp.dot(jnp.where(tie, 1.0, 0.0).astype(BF16), upper,
                                  preferred_element_type=F32)
            keys_ref[:, pl.ds(off, kb_w)] = jnp.where(tie & (rank > n_tie_take), -jnp.inf, kblk)
            return rank[:, kb_w - 1:kb_w]

        lax.fori_loop(0, nkb, body, jnp.zeros((tq, 1), F32))

    rows = B_GROUP * th
    lane = lax.broadcasted_iota(I32, (th, 128), 1)

    def fold_lanes(x, op):
        part = x[:, 0:128]
        for j in range(1, kb_w // 128):
            part = op(part, x[:, j * 128:(j + 1) * 128])
        return part

    def attend_part(r0):
        q = q_ref[pl.ds(r0, th), :]
        thr_part = row_ref[pl.ds(r0, th), 0:1]
        q_chunk = (tile_pos // CHUNK + part_chunk(r0)).astype(F32)
        q_row = (lax.broadcasted_iota(I32, (th, 128), 0) & (CHUNK - 1)).astype(F32)
        qpos = tile_pos + r0 + lax.broadcasted_iota(I32, (th, 1), 0)
        qaug, slope2, own_half = [], [], []
        for n in range(B_KV_HEADS):
            q_parts, slope_parts = [], []
            keep = (lane >= n * B_HEAD_DIM) & (lane < (n + 1) * B_HEAD_DIM)
            own_half.append((lane_q128 >= n * B_HEAD_DIM) & (lane_q128 < (n + 1) * B_HEAD_DIM))
            pos_lane = lane - (1 - n) * B_HEAD_DIM
            for g in range(B_GROUP):
                h = n * B_GROUP + g
                slope = 2.0 ** (-(8.0 / B_HEADS) * (h + 1))
                slab = q[:, 128 * (h // 2):128 * (h // 2) + 128]
                if h % 2 != n:
                    slab = pltpu.roll(slab, B_HEAD_DIM, 1)
                pos_feat = jnp.where(pos_lane == 0, CHUNK * slope,
                           jnp.where(pos_lane == 1, slope,
                           jnp.where(pos_lane == 2, -CHUNK * slope * q_chunk,
                           jnp.where(pos_lane == 3, -slope * q_row, 0.0))))
                q_parts.append(jnp.where(keep, slab * att_scale, pos_feat))
                slope_parts.append(jnp.full((th, 1), 2.0 * slope, F32))
            qaug.append(jnp.concatenate(q_parts, axis=0).astype(BF16))
            slope2.append(jnp.concatenate(slope_parts, axis=0))

        macc_ref[...] = jnp.full(macc_ref.shape, NEG_BIG, F32)
        oacc_ref[...] = jnp.zeros(oacc_ref.shape, F32)

        def score_pass(off, kblk, own_chunk):
            sel = keys_ref[pl.ds(r0, th), pl.ds(off, kb_w)] >= thr_part
            sel4 = jnp.concatenate([sel] * B_GROUP, axis=0)
            pblk = posf_ref[pl.ds(off, kb_w), :]
            if own_chunk:
                ahead = jnp.maximum((off + lane_k) - qpos, 0).astype(F32)
                ahead4 = jnp.concatenate([ahead] * B_GROUP, axis=0)
            for n in range(B_KV_HEADS):
                kaug = jnp.where(own_half[n], kblk, pblk)
                s = _dot_nt(qaug[n], kaug, preferred_element_type=F32)
                if own_chunk:
                    s = s - slope2[n] * ahead4
                s = jnp.where(sel4, s, NEG_BIG)
                s_ref[n, :, pl.ds(off, kb_w)] = s
                macc_ref[n] = jnp.maximum(macc_ref[n], fold_lanes(s, jnp.maximum))

        for_blocks(score_pass, *k_blocks)

        for n in range(B_KV_HEADS):
            m = jnp.max(macc_ref[n], axis=1, keepdims=True)
            macc_ref[n] = jnp.broadcast_to(m, (rows, 128))

        ones_blk = jnp.ones((kb_w, 128), BF16)

        def value_pass(off, vblk, _):
            vaug = jnp.concatenate([vblk, ones_blk], axis=1)
            for n in range(B_KV_HEADS):
                m_b = macc_ref[n]
                p = jnp.exp(s_ref[n, :, pl.ds(off, kb_w)]
                            - jnp.concatenate([m_b] * (kb_w // 128), axis=1))
                oacc_ref[n] = oacc_ref[n] + jnp.dot(p.astype(BF16), vaug, preferred_element_type=F32)

        for_blocks(value_pass, *v_blocks)
        pieces = []
        for n in range(B_KV_HEADS):
            acc = oacc_ref[n]
            o_n = acc[:, n * B_HEAD_DIM:(n + 1) * B_HEAD_DIM] / acc[:, 128:129]
            pieces += [o_n[g * th:(g + 1) * th] for g in range(B_GROUP)]
        gd = gd_ref[pl.ds(r0, th), :]
        o_ref[pl.ds(r0, th), :] = (jnp.concatenate(pieces, axis=1) * (gd * _sigmoid(gd))).astype(BF16)

    for_parts(attend_part)


def _dsa(proj, bsz, t_len, cache=None):
    kb_w = KEY_BLOCK
    pos0 = 0 if cache is None else cache[0].shape[1]
    tq = next(c for c in (8 * CHUNK, 4 * CHUNK, 2 * CHUNK, CHUNK) if t_len % c == 0 and pos0 % c == 0)
    th = min(tq, 2 * CHUNK)
    nq = t_len // tq
    n_cache = pos0 // kb_w
    s_pad = pos0 + kb_w if cache is not None else t_len
    assert s_pad % kb_w == 0 and pos0 % kb_w == 0 and kb_w % tq == 0
    assert cache is None or t_len == tq
    rows = B_GROUP * th
    proj3 = proj.reshape(bsz, t_len, N_PAD)
    key_cols = (C_KD // 128, C_VD // 128, C_KIWI // 128)
    if cache is None:
        key_specs = [pl.BlockSpec((1, s_pad, 128), functools.partial(lambda b, t, c: (b, 0, c), c=c))
                     for c in key_cols]
        key_args = (proj3, proj3, proj3)
    else:
        key_specs = [pl.BlockSpec((1, tq, 128), functools.partial(lambda b, t, c: (b, t, c), c=c))
                     for c in key_cols]
        key_specs += [pl.BlockSpec((1, pos0, a.shape[2]), lambda b, t: (b, 0, 0)) for a in cache]
        key_args = (proj3, proj3, proj3) + tuple(cache)
    kpos = lax.broadcasted_iota(I32, (s_pad, 128), 0)
    feat = lax.broadcasted_iota(I32, (s_pad, 128), 1)
    feat = feat % B_HEAD_DIM
    posf = jnp.where(feat == 0, kpos // CHUNK, jnp.where(feat == 1, kpos % CHUNK,
                     jnp.where(feat < 4, 1, 0))).astype(BF16)
    return pl.pallas_call(
        functools.partial(_dsa_kernel, pos0=pos0, kb_w=kb_w, tq=tq, th=th, n_cache=n_cache,
                          topk=min(MAX_TOPK, (pos0 + t_len) // 4)),
        grid=(bsz, nq),
        in_specs=[
            pl.BlockSpec((tq, 512), lambda b, t: (b * nq + t, C_Q // 512)),
            pl.BlockSpec((tq, 512), lambda b, t: (b * nq + t, C_GD // 512)),
            pl.BlockSpec((tq, 256), lambda b, t: (b * nq + t, C_QI // 256)),
            pl.BlockSpec((tq, 128), lambda b, t: (b * nq + t, C_KIWI // 128)),
            *key_specs,
            pl.BlockSpec((s_pad, 128), lambda b, t: (0, 0)),
        ],
        out_specs=pl.BlockSpec((tq, B_WIDTH), lambda b, t: (b * nq + t, 0)),
        out_shape=jax.ShapeDtypeStruct((bsz * t_len, B_WIDTH), BF16),
        scratch_shapes=[
            pltpu.VMEM((tq, s_pad), F32),
            pltpu.VMEM((B_KV_HEADS, rows, s_pad), F32),
            pltpu.VMEM((B_KV_HEADS, rows, 128), F32),
            pltpu.VMEM((B_KV_HEADS, rows, 256), F32),
            pltpu.VMEM((tq, 128), F32),
        ],
        compiler_params=pltpu.CompilerParams(
            dimension_semantics=("arbitrary", "arbitrary"), vmem_limit_bytes=VMEM_LIMIT),
        name="dsa",
    )(proj, proj, proj, proj, *key_args, posf)


def _merge_kernel(x_ref, ya_ref, yb_ref, gab_ref, wpa_ref, wpb_ref, wo_ref, fnw_ref, o_ref):
    pa = jnp.dot(ya_ref[...], wpa_ref[...], preferred_element_type=F32)
    pb = jnp.dot(yb_ref[...], wpb_ref[...], preferred_element_type=F32)
    merged = _sigmoid(gab_ref[:, 0:D_MODEL]) * pa + _sigmoid(gab_ref[:, D_MODEL:2 * D_MODEL]) * pb
    out = x_ref[...] + jnp.dot(merged.astype(BF16), wo_ref[...], preferred_element_type=F32)
    ms = jnp.mean(out * out, axis=-1, keepdims=True)
    o_ref[...] = (out * lax.rsqrt(ms + NORM_EPS)) * fnw_ref[...]


def _merge(x2d, ya, yb, proj, w_pa, w_pb, w_o, final_w):
    m = x2d.shape[0]
    tm = min(1024, m)
    full = lambda shape: pl.BlockSpec(shape, lambda i: (0, 0))
    return pl.pallas_call(
        _merge_kernel,
        grid=(m // tm,),
        in_specs=[
            pl.BlockSpec((tm, D_MODEL), lambda i: (i, 0)),
            pl.BlockSpec((tm, A_WIDTH), lambda i: (i, 0)),
            pl.BlockSpec((tm, B_WIDTH), lambda i: (i, 0)),
            pl.BlockSpec((tm, 2048), lambda i: (i, C_GATES // 2048)),
            full((A_WIDTH, D_MODEL)), full((B_WIDTH, D_MODEL)), full((D_MODEL, D_MODEL)),
            full((1, D_MODEL)),
        ],
        out_specs=pl.BlockSpec((tm, D_MODEL), lambda i: (i, 0)),
        out_shape=jax.ShapeDtypeStruct((m, D_MODEL), F32),
        compiler_params=pltpu.CompilerParams(
            dimension_semantics=("arbitrary",), vmem_limit_bytes=VMEM_LIMIT),
        name="merge",
    )(x2d, ya, yb, proj, w_pa.astype(BF16), w_pb.astype(BF16), w_o.astype(BF16),
      final_w.reshape(1, D_MODEL))


def _rwkv_order(row):
    return row[..., 0:2048], row[..., 2048:2176]


def _mixer(x, shift_prev, wkv_prev, past_k, past_v, past_ki, w_perm, norm_w, prm, w_pa, w_pb, w_o,
           final_w):
    bsz, t_len, _ = x.shape
    x2d = x.reshape(bsz * t_len, D_MODEL)
    proj = _proj(x2d, norm_w, w_perm)
    proj3 = proj.reshape(bsz, t_len, N_PAD)

    shift4, shiftw = _rwkv_order(shift_prev)
    ya, wkv_new = _rwkv(proj, bsz, t_len, shift4, shiftw, wkv_prev, prm)

    k_new = proj3[:, :, C_KD:C_KD + 128]
    v_new = proj3[:, :, C_VD:C_VD + 128]
    ki_new = proj3[:, :, C_KIWI:C_KIWI + IDX_DIM]
    past_len = 0 if past_k is None else past_k.shape[1]
    if past_len == 0:
        yb = _dsa(proj, bsz, t_len)
    else:
        yb = _dsa(proj, bsz, t_len, cache=(past_k.reshape(bsz, past_len, 128),
                                           past_v.reshape(bsz, past_len, 128), past_ki))

    y = _merge(x2d, ya, yb, proj, w_pa, w_pb, w_o, final_w).reshape(bsz, t_len, D_MODEL)
    last = proj3[:, t_len - 1:t_len, :]
    shift_new = jnp.concatenate([last[..., C_RKVG:C_RKVG + 2048], last[..., C_WDAD:C_WDAD + 128]], axis=-1)
    kv_shape = (bsz, t_len, B_KV_HEADS, B_HEAD_DIM)
    return y, k_new.reshape(kv_shape), v_new.reshape(kv_shape), ki_new, wkv_new, shift_new


def kernel(x_prompt, x_sample, cache_k, cache_v, cache_kidx, state_wkv, state_shift, norm_w, w_in,
           shift_mu, decay_w0, decay_up, iclr_a0, iclr_up, k_k, k_a, r_k, gn_w, gn_b, w_pa, w_pb,
           w_o, final_norm_w):
    assert w_in.shape[0] == 1, "the final norm is fused into the (single) layer's merge kernel"
    bp = x_prompt.shape[0]
    w_perm = _permute_w_in(w_in[0])
    mu4, muw = _rwkv_order(shift_mu[0].reshape(1, RWKV_COLS))
    row = lambda a: a.reshape(1, A_WIDTH)
    prm = dict(mu4=mu4, muw=muw, w0=row(decay_w0[0]), dup=decay_up[0], a0=row(iclr_a0[0]),
               aup=iclr_up[0], kk=row(k_k[0]), ka=row(k_a[0]), rk=row(r_k[0]), gnw=row(gn_w[0]),
               gnb=row(gn_b[0]))
    common = (w_perm, norm_w[0], prm, w_pa[0], w_pb[0], w_o[0], final_norm_w)
    yp, kp, vp, kip, wkvp, shp = _mixer(
        x_prompt, jnp.zeros((bp, 1, RWKV_COLS), F32),
        jnp.zeros((bp, A_HEADS, A_HEAD_DIM, A_HEAD_DIM), F32), None, None, None, *common)
    ys, ks, vs, kis, wkvs, shs = _mixer(
        x_sample, state_shift[0], state_wkv[0], cache_k[0], cache_v[0], cache_kidx[0], *common)
    st = lambda a: a[None]
    return (yp, ys, st(kp), st(vp), st(kip), st(wkvp), st(shp),
            st(ks), st(vs), st(kis), st(wkvs), st(shs))
```
